```python
import jax, jax.numpy as jnp
from jax import lax
import numpy as np

D_MODEL = 1024
BATCH = 2
SEQ = 16384
DEPTH = 1

A_GROUPS = 4
A_GROUP_DIM = 128
A_WIDTH = A_GROUPS * A_GROUP_DIM
CHUNK = 128
N_HEADS = 4
HEAD_DIM = 128
B_WIDTH = N_HEADS * HEAD_DIM
KV_LATENT = 256
ROT_DIM = HEAD_DIM // 4
IDX_HEADS = 4
IDX_DIM = 64
IDX_ROT = IDX_DIM // 4
TOPK_MAX = 256
Q_BLOCK = 128
ROPE_THETA = 500000.0
N_BRANCHES = 2
PEER_HEADS = 8
PEER_KEY_DIM = 256
PEER_HALF = PEER_KEY_DIM // 2
N_KEYS = 128
N_EXPERTS = N_KEYS * N_KEYS
PEER_TOPK = 16
PEER_BLOCK = 128
EPS = 1e-6

SPLITS = (A_WIDTH, A_WIDTH, B_WIDTH, KV_LATENT, IDX_HEADS * IDX_DIM, IDX_DIM, IDX_HEADS, N_BRANCHES * D_MODEL)
IN_WIDTH = sum(SPLITS)
SPLIT_OFFSETS = tuple(int(o) for o in np.cumsum(SPLITS)[:-1])

kernel_name = "hybrid_gmlp_dsa_peer_block"


def rms_norm(x, g):
    xf = x.astype(jnp.float32)
    y = xf * lax.rsqrt(jnp.mean(xf * xf, axis=-1, keepdims=True) + EPS)
    return (y * g.astype(jnp.float32)).astype(x.dtype)


def layer_norm(x, g, b):
    xf = x.astype(jnp.float32)
    mu = jnp.mean(xf, axis=-1, keepdims=True)
    xc = xf - mu
    y = xc * lax.rsqrt(jnp.mean(xc * xc, axis=-1, keepdims=True) + EPS)
    return (y * g.astype(jnp.float32) + b.astype(jnp.float32)).astype(x.dtype)


def partial_rope(x, pos, rot_dim):
    half = rot_dim // 2
    inv = jnp.power(jnp.float32(ROPE_THETA), -jnp.arange(half, dtype=jnp.float32) * 2.0 / rot_dim)
    ang = pos.astype(jnp.float32)[..., None] * inv
    cos = jnp.cos(ang)[:, :, None, :]
    sin = jnp.sin(ang)[:, :, None, :]
    xr = x[..., :rot_dim].astype(jnp.float32)
    x1, x2 = xr[..., :half], xr[..., half:]
    rot = jnp.concatenate([x1 * cos - x2 * sin, x2 * cos + x1 * sin], axis=-1)
    return jnp.concatenate([rot.astype(x.dtype), x[..., rot_dim:]], axis=-1)


def chunked_spatial_gating(u, v, v_g, v_b, w_s, b_s):
    bsz, s, _ = u.shape
    u = jax.nn.gelu(u)
    v = layer_norm(jax.nn.gelu(v), v_g, v_b)
    v = v.reshape(bsz, s // CHUNK, CHUNK, A_GROUPS, A_GROUP_DIM)
    causal = jnp.tril(jnp.ones((CHUNK, CHUNK), dtype=bool))
    w = jnp.where(causal[None], w_s, jnp.zeros_like(w_s))
    z = jnp.einsum('gts,bcsgd->bctgd', w, v) + b_s.T[None, None, :, :, None]
    return u * z.reshape(bsz, s, A_WIDTH)


def dsa_attention(q, k, v, q_idx, k_idx, w_idx):
    bsz, s = q.shape[:2]
    n_sel = min(TOPK_MAX, s // 4)
    nblk = s // Q_BLOCK
    key_pos = jnp.arange(s)

    def to_blocks(a):
        return a.reshape(bsz, nblk, Q_BLOCK, *a.shape[2:]).swapaxes(0, 1)

    def gather_rows(table, idx):
        return table[idx]

    def block_fn(args):
        blk, qb, qib, wib = args
        q_pos = blk * Q_BLOCK + jnp.arange(Q_BLOCK)
        causal = key_pos[None, :] <= q_pos[:, None]
        logits = jnp.einsum('bqhd,bsd->bqhs', qib, k_idx, preferred_element_type=jnp.float32) * (IDX_DIM ** -0.5)
        scores = jnp.einsum('bqh,bqhs->bqs', wib.astype(jnp.float32), jax.nn.relu(logits))
        scores = jnp.where(causal[None], scores, -jnp.inf)
        _, idx = lax.top_k(scores, n_sel)
        valid = idx <= q_pos[None, :, None]
        k_sel = jax.vmap(gather_rows)(k, idx)
        v_sel = jax.vmap(gather_rows)(v, idx)
        att = jnp.einsum('bqhd,bqnd->bqhn', qb, k_sel, preferred_element_type=jnp.float32) * (HEAD_DIM ** -0.5)
        att = jnp.where(valid[:, :, None, :], att, jnp.float32(-1e30))
        p = jax.nn.softmax(att, axis=-1).astype(v.dtype)
        return jnp.einsum('bqhn,bqnd->bqhd', p, v_sel)

    out = lax.map(block_fn, (jnp.arange(nblk), to_blocks(q), to_blocks(q_idx), to_blocks(w_idx)))
    return out.swapaxes(0, 1).reshape(bsz, s, B_WIDTH)


def peer(xn, wq, subkeys, u_tab, v_tab):
    bsz, s, d = xn.shape
    q = (xn @ wq).reshape(bsz, s, PEER_HEADS, 2, PEER_HALF)
    sub = jnp.einsum('bshpd,hpnd->bshpn', q, subkeys, preferred_element_type=jnp.float32)
    s_top, i_top = lax.top_k(sub, PEER_TOPK)
    cand = (s_top[..., 0, :, None] + s_top[..., 1, None, :]).reshape(bsz, s, PEER_HEADS, PEER_TOPK * PEER_TOPK)
    cand_idx = (i_top[..., 0, :, None] * N_KEYS + i_top[..., 1, None, :]).reshape(bsz, s, PEER_HEADS, PEER_TOPK * PEER_TOPK)
    best, pos = lax.top_k(cand, PEER_TOPK)
    expert = jnp.take_along_axis(cand_idx, pos, axis=-1)
    gate = jax.nn.softmax(best, axis=-1)
    nb = (bsz * s) // PEER_BLOCK
    hk = PEER_HEADS * PEER_TOPK
    xt = xn.reshape(nb, PEER_BLOCK, d)
    et = expert.reshape(nb, PEER_BLOCK, hk)
    gt = gate.reshape(nb, PEER_BLOCK, hk)

    def blk(args):
        xb, eb, gb = args
        ub = u_tab[eb]
        vb = v_tab[eb]
        a = jax.nn.gelu(jnp.einsum('td,ted->te', xb, ub, preferred_element_type=jnp.float32))
        return jnp.einsum('te,ted->td', (gb * a).astype(vb.dtype), vb)

    y = lax.map(blk, (xt, et, gt))
    return y.reshape(bsz, s, d)


def setup_inputs(seed: int = 0) -> dict:
    key = jax.random.key(seed)
    ks = jax.random.split(key, 24)
    f32 = jnp.float32
    L = DEPTH
    nrm = lambda k, shape, scale: jax.random.normal(k, shape, f32) * scale
    x = jax.random.normal(ks[0], (BATCH, SEQ, D_MODEL), f32)
    positions = jnp.broadcast_to(jnp.arange(SEQ, dtype=jnp.int32)[None, :], (BATCH, SEQ))
    return {
        "x": x,
        "positions": positions,
        "norm1_g": 1.0 + nrm(ks[1], (L, D_MODEL), 0.02),
        "w_in": nrm(ks[2], (L, D_MODEL, IN_WIDTH), D_MODEL ** -0.5),
        "v_norm_g": 1.0 + nrm(ks[3], (L, A_WIDTH), 0.02),
        "v_norm_b": nrm(ks[4], (L, A_WIDTH), 0.02),
        "spatial_w": nrm(ks[5], (L, A_GROUPS, CHUNK, CHUNK), CHUNK ** -0.5),
        "spatial_b": 1.0 + nrm(ks[6], (L, A_GROUPS, CHUNK), 0.1),
        "kv_norm_g": 1.0 + nrm(ks[7], (L, KV_LATENT), 0.02),
        "w_uk": nrm(ks[8], (L, KV_LATENT, HEAD_DIM), KV_LATENT ** -0.5),
        "w_uv": nrm(ks[9], (L, KV_LATENT, HEAD_DIM), KV_LATENT ** -0.5),
        "q_norm_g": 1.0 + nrm(ks[10], (L, HEAD_DIM), 0.02),
        "k_norm_g": 1.0 + nrm(ks[11], (L, HEAD_DIM), 0.02),
        "w_a_out": nrm(ks[12], (L, A_WIDTH, D_MODEL), A_WIDTH ** -0.5),
        "w_b_out": nrm(ks[13], (L, B_WIDTH, D_MODEL), B_WIDTH ** -0.5),
        "w_o": nrm(ks[14], (L, D_MODEL, D_MODEL), D_MODEL ** -0.5),
        "norm2_g": 1.0 + nrm(ks[15], (L, D_MODEL), 0.02),
        "peer_wq": nrm(ks[16], (L, D_MODEL, PEER_HEADS * PEER_KEY_DIM), D_MODEL ** -0.5),
        "peer_subkeys": nrm(ks[17], (L, PEER_HEADS, 2, N_KEYS, PEER_HALF), PEER_HALF ** -0.5),
        "peer_u": nrm(ks[18], (L, N_EXPERTS, D_MODEL), D_MODEL ** -0.5),
        "peer_v": nrm(ks[19], (L, N_EXPERTS, D_MODEL), 0.5 * PEER_HEADS ** -0.5),
    }


def reference(x, positions, norm1_g, w_in, v_norm_g, v_norm_b, spatial_w, spatial_b, kv_norm_g, w_uk, w_uv,
              q_norm_g, k_norm_g, w_a_out, w_b_out, w_o, norm2_g, peer_wq, peer_subkeys, peer_u, peer_v):
    bsz, s, d = x.shape
    h = x
    for l in range(DEPTH):
        xn = rms_norm(h, norm1_g[l])
        proj = xn @ w_in[l]
        u, v, q, c_kv, q_idx, k_idx, w_idx, gates = jnp.split(proj, SPLIT_OFFSETS, axis=-1)

        y_a = chunked_spatial_gating(u, v, v_norm_g[l], v_norm_b[l], spatial_w[l], spatial_b[l])

        c_n = rms_norm(c_kv, kv_norm_g[l])
        k = rms_norm(c_n @ w_uk[l], k_norm_g[l])
        vv = c_n @ w_uv[l]
        qh = rms_norm(q.reshape(bsz, s, N_HEADS, HEAD_DIM), q_norm_g[l])
        qh = partial_rope(qh, positions, ROT_DIM)
        k = partial_rope(k[:, :, None, :], positions, ROT_DIM)[:, :, 0, :]
        qi = partial_rope(q_idx.reshape(bsz, s, IDX_HEADS, IDX_DIM), positions, IDX_ROT)
        ki = partial_rope(k_idx[:, :, None, :], positions, IDX_ROT)[:, :, 0, :]
        wi = w_idx * (IDX_HEADS ** -0.5)
        y_b = dsa_attention(qh, k, vv, qi, ki, wi)

        g = jax.nn.sigmoid(gates.reshape(bsz, s, N_BRANCHES, d))
        merged = g[:, :, 0, :] * (y_a @ w_a_out[l]) + g[:, :, 1, :] * (y_b @ w_b_out[l])
        h = h + merged @ w_o[l]

        hn = rms_norm(h, norm2_g[l])
        h = h + peer(hn, peer_wq[l], peer_subkeys[l], peer_u[l], peer_v[l])
    return h
```

```python
import functools

import jax
import jax.numpy as jnp
import numpy as np
from jax import lax
from jax.experimental import pallas as pl
from jax.experimental.pallas import tpu as pltpu

EPS = 1e-6
ROPE_THETA = 500000.0
CHUNK = 128
A_GROUPS = 4
A_GROUP_DIM = 128
N_HEADS = 4
HEAD_DIM = 128
KV_LATENT = 256
ROT_DIM = HEAD_DIM // 4
IDX_HEADS = 4
IDX_DIM = 64
IDX_ROT = IDX_DIM // 4
TOPK_MAX = 256
Q_BLOCK = 128
PEER_HEADS = 8
PEER_HALF = 128
N_KEYS = 128
PEER_TOPK = 16
LANES = 128

_MXU_DTYPE = jnp.bfloat16
_INT_MIN = -2147483648
_NEG_BIG = -1e30
_VMEM_LIMIT = 56 * 1024 * 1024


def _dot(a, b):
    return jnp.dot(a.astype(_MXU_DTYPE), b.astype(_MXU_DTYPE), preferred_element_type=jnp.float32)


def _dot_nt(a, b):
    return lax.dot_general(a.astype(_MXU_DTYPE), b.astype(_MXU_DTYPE), (((1,), (1,)), ((), ())),
                           preferred_element_type=jnp.float32)


def _rms(x, g):
    return x * lax.rsqrt(jnp.mean(x * x, axis=-1, keepdims=True) + EPS) * g


def _rope(x, cos_t, sin_lo, sin_hi, half):
    n = x.shape[-1]
    x_up = pltpu.roll(x, n - half, 1)
    x_dn = pltpu.roll(x, half, 1)
    return x * cos_t + x_up * sin_lo + x_dn * sin_hi


def _rope_tables(pos, inv, half, reps):
    ang = pos * inv
    c = jnp.cos(ang)
    s = jnp.sin(ang)
    lane = lax.broadcasted_iota(jnp.int32, ang.shape, 1)
    s_lo = jnp.where(lane < half, -s, 0.0)
    s_hi = jnp.where(lane >= half, s, 0.0)
    if reps > 1:
        c = jnp.concatenate([c] * reps, axis=1)
        s_lo = jnp.concatenate([s_lo] * reps, axis=1)
        s_hi = jnp.concatenate([s_hi] * reps, axis=1)
    return c, s_lo, s_hi


_OFF_U, _OFF_V, _OFF_Q, _OFF_C, _OFF_QI, _OFF_KI, _OFF_G = 0, 512, 1024, 1536, 1792, 2304, 2432
_W_COLS = 2432 + 2048


def _inproj_kernel(x_ref, pos_ref, g1_ref, w_ref, vg_ref, vb_ref, ws_ref, bs_ref, kvg_ref, wuk_ref, wuv_ref,
                   qg_ref, kg_ref, wa_ref, invq_ref, invi_ref,
                   ma_ref, gb_ref, q_ref, k_ref, v_ref, qi_ref, ki_ref, wi_ref):
    tm = x_ref.shape[0]
    x = x_ref[...]
    xn = _rms(x, g1_ref[...]).astype(_MXU_DTYPE)

    def proj(off, width):
        return jnp.dot(xn, w_ref[:, off:off + width], preferred_element_type=jnp.float32)

    u = jax.nn.gelu(proj(_OFF_U, 512))
    v = jax.nn.gelu(proj(_OFF_V, 512))
    mu = jnp.mean(v, axis=-1, keepdims=True)
    vc = v - mu
    v = vc * lax.rsqrt(jnp.mean(vc * vc, axis=-1, keepdims=True) + EPS) * vg_ref[...] + vb_ref[...]
    v = v.astype(_MXU_DTYPE)
    row = lax.broadcasted_iota(jnp.int32, (CHUNK, CHUNK), 0)
    col = lax.broadcasted_iota(jnp.int32, (CHUNK, CHUNK), 1)
    z_chunks = []
    for c in range(tm // CHUNK):
        zg = []
        for g in range(A_GROUPS):
            wt = jnp.where(row >= col, ws_ref[g], 0.0).astype(_MXU_DTYPE)
            vcg = v[c * CHUNK:(c + 1) * CHUNK, g * A_GROUP_DIM:(g + 1) * A_GROUP_DIM]
            zg.append(jnp.dot(wt, vcg, preferred_element_type=jnp.float32) + bs_ref[:, g:g + 1])
        z_chunks.append(jnp.concatenate(zg, axis=1))
    z = jnp.concatenate(z_chunks, axis=0) if len(z_chunks) > 1 else z_chunks[0]
    ya = u * z
    gate_a = jax.nn.sigmoid(proj(_OFF_G, 1024))
    ma_ref[...] = gate_a * _dot(ya, wa_ref[...])
    gb_ref[...] = jax.nn.sigmoid(proj(_OFF_G + 1024, 1024))

    pos = pos_ref[...]
    cq, sq_lo, sq_hi = _rope_tables(pos, invq_ref[...], ROT_DIM // 2, 1)
    ci, si_lo, si_hi = _rope_tables(pos, invi_ref[...], IDX_ROT // 2, 1)

    q = proj(_OFF_Q, 512)
    qg = qg_ref[...]
    qh = [_rms(q[:, h * HEAD_DIM:(h + 1) * HEAD_DIM], qg) for h in range(N_HEADS)]
    qh = [_rope(t, cq, sq_lo, sq_hi, ROT_DIM // 2) for t in qh]
    q_ref[...] = (jnp.concatenate(qh, axis=1) * (HEAD_DIM ** -0.5)).astype(q_ref.dtype)

    c_n = _rms(proj(_OFF_C, KV_LATENT), kvg_ref[...]).astype(_MXU_DTYPE)
    kk = _rms(jnp.dot(c_n, wuk_ref[...], preferred_element_type=jnp.float32), kg_ref[...])
    k_ref[...] = _rope(kk, cq, sq_lo, sq_hi, ROT_DIM // 2).astype(k_ref.dtype)
    v_ref[...] = jnp.dot(c_n, wuv_ref[...], preferred_element_type=jnp.float32).astype(v_ref.dtype)

    qi = proj(_OFF_QI, 512)
    qis = [_rope(qi[:, h * LANES:(h + 1) * LANES], ci, si_lo, si_hi, IDX_ROT // 2) for h in range(IDX_HEADS)]
    qi_ref[...] = (jnp.concatenate(qis, axis=1) * (IDX_DIM ** -0.5)).astype(qi_ref.dtype)
    kw = proj(_OFF_KI, LANES)
    lane = lax.broadcasted_iota(jnp.int32, kw.shape, 1)
    ki = _rope(jnp.where(lane < IDX_DIM, kw, 0.0), ci, si_lo, si_hi, IDX_ROT // 2)
    ki_ref[...] = ki.astype(ki_ref.dtype)
    wi_ref[...] = kw * (IDX_HEADS ** -0.5)


def _inproj_call(x2, pos, g1, w_pack, vg, vb, ws, bs_t, kvg, wuk, wuv, qg, kg, wa, invq, invi, tm):
    t, d = x2.shape
    full = lambda a: pl.BlockSpec(a.shape, lambda i: (0,) * a.ndim)
    row = lambda w: pl.BlockSpec((tm, w), lambda i: (i, 0))
    f32, mx = jnp.float32, _MXU_DTYPE
    outs = [(1024, f32), (1024, f32), (512, mx), (128, mx), (128, mx), (512, mx), (128, mx), (128, f32)]
    return pl.pallas_call(
        _inproj_kernel,
        grid=(t // tm,),
        in_specs=[row(d), row(1), full(g1), full(w_pack), full(vg), full(vb), full(ws), full(bs_t), full(kvg),
                  full(wuk), full(wuv), full(qg), full(kg), full(wa), full(invq), full(invi)],
        out_specs=[row(w) for w, _ in outs],
        out_shape=[jax.ShapeDtypeStruct((t, w), dt) for w, dt in outs],
        compiler_params=pltpu.CompilerParams(dimension_semantics=("arbitrary",), vmem_limit_bytes=_VMEM_LIMIT),
        name="inproj",
    )(x2, pos, g1, w_pack, vg, vb, ws, bs_t, kvg, wuk, wuv, qg, kg, wa, invq, invi)


def _dsa_kernel(q_ref, qi_ref, wi_ref, kT_ref, kiT_ref, v_ref, o_ref, keys_ref, m_ref, l_ref, acc_ref, *, tk, n_sel):
    blk = pl.program_id(1)
    nq = Q_BLOCK
    n_tiles = (blk * nq + nq + tk - 1) // tk
    qpos = blk * nq + lax.broadcasted_iota(jnp.int32, (nq, tk), 0)
    lane = lax.broadcasted_iota(jnp.int32, (nq, tk), 1)

    qi = qi_ref[...]
    qi_stack = jnp.concatenate([qi[:, h * LANES:(h + 1) * LANES] for h in range(IDX_HEADS)], axis=0)
    wi = wi_ref[...]
    wb = [jnp.broadcast_to(wi[:, IDX_DIM + h:IDX_DIM + h + 1], (nq, tk)) for h in range(IDX_HEADS)]

    def score_tile(j, carry):
        off = pl.multiple_of(j * tk, tk)
        lg = jnp.dot(qi_stack, kiT_ref[:, pl.ds(off, tk)], preferred_element_type=jnp.float32)
        r = jnp.maximum(lg, 0.0)
        sc = wb[0] * r[0:nq]
        for h in range(1, IDX_HEADS):
            sc = sc + wb[h] * r[h * nq:(h + 1) * nq]
        bits = pltpu.bitcast(sc, jnp.int32)
        key = bits ^ ((bits >> 31) & 0x7FFFFFFF)
        key = jnp.where(off + lane <= qpos, key, _INT_MIN)
        keys_ref[:, pl.ds(off, tk)] = key
        return carry

    lax.fori_loop(0, n_tiles, score_tile, 0)

    def count_ge(cand):
        def body(j, cnt):
            off = pl.multiple_of(j * tk, tk)
            hit = (keys_ref[:, pl.ds(off, tk)] >= cand).astype(jnp.int32)
            for c in range(tk // LANES):
                cnt = cnt + hit[:, c * LANES:(c + 1) * LANES]
            return cnt
        cnt = lax.fori_loop(0, n_tiles, body, jnp.zeros((nq, LANES), jnp.int32))
        return jnp.sum(cnt, axis=1, keepdims=True)

    zero = jnp.zeros((nq, 1), jnp.int32)
    base = jnp.where(count_ge(zero) >= n_sel, zero, _INT_MIN)

    def bit_step(b, base):
        cand = base | jnp.left_shift(jnp.int32(1), 30 - b)
        return jnp.where(count_ge(cand) >= n_sel, cand, base)

    thr = lax.fori_loop(0, 31, bit_step, base)
    thr = jnp.maximum(thr, _INT_MIN + 1)
    n_gt = count_ge(thr + 1)
    need = (n_sel - n_gt).astype(jnp.float32)

    q = q_ref[...]
    q_stack = jnp.concatenate([q[:, h * HEAD_DIM:(h + 1) * HEAD_DIM] for h in range(N_HEADS)], axis=0)
    tri = (lax.broadcasted_iota(jnp.int32, (tk, tk), 0) <= lax.broadcasted_iota(jnp.int32, (tk, tk), 1))
    tri = jnp.where(tri, 1.0, 0.0).astype(_MXU_DTYPE)
    m_ref[...] = jnp.full(m_ref.shape, _NEG_BIG, jnp.float32)
    l_ref[...] = jnp.zeros(l_ref.shape, jnp.float32)
    acc_ref[...] = jnp.zeros(acc_ref.shape, jnp.float32)

    def attn_tile(j, seen):
        off = pl.multiple_of(j * tk, tk)
        key = keys_ref[:, pl.ds(off, tk)]
        eq = key == thr
        pre = jnp.dot(jnp.where(eq, 1.0, 0.0).astype(_MXU_DTYPE), tri, preferred_element_type=jnp.float32)
        sel = (key > thr) | (eq & (pre <= need - seen))
        bias = jnp.where(sel, 0.0, _NEG_BIG)
        s = jnp.dot(q_stack, kT_ref[:, pl.ds(off, tk)], preferred_element_type=jnp.float32)
        s = s + jnp.concatenate([bias] * N_HEADS, axis=0)
        m_old = m_ref[...]
        m_new = jnp.maximum(m_old, jnp.max(s, axis=1, keepdims=True))
        alpha = jnp.exp(m_old - m_new)
        p = jnp.exp(s - m_new)
        l_ref[...] = alpha * l_ref[...] + jnp.sum(p, axis=1, keepdims=True)
        acc_ref[...] = alpha * acc_ref[...] + jnp.dot(p.astype(_MXU_DTYPE), v_ref[pl.ds(off, tk), :],
                                                     preferred_element_type=jnp.float32)
        m_ref[...] = m_new
        return seen + pre[:, tk - 1:tk]

    lax.fori_loop(0, n_tiles, attn_tile, jnp.zeros((nq, 1), jnp.float32))
    y = acc_ref[...] / l_ref[...]
    o_ref[...] = jnp.concatenate([y[h * nq:(h + 1) * nq] for h in range(N_HEADS)], axis=1).astype(o_ref.dtype)


def _dsa_call(q, qi, wi, kT, kiT, v, n_sel, tk):
    b, s, _ = q.shape
    nblk = s // Q_BLOCK
    qspec = lambda w: pl.BlockSpec((None, Q_BLOCK, w), lambda bi, i: (bi, i, 0))
    kern = functools.partial(_dsa_kernel, tk=tk, n_sel=n_sel)
    return pl.pallas_call(
        kern,
        grid=(b, nblk),
        in_specs=[qspec(512), qspec(512), qspec(128),
                  pl.BlockSpec((None, HEAD_DIM, s), lambda bi, i: (bi, 0, 0)),
                  pl.BlockSpec((None, LANES, s), lambda bi, i: (bi, 0, 0)),
                  pl.BlockSpec((None, s, HEAD_DIM), lambda bi, i: (bi, 0, 0))],
        out_specs=qspec(512),
        out_shape=jax.ShapeDtypeStruct((b, s, 512), _MXU_DTYPE),
        scratch_shapes=[pltpu.VMEM((Q_BLOCK, s), jnp.int32),
                        pltpu.VMEM((N_HEADS * Q_BLOCK, 1), jnp.float32),
                        pltpu.VMEM((N_HEADS * Q_BLOCK, 1), jnp.float32),
                        pltpu.VMEM((N_HEADS * Q_BLOCK, HEAD_DIM), jnp.float32)],
        compiler_params=pltpu.CompilerParams(dimension_semantics=("arbitrary", "arbitrary"),
                                             vmem_limit_bytes=_VMEM_LIMIT),
        name="dsa",
    )(q, qi, wi, kT, kiT, v)


def _merge_kernel(x_ref, ma_ref, gb_ref, yb_ref, wb_ref, wo_ref, g2_ref, wq_ref, sk_ref, h_ref, hn_ref, sub_ref):
    merged = ma_ref[...] + gb_ref[...] * jnp.dot(yb_ref[...], wb_ref[...], preferred_element_type=jnp.float32)
    h1 = x_ref[...] + _dot(merged, wo_ref[...])
    h_ref[...] = h1
    hn = _rms(h1, g2_ref[...]).astype(_MXU_DTYPE)
    hn_ref[...] = hn
    qq = jnp.dot(hn, wq_ref[...], preferred_element_type=jnp.float32).astype(_MXU_DTYPE)
    for hp in range(2 * PEER_HEADS):
        sub_ref[hp] = _dot_nt(sk_ref[hp], qq[:, hp * PEER_HALF:(hp + 1) * PEER_HALF])


def _merge_call(x2, ma, gb, yb, wb, wo, g2, wq, sk, tm):
    t, d = x2.shape
    full = lambda a: pl.BlockSpec(a.shape, lambda i: (0,) * a.ndim)
    row = lambda w: pl.BlockSpec((tm, w), lambda i: (i, 0))
    return pl.pallas_call(
        _merge_kernel,
        grid=(t // tm,),
        in_specs=[row(d), row(d), row(d), row(512), full(wb), full(wo), full(g2), full(wq), full(sk)],
        out_specs=[row(d), row(d), pl.BlockSpec((2 * PEER_HEADS, N_KEYS, tm), lambda i: (0, 0, i))],
        out_shape=[jax.ShapeDtypeStruct((t, d), jnp.float32), jax.ShapeDtypeStruct((t, d), _MXU_DTYPE),
                   jax.ShapeDtypeStruct((2 * PEER_HEADS, N_KEYS, t), jnp.float32)],
        compiler_params=pltpu.CompilerParams(dimension_semantics=("arbitrary",), vmem_limit_bytes=_VMEM_LIMIT),
        name="merge",
    )(x2, ma, gb, yb, wb, wo, g2, wq, sk)


def _topk_rows(vals, idx_payload, k):
    n_rows = vals.shape[0]
    rid = lax.broadcasted_iota(jnp.int32, vals.shape, 0)
    tops, pays = [], []
    for _ in range(k):
        m = jnp.max(vals, axis=0, keepdims=True)
        pos = jnp.min(jnp.where(vals == m, rid, n_rows), axis=0, keepdims=True)
        hit = rid == pos
        tops.append(m)
        if idx_payload is None:
            pays.append(pos)
        else:
            pays.append(jnp.max(jnp.where(hit, idx_payload, -1), axis=0, keepdims=True))
        vals = jnp.where(hit, -jnp.inf, vals)
    return tops, pays


def _peer_topk_kernel(sub_ref, i_ref, j_ref, g_ref):
    kk = PEER_TOPK
    for h in range(PEER_HEADS):
        s1, i1 = _topk_rows(sub_ref[2 * h], None, kk)
        s2, i2 = _topk_rows(sub_ref[2 * h + 1], None, kk)
        s2c = jnp.concatenate(s2, axis=0)
        i2c = jnp.concatenate(i2, axis=0)
        cand = jnp.concatenate([s1[a] + s2c for a in range(kk)], axis=0)
        cidx = jnp.concatenate([i1[a] * N_KEYS + i2c for a in range(kk)], axis=0)
        best, exp_id = _topk_rows(cand, cidx, kk)
        best = jnp.concatenate(best, axis=0)
        exp_id = jnp.concatenate(exp_id, axis=0)
        e = jnp.exp(best - best[0:1])
        gate = e / jnp.sum(e, axis=0, keepdims=True)
        i_ref[h * kk:(h + 1) * kk, :] = exp_id >> 7
        j_ref[h * kk:(h + 1) * kk, :] = exp_id & (N_KEYS - 1)
        g_ref[h * kk:(h + 1) * kk, :] = gate


def _peer_topk_call(sub, tt):
    _, _, t = sub.shape
    hk = PEER_HEADS * PEER_TOPK
    out = pl.BlockSpec((hk, tt), lambda i: (0, i))
    return pl.pallas_call(
        _peer_topk_kernel,
        grid=(t // tt,),
        in_specs=[pl.BlockSpec((2 * PEER_HEADS, N_KEYS, tt), lambda i: (0, 0, i))],
        out_specs=[out, out, out],
        out_shape=[jax.ShapeDtypeStruct((hk, t), jnp.int32), jax.ShapeDtypeStruct((hk, t), jnp.int32),
                   jax.ShapeDtypeStruct((hk, t), jnp.float32)],
        compiler_params=pltpu.CompilerParams(dimension_semantics=("arbitrary",), vmem_limit_bytes=_VMEM_LIMIT),
        name="peer_topk",
    )(sub)


def _peer_coef_kernel(i_ref, j_ref, g_ref, m_ref):
    tt = i_ref.shape[0]
    rid = lax.broadcasted_iota(jnp.int32, (N_KEYS, LANES), 0)

    def body(t, carry):
        irow = i_ref[pl.ds(t, 1), :]
        jrow = j_ref[pl.ds(t, 1), :]
        grow = g_ref[pl.ds(t, 1), :]
        rt = jnp.where(rid == irow, grow, 0.0)
        ct = jnp.where(rid == jrow, 1.0, 0.0)
        m_ref[t] = _dot_nt(rt, ct)
        return carry

    lax.fori_loop(0, tt, body, 0, unroll=4)


def _peer_coef_call(it, jt, gt, tt):
    t, hk = it.shape
    row = pl.BlockSpec((tt, hk), lambda i: (i, 0))
    return pl.pallas_call(
        _peer_coef_kernel,
        grid=(t // tt,),
        in_specs=[row, row, row],
        out_specs=pl.BlockSpec((tt, N_KEYS, N_KEYS), lambda i: (i, 0, 0)),
        out_shape=jax.ShapeDtypeStruct((t, N_KEYS, N_KEYS), jnp.float32),
        compiler_params=pltpu.CompilerParams(dimension_semantics=("arbitrary",), vmem_limit_bytes=_VMEM_LIMIT),
        name="peer_coef",
    )(it, jt, gt)


def _peer_dense_kernel(hn_ref, h_ref, ut_ref, v_ref, m_ref, o_ref, *, ib):
    e = pl.program_id(1)

    @pl.when(e == 0)
    def _():
        o_ref[...] = h_ref[...]

    a = jax.nn.gelu(jnp.dot(hn_ref[...], ut_ref[...], preferred_element_type=jnp.float32))
    c = jnp.concatenate([a[:, ii * N_KEYS:(ii + 1) * N_KEYS] * m_ref[:, ii, :] for ii in range(ib)], axis=1)
    o_ref[...] += jnp.dot(c.astype(_MXU_DTYPE), v_ref[...], preferred_element_type=jnp.float32)


def _peer_dense_call(hn, h1, ut, v, m3, tm, ib):
    t, d = hn.shape
    te = ib * N_KEYS
    kern = functools.partial(_peer_dense_kernel, ib=ib)
    return pl.pallas_call(
        kern,
        grid=(t // tm, N_KEYS // ib),
        in_specs=[pl.BlockSpec((tm, d), lambda ti, e: (ti, 0)),
                  pl.BlockSpec((tm, d), lambda ti, e: (ti, 0)),
                  pl.BlockSpec((d, te), lambda ti, e: (0, e)),
                  pl.BlockSpec((te, d), lambda ti, e: (e, 0)),
                  pl.BlockSpec((tm, ib, N_KEYS), lambda ti, e: (ti, e, 0))],
        out_specs=pl.BlockSpec((tm, d), lambda ti, e: (ti, 0)),
        out_shape=jax.ShapeDtypeStruct((t, d), jnp.float32),
        compiler_params=pltpu.CompilerParams(dimension_semantics=("arbitrary", "arbitrary"),
                                             vmem_limit_bytes=_VMEM_LIMIT),
        name="peer_dense",
    )(hn, h1, ut, v, m3)


def _rope_inv(rot_dim, period):
    half = rot_dim // 2
    inv = jnp.power(jnp.float32(ROPE_THETA), -jnp.arange(half, dtype=jnp.float32) * 2.0 / rot_dim)
    lane = np.arange(LANES)
    in_rot = (lane % period) < rot_dim
    pat = jnp.where(jnp.asarray(in_rot), inv[jnp.asarray(lane % period % half)], 0.0)
    return pat.reshape(1, LANES).astype(jnp.float32)


def _pack_w_in(w):
    d = w.shape[0]
    z = lambda n: jnp.zeros((d, n), w.dtype)
    segs = [w[:, 0:1536 + KV_LATENT]]
    for h in range(IDX_HEADS):
        segs += [w[:, 1792 + h * IDX_DIM:1792 + (h + 1) * IDX_DIM], z(LANES - IDX_DIM)]
    segs += [w[:, 2048:2116], z(LANES - 68)]
    segs += [w[:, 2116:]]
    out = jnp.concatenate(segs, axis=1)
    assert out.shape[1] == _W_COLS
    return out.astype(_MXU_DTYPE)


def kernel(x, positions, norm1_g, w_in, v_norm_g, v_norm_b, spatial_w, spatial_b, kv_norm_g, w_uk, w_uv,
           q_norm_g, k_norm_g, w_a_out, w_b_out, w_o, norm2_g, peer_wq, peer_subkeys, peer_u, peer_v):
    bsz, s, d = x.shape
    t = bsz * s
    depth = w_in.shape[0]
    mx = _MXU_DTYPE
    n_sel = min(TOPK_MAX, s // 4)
    tm = min(256, t)
    tk = min(256, s)
    invq = _rope_inv(ROT_DIM, LANES)
    invi = _rope_inv(IDX_ROT, LANES)
    pos = positions.reshape(t, 1).astype(jnp.float32)
    r2 = lambda a: a.reshape(1, -1)
    h = x.reshape(t, d)
    for l in range(depth):
        ma, gb, q, k, v, qi, ki, wi = _inproj_call(
            h, pos, r2(norm1_g[l]), _pack_w_in(w_in[l]), r2(v_norm_g[l]), r2(v_norm_b[l]), spatial_w[l],
            spatial_b[l].T, r2(kv_norm_g[l]), w_uk[l].astype(mx), w_uv[l].astype(mx), r2(q_norm_g[l]),
            r2(k_norm_g[l]), w_a_out[l].astype(mx), invq, invi, tm)
        b3 = lambda a: a.reshape(bsz, s, a.shape[-1])
        kT = b3(k).transpose(0, 2, 1)
        kiT = b3(ki).transpose(0, 2, 1)
        yb = _dsa_call(b3(q), b3(qi), b3(wi), kT, kiT, b3(v), n_sel, tk).reshape(t, 512)
        sk = peer_subkeys[l].reshape(2 * PEER_HEADS, N_KEYS, PEER_HALF).astype(mx)
        h1, hn, sub = _merge_call(h, ma, gb, yb, w_b_out[l].astype(mx), w_o[l].astype(mx), r2(norm2_g[l]),
                                  peer_wq[l].astype(mx), sk, tm)
        ei, ej, eg = _peer_topk_call(sub, min(256, t))
        m3 = _peer_coef_call(ei.T, ej.T, eg.T, min(32, t))
        h = _peer_dense_call(hn, h1, peer_u[l].T.astype(mx), peer_v[l].astype(mx), m3, min(512, t), 8)
    return h.reshape(bsz, s, d)
```

```python
import functools
import math

import jax
import jax.numpy as jnp
import numpy as np
from jax import lax
from jax.experimental import pallas as pl
from jax.experimental.pallas import tpu as pltpu

EPS = 1e-6
ROPE_THETA = 500000.0
CHUNK = 128
A_GROUPS = 4
A_GROUP_DIM = 128
N_HEADS = 4
HEAD_DIM = 128
KV_LATENT = 256
ROT_DIM = HEAD_DIM // 4
IDX_HEADS = 4
IDX_DIM = 64
IDX_ROT = IDX_DIM // 4
TOPK_MAX = 256
Q_BLOCK = 128
PEER_HEADS = 8
PEER_HALF = 128
N_KEYS = 128
PEER_TOPK = 16
LANES = 128

_MXU_DTYPE = jnp.bfloat16
_INT_MIN = -2147483648
_NEG_BIG = -1e30
_VMEM_LIMIT = 56 * 1024 * 1024
_V_EXT = 2 * HEAD_DIM


def _dot(a, b):
    return jnp.dot(a.astype(_MXU_DTYPE), b.astype(_MXU_DTYPE), preferred_element_type=jnp.float32)


def _dot_nt(a, b):
    return lax.dot_general(a.astype(_MXU_DTYPE), b.astype(_MXU_DTYPE), (((1,), (1,)), ((), ())),
                           preferred_element_type=jnp.float32)


def _rms(x, g):
    return x * lax.rsqrt(jnp.mean(x * x, axis=-1, keepdims=True) + EPS) * g


def _rope(x, cos_t, sin_lo, sin_hi, half):
    n = x.shape[-1]
    x_up = pltpu.roll(x, n - half, 1)
    x_dn = pltpu.roll(x, half, 1)
    return x * cos_t + x_up * sin_lo + x_dn * sin_hi


def _rope_tables(pos, inv, half):
    ang = pos * inv
    c = jnp.cos(ang)
    s = jnp.sin(ang)
    lane = lax.broadcasted_iota(jnp.int32, ang.shape, 1)
    s_lo = jnp.where(lane < half, -s, 0.0)
    s_hi = jnp.where(lane >= half, s, 0.0)
    return c, s_lo, s_hi


_OFF_U, _OFF_V, _OFF_Q, _OFF_C, _OFF_QI, _OFF_KI, _OFF_G = 0, 512, 1024, 1536, 1792, 2304, 2432
_W_COLS = 2432 + 2048


def _inproj_kernel(x_ref, pos_ref, g1_ref, w_ref, vg_ref, vb_ref, ws_ref, bs_ref, kvg_ref, wuk_ref, wuv_ref,
                   qg_ref, kg_ref, wa_ref, invq_ref, invi_ref,
                   ma_ref, gb_ref, q_ref, k_ref, v_ref, qi_ref, ki_ref, wi_ref):
    tm = x_ref.shape[0]
    x = x_ref[...]
    xn = _rms(x, g1_ref[...]).astype(_MXU_DTYPE)

    def proj(off, width):
        return jnp.dot(xn, w_ref[:, off:off + width], preferred_element_type=jnp.float32)

    u = jax.nn.gelu(proj(_OFF_U, 512))
    v = jax.nn.gelu(proj(_OFF_V, 512))
    mu = jnp.mean(v, axis=-1, keepdims=True)
    vc = v - mu
    v = vc * lax.rsqrt(jnp.mean(vc * vc, axis=-1, keepdims=True) + EPS) * vg_ref[...] + vb_ref[...]
    v = v.astype(_MXU_DTYPE)
    row = lax.broadcasted_iota(jnp.int32, (CHUNK, CHUNK), 0)
    col = lax.broadcasted_iota(jnp.int32, (CHUNK, CHUNK), 1)
    z_chunks = []
    for c in range(tm // CHUNK):
        zg = []
        for g in range(A_GROUPS):
            wt = jnp.where(row >= col, ws_ref[g], 0.0).astype(_MXU_DTYPE)
            vcg = v[c * CHUNK:(c + 1) * CHUNK, g * A_GROUP_DIM:(g + 1) * A_GROUP_DIM]
            zg.append(jnp.dot(wt, vcg, preferred_element_type=jnp.float32) + bs_ref[:, g:g + 1])
        z_chunks.append(jnp.concatenate(zg, axis=1))
    z = jnp.concatenate(z_chunks, axis=0) if len(z_chunks) > 1 else z_chunks[0]
    ya = u * z
    gate_a = jax.nn.sigmoid(proj(_OFF_G, 1024))
    ma_ref[...] = gate_a * _dot(ya, wa_ref[...])
    gb_ref[...] = jax.nn.sigmoid(proj(_OFF_G + 1024, 1024))

    pos = pos_ref[...]
    cq, sq_lo, sq_hi = _rope_tables(pos, invq_ref[...], ROT_DIM // 2)
    ci, si_lo, si_hi = _rope_tables(pos, invi_ref[...], IDX_ROT // 2)

    q = proj(_OFF_Q, 512)
    qg = qg_ref[...]
    qh = [_rms(q[:, h * HEAD_DIM:(h + 1) * HEAD_DIM], qg) for h in range(N_HEADS)]
    qh = [_rope(t, cq, sq_lo, sq_hi, ROT_DIM // 2) for t in qh]
    q_ref[...] = (jnp.concatenate(qh, axis=1) * (HEAD_DIM ** -0.5 * math.log2(math.e))).astype(q_ref.dtype)

    c_n = _rms(proj(_OFF_C, KV_LATENT), kvg_ref[...]).astype(_MXU_DTYPE)
    kk = _rms(jnp.dot(c_n, wuk_ref[...], preferred_element_type=jnp.float32), kg_ref[...])
    k_ref[...] = _rope(kk, cq, sq_lo, sq_hi, ROT_DIM // 2).astype(k_ref.dtype)
    vv = jnp.dot(c_n, wuv_ref[...], preferred_element_type=jnp.float32)
    ones_col = jnp.where(lax.broadcasted_iota(jnp.int32, vv.shape, 1) == 0, 1.0, 0.0)
    v_ref[...] = jnp.concatenate([vv, ones_col], axis=1).astype(v_ref.dtype)

    qi = proj(_OFF_QI, 512)
    qis = [_rope(qi[:, h * LANES:(h + 1) * LANES], ci, si_lo, si_hi, IDX_ROT // 2) for h in range(IDX_HEADS)]
    qi_ref[...] = (jnp.concatenate(qis, axis=1) * (IDX_DIM ** -0.5)).astype(qi_ref.dtype)
    kw = proj(_OFF_KI, LANES)
    lane = lax.broadcasted_iota(jnp.int32, kw.shape, 1)
    ki = _rope(jnp.where(lane < IDX_DIM, kw, 0.0), ci, si_lo, si_hi, IDX_ROT // 2)
    ki_ref[...] = ki.astype(ki_ref.dtype)
    wi_ref[...] = kw * (IDX_HEADS ** -0.5)


def _inproj_call(x2, pos, g1, w_pack, vg, vb, ws, bs_t, kvg, wuk, wuv, qg, kg, wa, invq, invi, tm):
    t, d = x2.shape
    full = lambda a: pl.BlockSpec(a.shape, lambda i: (0,) * a.ndim)
    row = lambda w: pl.BlockSpec((tm, w), lambda i: (i, 0))
    f32, mx = jnp.float32, _MXU_DTYPE
    outs = [(1024, f32), (1024, f32), (512, mx), (128, mx), (_V_EXT, mx), (512, mx), (128, mx), (128, f32)]
    return pl.pallas_call(
        _inproj_kernel,
        grid=(t // tm,),
        in_specs=[row(d), row(1), full(g1), full(w_pack), full(vg), full(vb), full(ws), full(bs_t), full(kvg),
                  full(wuk), full(wuv), full(qg), full(kg), full(wa), full(invq), full(invi)],
        out_specs=[row(w) for w, _ in outs],
        out_shape=[jax.ShapeDtypeStruct((t, w), dt) for w, dt in outs],
        compiler_params=pltpu.CompilerParams(dimension_semantics=("arbitrary",), vmem_limit_bytes=_VMEM_LIMIT),
        name="inproj",
    )(x2, pos, g1, w_pack, vg, vb, ws, bs_t, kvg, wuk, wuv, qg, kg, wa, invq, invi)


def _dsa_kernel(q_ref, qi_ref, wi_ref, kT_ref, kiT_ref, v_ref, o_ref, keys_ref, wb_ref, m_ref, acc_ref, *, tk, n_sel):
    blk = pl.program_id(1)
    nq = Q_BLOCK
    ck = 2 * tk
    n_pairs = (blk * nq + nq + ck - 1) // ck

    wi = wi_ref[...]
    for h in range(IDX_HEADS):
        wb_ref[h] = jnp.broadcast_to(wi[:, IDX_DIM + h:IDX_DIM + h + 1], (nq, tk))

    def score_pair(j, causal_mask):
        for sub in range(2):
            off = pl.multiple_of(j * ck + sub * tk, tk)
            kt = kiT_ref[:, pl.ds(off, tk)]
            sc = None
            for h in range(IDX_HEADS):
                lg = jnp.dot(qi_ref[:, h * LANES:(h + 1) * LANES], kt, preferred_element_type=jnp.float32)
                term = wb_ref[h] * jnp.maximum(lg, 0.0)
                sc = term if sc is None else sc + term
            bits = pltpu.bitcast(sc, jnp.int32)
            key = bits ^ ((bits >> 31) & 0x7FFFFFFF)
            if causal_mask:
                qpos = blk * nq + lax.broadcasted_iota(jnp.int32, (nq, tk), 0)
                kpos = off + lax.broadcasted_iota(jnp.int32, (nq, tk), 1)
                key = jnp.where(kpos <= qpos, key, _INT_MIN)
            keys_ref[:, pl.ds(off, tk)] = key

    def score_body(j, carry):
        score_pair(j, False)
        return carry

    lax.fori_loop(0, n_pairs - 1, score_body, 0)
    score_pair(n_pairs - 1, True)


    def count_ge(cand):
        def body(j, cnt):
            off = pl.multiple_of(j * ck, ck)
            hit = (keys_ref[:, pl.ds(off, ck)] >= cand).astype(jnp.int32)
            for c in range(ck // LANES):
                cnt = cnt + hit[:, c * LANES:(c + 1) * LANES]
            return cnt
        cnt = lax.fori_loop(0, n_pairs, body, jnp.zeros((nq, LANES), jnp.int32))
        return jnp.sum(cnt, axis=1, keepdims=True)

    zero = jnp.zeros((nq, 1), jnp.int32)
    base = jnp.where(count_ge(zero) >= n_sel, zero, _INT_MIN)

    def bit_step(b, base):
        cand = base | jnp.left_shift(jnp.int32(1), 30 - b)
        return jnp.where(count_ge(cand) >= n_sel, cand, base)

    thr = lax.fori_loop(0, 31, bit_step, base)
    thr = jnp.maximum(thr, _INT_MIN + 1)
    n_gt = count_ge(thr + 1)
    need = (n_sel - n_gt).astype(jnp.float32)

    tri = (lax.broadcasted_iota(jnp.int32, (tk, tk), 0) <= lax.broadcasted_iota(jnp.int32, (tk, tk), 1))
    tri = jnp.where(tri, 1.0, 0.0).astype(_MXU_DTYPE)
    m_ref[...] = jnp.full(m_ref.shape, _NEG_BIG, jnp.float32)
    acc_ref[...] = jnp.zeros(acc_ref.shape, jnp.float32)

    def attn_pair(j, room):
        off = pl.multiple_of(j * ck, ck)
        biases = []
        for sub in range(2):
            key = keys_ref[:, pl.ds(off + sub * tk, tk)]
            eq = key == thr
            pre = jnp.dot(jnp.where(eq, 1.0, 0.0).astype(_MXU_DTYPE), tri, preferred_element_type=jnp.float32)
            sel = (key > thr) | (eq & (pre <= room))
            biases.append(jnp.where(sel, 0.0, _NEG_BIG))
            room = room - pre[:, tk - 1:tk]
        bias = jnp.concatenate(biases, axis=1)
        kt = kT_ref[:, pl.ds(off, ck)]
        vt = v_ref[pl.ds(off, ck), :]
        for h in range(N_HEADS):
            s = jnp.dot(q_ref[:, h * HEAD_DIM:(h + 1) * HEAD_DIM], kt, preferred_element_type=jnp.float32) + bias
            m_old = m_ref[h]
            m_new = jnp.maximum(m_old, jnp.max(s, axis=1, keepdims=True))
            alpha = jnp.exp2(m_old - m_new)
            p = jnp.exp2(s - jnp.concatenate([m_new] * (ck // LANES), axis=1))
            acc_ref[h] = (jnp.concatenate([alpha] * (_V_EXT // LANES), axis=1) * acc_ref[h]
                          + jnp.dot(p.astype(_MXU_DTYPE), vt, preferred_element_type=jnp.float32))
            m_ref[h] = m_new
        return room

    lax.fori_loop(0, n_pairs, attn_pair, need)
    ys = []
    for h in range(N_HEADS):
        a = acc_ref[h]
        ys.append(a[:, :HEAD_DIM] / a[:, HEAD_DIM:HEAD_DIM + 1])
    o_ref[...] = jnp.concatenate(ys, axis=1).astype(o_ref.dtype)


def _dsa_call(q, qi, wi, kT, kiT, v, n_sel, tk):
    b, s, _ = q.shape
    nblk = s // Q_BLOCK
    assert s % (2 * tk) == 0 and tk % Q_BLOCK == 0
    qspec = lambda w: pl.BlockSpec((None, Q_BLOCK, w), lambda bi, i: (bi, i, 0))
    kern = functools.partial(_dsa_kernel, tk=tk, n_sel=n_sel)
    return pl.pallas_call(
        kern,
        grid=(b, nblk),
        in_specs=[qspec(512), qspec(512), qspec(128),
                  pl.BlockSpec((None, HEAD_DIM, s), lambda bi, i: (bi, 0, 0)),
                  pl.BlockSpec((None, LANES, s), lambda bi, i: (bi, 0, 0)),
                  pl.BlockSpec((None, s, _V_EXT), lambda bi, i: (bi, 0, 0))],
        out_specs=qspec(512),
        out_shape=jax.ShapeDtypeStruct((b, s, 512), _MXU_DTYPE),
        scratch_shapes=[pltpu.VMEM((Q_BLOCK, s), jnp.int32),
                        pltpu.VMEM((IDX_HEADS, Q_BLOCK, tk), jnp.float32),
                        pltpu.VMEM((N_HEADS, Q_BLOCK, LANES), jnp.float32),
                        pltpu.VMEM((N_HEADS, Q_BLOCK, _V_EXT), jnp.float32)],
        compiler_params=pltpu.CompilerParams(dimension_semantics=("arbitrary", "arbitrary"),
                                             vmem_limit_bytes=_VMEM_LIMIT),
        name="dsa",
    )(q, qi, wi, kT, kiT, v)


def _merge_kernel(x_ref, ma_ref, gb_ref, yb_ref, wb_ref, wo_ref, g2_ref, wq_ref, sk_ref, h_ref, hn_ref, sub_ref):
    merged = ma_ref[...] + gb_ref[...] * jnp.dot(yb_ref[...], wb_ref[...], preferred_element_type=jnp.float32)
    h1 = x_ref[...] + _dot(merged, wo_ref[...])
    h_ref[...] = h1
    hn = _rms(h1, g2_ref[...]).astype(_MXU_DTYPE)
    hn_ref[...] = hn
    qq = jnp.dot(hn, wq_ref[...], preferred_element_type=jnp.float32).astype(_MXU_DTYPE)
    for hp in range(2 * PEER_HEADS):
        sub_ref[hp] = _dot_nt(sk_ref[hp], qq[:, hp * PEER_HALF:(hp + 1) * PEER_HALF])


def _merge_call(x2, ma, gb, yb, wb, wo, g2, wq, sk, tm):
    t, d = x2.shape
    full = lambda a: pl.BlockSpec(a.shape, lambda i: (0,) * a.ndim)
    row = lambda w: pl.BlockSpec((tm, w), lambda i: (i, 0))
    return pl.pallas_call(
        _merge_kernel,
        grid=(t // tm,),
        in_specs=[row(d), row(d), row(d), row(512), full(wb), full(wo), full(g2), full(wq), full(sk)],
        out_specs=[row(d), row(d), pl.BlockSpec((2 * PEER_HEADS, N_KEYS, tm), lambda i: (0, 0, i))],
        out_shape=[jax.ShapeDtypeStruct((t, d), jnp.float32), jax.ShapeDtypeStruct((t, d), _MXU_DTYPE),
                   jax.ShapeDtypeStruct((2 * PEER_HEADS, N_KEYS, t), jnp.float32)],
        compiler_params=pltpu.CompilerParams(dimension_semantics=("arbitrary",), vmem_limit_bytes=_VMEM_LIMIT),
        name="merge",
    )(x2, ma, gb, yb, wb, wo, g2, wq, sk)


def _topk_rows(vals, idx_payload, k):
    n_rows = vals.shape[0]
    rid = lax.broadcasted_iota(jnp.int32, vals.shape, 0)
    tops, pays = [], []
    for _ in range(k):
        m = jnp.max(vals, axis=0, keepdims=True)
        pos = jnp.min(jnp.where(vals == m, rid, n_rows), axis=0, keepdims=True)
        hit = rid == pos
        tops.append(m)
        if idx_payload is None:
            pays.append(pos)
        else:
            pays.append(jnp.max(jnp.where(hit, idx_payload, -1), axis=0, keepdims=True))
        vals = jnp.where(hit, -jnp.inf, vals)
    return tops, pays


_PEER_PAIR_COUNTS = [PEER_TOPK // (a + 1) for a in range(PEER_TOPK)]
_PEER_CAND_ROWS = -(-sum(_PEER_PAIR_COUNTS) // 8) * 8


def _peer_topk_kernel(sub_ref, i_ref, j_ref, g_ref):
    kk = PEER_TOPK
    tt = sub_ref.shape[-1]

    def head(h, carry):
        s1, i1 = _topk_rows(sub_ref[2 * h], None, kk)
        s2, i2 = _topk_rows(sub_ref[2 * h + 1], None, kk)
        s2c = jnp.concatenate(s2, axis=0)
        i2c = jnp.concatenate(i2, axis=0)
        n_pad = _PEER_CAND_ROWS - sum(_PEER_PAIR_COUNTS)
        cand = jnp.concatenate([s1[a] + s2c[0:nb] for a, nb in enumerate(_PEER_PAIR_COUNTS)]
                               + [jnp.full((n_pad, tt), -jnp.inf, jnp.float32)], axis=0)
        cidx = jnp.concatenate([i1[a] * N_KEYS + i2c[0:nb] for a, nb in enumerate(_PEER_PAIR_COUNTS)]
                               + [jnp.zeros((n_pad, tt), jnp.int32)], axis=0)
        best, exp_id = _topk_rows(cand, cidx, kk)
        best = jnp.concatenate(best, axis=0)
        exp_id = jnp.concatenate(exp_id, axis=0)
        e = jnp.exp(best - best[0:1])
        gate = e / jnp.sum(e, axis=0, keepdims=True)
        rows = pl.ds(pl.multiple_of(h * kk, kk), kk)
        i_ref[rows, :] = exp_id >> 7
        j_ref[rows, :] = exp_id & (N_KEYS - 1)
        g_ref[rows, :] = gate
        return carry

    lax.fori_loop(0, PEER_HEADS, head, 0)


def _peer_topk_call(sub, tt):
    _, _, t = sub.shape
    hk = PEER_HEADS * PEER_TOPK
    out = pl.BlockSpec((hk, tt), lambda i: (0, i))
    return pl.pallas_call(
        _peer_topk_kernel,
        grid=(t // tt,),
        in_specs=[pl.BlockSpec((2 * PEER_HEADS, N_KEYS, tt), lambda i: (0, 0, i))],
        out_specs=[out, out, out],
        out_shape=[jax.ShapeDtypeStruct((hk, t), jnp.int32), jax.ShapeDtypeStruct((hk, t), jnp.int32),
                   jax.ShapeDtypeStruct((hk, t), jnp.float32)],
        compiler_params=pltpu.CompilerParams(dimension_semantics=("arbitrary",), vmem_limit_bytes=_VMEM_LIMIT),
        name="peer_topk",
    )(sub)


def _peer_coef_kernel(i_ref, j_ref, g_ref, m_ref):
    tt = i_ref.shape[0]
    rid = lax.broadcasted_iota(jnp.int32, (N_KEYS, LANES), 0)

    def body(t, carry):
        irow = i_ref[pl.ds(t, 1), :]
        jrow = j_ref[pl.ds(t, 1), :]
        grow = g_ref[pl.ds(t, 1), :]
        rt = jnp.where(rid == irow, grow, 0.0)
        ct = jnp.where(rid == jrow, 1.0, 0.0)
        m_ref[t] = _dot_nt(rt, ct)
        return carry

    lax.fori_loop(0, tt, body, 0, unroll=4)


def _peer_coef_call(it, jt, gt, tt):
    t, hk = it.shape
    row = pl.BlockSpec((tt, hk), lambda i: (i, 0))
    return pl.pallas_call(
        _peer_coef_kernel,
        grid=(t // tt,),
        in_specs=[row, row, row],
        out_specs=pl.BlockSpec((tt, N_KEYS, N_KEYS), lambda i: (i, 0, 0)),
        out_shape=jax.ShapeDtypeStruct((t, N_KEYS, N_KEYS), jnp.float32),
        compiler_params=pltpu.CompilerParams(dimension_semantics=("arbitrary",), vmem_limit_bytes=_VMEM_LIMIT),
        name="peer_coef",
    )(it, jt, gt)


def _peer_dense_kernel(hn_ref, h_ref, ut_ref, v_ref, m_ref, o_ref, *, ib):
    e = pl.program_id(1)

    @pl.when(e == 0)
    def _():
        o_ref[...] = h_ref[...]

    a = jax.nn.gelu(jnp.dot(hn_ref[...], ut_ref[...], preferred_element_type=jnp.float32))
    c = jnp.concatenate([a[:, ii * N_KEYS:(ii + 1) * N_KEYS] * m_ref[:, ii, :] for ii in range(ib)], axis=1)
    o_ref[...] += jnp.dot(c.astype(_MXU_DTYPE), v_ref[...], preferred_element_type=jnp.float32)


def _peer_dense_call(hn, h1, ut, v, m3, tm, ib):
    t, d = hn.shape
    te = ib * N_KEYS
    kern = functools.partial(_peer_dense_kernel, ib=ib)
    return pl.pallas_call(
        kern,
        grid=(t // tm, N_KEYS // ib),
        in_specs=[pl.BlockSpec((tm, d), lambda ti, e: (ti, 0)),
                  pl.BlockSpec((tm, d), lambda ti, e: (ti, 0)),
                  pl.BlockSpec((d, te), lambda ti, e: (0, e)),
                  pl.BlockSpec((te, d), lambda ti, e: (e, 0)),
                  pl.BlockSpec((tm, ib, N_KEYS), lambda ti, e: (ti, e, 0))],
        out_specs=pl.BlockSpec((tm, d), lambda ti, e: (ti, 0)),
        out_shape=jax.ShapeDtypeStruct((t, d), jnp.float32),
        compiler_params=pltpu.CompilerParams(dimension_semantics=("arbitrary", "arbitrary"),
                                             vmem_limit_bytes=_VMEM_LIMIT),
        name="peer_dense",
    )(hn, h1, ut, v, m3)


def _rope_inv(rot_dim, period):
    half = rot_dim // 2
    inv = jnp.power(jnp.float32(ROPE_THETA), -jnp.arange(half, dtype=jnp.float32) * 2.0 / rot_dim)
    lane = np.arange(LANES)
    in_rot = (lane % period) < rot_dim
    pat = jnp.where(jnp.asarray(in_rot), inv[jnp.asarray(lane % period % half)], 0.0)
    return pat.reshape(1, LANES).astype(jnp.float32)


def _pack_w_in(w):
    d = w.shape[0]
    z = lambda n: jnp.zeros((d, n), w.dtype)
    segs = [w[:, 0:1536 + KV_LATENT]]
    for h in range(IDX_HEADS):
        segs += [w[:, 1792 + h * IDX_DIM:1792 + (h + 1) * IDX_DIM], z(LANES - IDX_DIM)]
    segs += [w[:, 2048:2116], z(LANES - 68)]
    segs += [w[:, 2116:]]
    out = jnp.concatenate(segs, axis=1)
    assert out.shape[1] == _W_COLS
    return out.astype(_MXU_DTYPE)


def kernel(x, positions, norm1_g, w_in, v_norm_g, v_norm_b, spatial_w, spatial_b, kv_norm_g, w_uk, w_uv,
           q_norm_g, k_norm_g, w_a_out, w_b_out, w_o, norm2_g, peer_wq, peer_subkeys, peer_u, peer_v):
    bsz, s, d = x.shape
    t = bsz * s
    depth = w_in.shape[0]
    mx = _MXU_DTYPE
    n_sel = min(TOPK_MAX, s // 4)
    tm = min(256, t)
    tk = min(256, s // 2)
    invq = _rope_inv(ROT_DIM, LANES)
    invi = _rope_inv(IDX_ROT, LANES)
    pos = positions.reshape(t, 1).astype(jnp.float32)
    r2 = lambda a: a.reshape(1, -1)
    h = x.reshape(t, d)
    for l in range(depth):
        ma, gb, q, k, v, qi, ki, wi = _inproj_call(
            h, pos, r2(norm1_g[l]), _pack_w_in(w_in[l]), r2(v_norm_g[l]), r2(v_norm_b[l]), spatial_w[l],
            spatial_b[l].T, r2(kv_norm_g[l]), w_uk[l].astype(mx), w_uv[l].astype(mx), r2(q_norm_g[l]),
            r2(k_norm_g[l]), w_a_out[l].astype(mx), invq, invi, tm)
        b3 = lambda a: a.reshape(bsz, s, a.shape[-1])
        kT = b3(k).transpose(0, 2, 1)
        kiT = b3(ki).transpose(0, 2, 1)
        yb = _dsa_call(b3(q), b3(qi), b3(wi), kT, kiT, b3(v), n_sel, tk).reshape(t, 512)
        sk = peer_subkeys[l].reshape(2 * PEER_HEADS, N_KEYS, PEER_HALF).astype(mx)
        h1, hn, sub = _merge_call(h, ma, gb, yb, w_b_out[l].astype(mx), w_o[l].astype(mx), r2(norm2_g[l]),
                                  peer_wq[l].astype(mx), sk, tm)
        ei, ej, eg = _peer_topk_call(sub, min(128, t))
        m3 = _peer_coef_call(ei.T, ej.T, eg.T, min(32, t))
        h = _peer_dense_call(hn, h1, peer_u[l].T.astype(mx), peer_v[l].astype(mx), m3, min(512, t), 8)
    return h.reshape(bsz, s, d)
```

```python
import functools
import math

import jax
import jax.numpy as jnp
import numpy as np
from jax import lax
from jax.experimental import pallas as pl
from jax.experimental.pallas import tpu as pltpu

EPS = 1e-6
ROPE_THETA = 500000.0
CHUNK = 128
A_GROUPS = 4
A_GROUP_DIM = 128
N_HEADS = 4
HEAD_DIM = 128
KV_LATENT = 256
ROT_DIM = HEAD_DIM // 4
IDX_HEADS = 4
IDX_DIM = 64
IDX_ROT = IDX_DIM // 4
TOPK_MAX = 256
Q_BLOCK = 128
PEER_HEADS = 8
PEER_HALF = 128
N_KEYS = 128
PEER_TOPK = 16
LANES = 128
_PEER_IB = 8

_MXU_DTYPE = jnp.bfloat16
_INT_MIN = -2147483648
_I16_MIN = -32768
_NEG_BIG = -1e30
_VMEM_LIMIT = 56 * 1024 * 1024
_V_EXT = 2 * HEAD_DIM


def _dot(a, b):
    return jnp.dot(a.astype(_MXU_DTYPE), b.astype(_MXU_DTYPE), preferred_element_type=jnp.float32)


def _dot_nt(a, b):
    return lax.dot_general(a.astype(_MXU_DTYPE), b.astype(_MXU_DTYPE), (((1,), (1,)), ((), ())),
                           preferred_element_type=jnp.float32)


def _rms(x, g):
    return x * lax.rsqrt(jnp.mean(x * x, axis=-1, keepdims=True) + EPS) * g


def _rope(x, cos_t, sin_lo, sin_hi, half):
    n = x.shape[-1]
    x_up = pltpu.roll(x, n - half, 1)
    x_dn = pltpu.roll(x, half, 1)
    return x * cos_t + x_up * sin_lo + x_dn * sin_hi


def _rope_tables(pos, inv, half):
    ang = pos * inv
    c = jnp.cos(ang)
    s = jnp.sin(ang)
    lane = lax.broadcasted_iota(jnp.int32, ang.shape, 1)
    s_lo = jnp.where(lane < half, -s, 0.0)
    s_hi = jnp.where(lane >= half, s, 0.0)
    return c, s_lo, s_hi


_OFF_U, _OFF_V, _OFF_Q, _OFF_C, _OFF_QI, _OFF_KI, _OFF_G = 0, 512, 1024, 1536, 1792, 2304, 2432
_W_COLS = 2432 + 2048


def _inproj_kernel(x_ref, pos_ref, g1_ref, w_ref, vg_ref, vb_ref, ws_ref, bs_ref, kvg_ref, wuk_ref, wuv_ref,
                   qg_ref, kg_ref, wa_ref, invq_ref, invi_ref,
                   ma_ref, gb_ref, q_ref, k_ref, v_ref, qi_ref, ki_ref, wi_ref):
    tm = x_ref.shape[0]
    x = x_ref[...]
    xn = _rms(x, g1_ref[...]).astype(_MXU_DTYPE)

    def proj(off, width):
        return jnp.dot(xn, w_ref[:, off:off + width], preferred_element_type=jnp.float32)

    u = jax.nn.gelu(proj(_OFF_U, 512))
    v = jax.nn.gelu(proj(_OFF_V, 512))
    mu = jnp.mean(v, axis=-1, keepdims=True)
    vc = v - mu
    v = vc * lax.rsqrt(jnp.mean(vc * vc, axis=-1, keepdims=True) + EPS) * vg_ref[...] + vb_ref[...]
    v = v.astype(_MXU_DTYPE)
    row = lax.broadcasted_iota(jnp.int32, (CHUNK, CHUNK), 0)
    col = lax.broadcasted_iota(jnp.int32, (CHUNK, CHUNK), 1)
    z_chunks = []
    for c in range(tm // CHUNK):
        zg = []
        for g in range(A_GROUPS):
            wt = jnp.where(row >= col, ws_ref[g], 0.0).astype(_MXU_DTYPE)
            vcg = v[c * CHUNK:(c + 1) * CHUNK, g * A_GROUP_DIM:(g + 1) * A_GROUP_DIM]
            zg.append(jnp.dot(wt, vcg, preferred_element_type=jnp.float32) + bs_ref[:, g:g + 1])
        z_chunks.append(jnp.concatenate(zg, axis=1))
    z = jnp.concatenate(z_chunks, axis=0) if len(z_chunks) > 1 else z_chunks[0]
    ya = u * z
    gate_a = jax.nn.sigmoid(proj(_OFF_G, 1024))
    ma_ref[...] = gate_a * _dot(ya, wa_ref[...])
    gb_ref[...] = jax.nn.sigmoid(proj(_OFF_G + 1024, 1024))

    pos = pos_ref[...]
    cq, sq_lo, sq_hi = _rope_tables(pos, invq_ref[...], ROT_DIM // 2)
    ci, si_lo, si_hi = _rope_tables(pos, invi_ref[...], IDX_ROT // 2)

    q = proj(_OFF_Q, 512)
    qg = qg_ref[...]
    qh = [_rms(q[:, h * HEAD_DIM:(h + 1) * HEAD_DIM], qg) for h in range(N_HEADS)]
    qh = [_rope(t, cq, sq_lo, sq_hi, ROT_DIM // 2) for t in qh]
    q_ref[...] = (jnp.concatenate(qh, axis=1) * (HEAD_DIM ** -0.5 * math.log2(math.e))).astype(q_ref.dtype)

    c_n = _rms(proj(_OFF_C, KV_LATENT), kvg_ref[...]).astype(_MXU_DTYPE)
    kk = _rms(jnp.dot(c_n, wuk_ref[...], preferred_element_type=jnp.float32), kg_ref[...])
    k_ref[...] = _rope(kk, cq, sq_lo, sq_hi, ROT_DIM // 2).astype(k_ref.dtype)
    vv = jnp.dot(c_n, wuv_ref[...], preferred_element_type=jnp.float32)
    ones_col = jnp.where(lax.broadcasted_iota(jnp.int32, vv.shape, 1) == 0, 1.0, 0.0)
    v_ref[...] = jnp.concatenate([vv, ones_col], axis=1).astype(v_ref.dtype)

    qi = proj(_OFF_QI, 512)
    qis = [_rope(qi[:, h * LANES:(h + 1) * LANES], ci, si_lo, si_hi, IDX_ROT // 2) for h in range(IDX_HEADS)]
    qi_ref[...] = (jnp.concatenate(qis, axis=1) * (IDX_DIM ** -0.5)).astype(qi_ref.dtype)
    kw = proj(_OFF_KI, LANES)
    lane = lax.broadcasted_iota(jnp.int32, kw.shape, 1)
    ki = _rope(jnp.where(lane < IDX_DIM, kw, 0.0), ci, si_lo, si_hi, IDX_ROT // 2)
    ki_ref[...] = ki.astype(ki_ref.dtype)
    wi_ref[...] = kw * (IDX_HEADS ** -0.5)


def _inproj_call(x2, pos, g1, w_pack, vg, vb, ws, bs_t, kvg, wuk, wuv, qg, kg, wa, invq, invi, tm):
    t, d = x2.shape
    full = lambda a: pl.BlockSpec(a.shape, lambda i: (0,) * a.ndim)
    row = lambda w: pl.BlockSpec((tm, w), lambda i: (i, 0))
    f32, mx = jnp.float32, _MXU_DTYPE
    outs = [(1024, f32), (1024, f32), (512, mx), (128, mx), (_V_EXT, mx), (512, mx), (128, mx), (128, f32)]
    return pl.pallas_call(
        _inproj_kernel,
        grid=(t // tm,),
        in_specs=[row(d), row(1), full(g1), full(w_pack), full(vg), full(vb), full(ws), full(bs_t), full(kvg),
                  full(wuk), full(wuv), full(qg), full(kg), full(wa), full(invq), full(invi)],
        out_specs=[row(w) for w, _ in outs],
        out_shape=[jax.ShapeDtypeStruct((t, w), dt) for w, dt in outs],
        compiler_params=pltpu.CompilerParams(dimension_semantics=("arbitrary",), vmem_limit_bytes=_VMEM_LIMIT),
        name="inproj",
    )(x2, pos, g1, w_pack, vg, vb, ws, bs_t, kvg, wuk, wuv, qg, kg, wa, invq, invi)


def _dsa_kernel(q_ref, qi_ref, wi_ref, kT_ref, kiT_ref, v_ref, o_ref,
                keys_ref, hi_ref, lo_ref, cb_ref, wb_ref, m_ref, acc_ref, *, tk, n_sel):
    blk = pl.program_id(1)
    nq = Q_BLOCK
    ck = 2 * tk
    n_pairs = (blk * nq + nq + ck - 1) // ck

    wi = wi_ref[...]
    for h in range(IDX_HEADS):
        wb_ref[h] = jnp.broadcast_to(wi[:, IDX_DIM + h:IDX_DIM + h + 1], (nq, tk))

    def score_pair(j, causal_mask):
        for sub in range(2):
            off = pl.multiple_of(j * ck + sub * tk, tk)
            kt = kiT_ref[:, pl.ds(off, tk)]
            sc = None
            for h in range(IDX_HEADS):
                lg = jnp.dot(qi_ref[:, h * LANES:(h + 1) * LANES], kt, preferred_element_type=jnp.float32)
                term = wb_ref[h] * jnp.maximum(lg, 0.0)
                sc = term if sc is None else sc + term
            bits = pltpu.bitcast(sc, jnp.int32)
            key = bits ^ ((bits >> 31) & 0x7FFFFFFF)
            if causal_mask:
                qpos = blk * nq + lax.broadcasted_iota(jnp.int32, (nq, tk), 0)
                kpos = off + lax.broadcasted_iota(jnp.int32, (nq, tk), 1)
                key = jnp.where(kpos <= qpos, key, _INT_MIN)
            keys_ref[:, pl.ds(off, tk)] = key
            hi_ref[:, pl.ds(off, tk)] = (key >> 16).astype(jnp.int16)
            lo_ref[:, pl.ds(off, tk)] = ((key & 0xFFFF) + _I16_MIN).astype(jnp.int16)

    def score_body(j, carry):
        score_pair(j, False)
        return carry

    lax.fori_loop(0, n_pairs - 1, score_body, 0)
    score_pair(n_pairs - 1, True)

    @pl.when(n_pairs % 2 == 1)
    def _():
        pad = pl.ds(pl.multiple_of(n_pairs * ck, ck), ck)
        hi_ref[:, pad] = jnp.full((nq, ck), _I16_MIN, jnp.int16)
        lo_ref[:, pad] = jnp.full((nq, ck), _I16_MIN, jnp.int16)

    cq = 2 * ck
    n_quads = (n_pairs + 1) // 2
    one16, zero16 = jnp.int16(1), jnp.int16(0)

    def set_cand(col):
        cb_ref[...] = jnp.broadcast_to(col, (nq, LANES)).astype(jnp.int16)

    def count16(ref, strict):
        def body(j, cnt):
            off = pl.multiple_of(j * cq, cq)
            cb = cb_ref[...]
            for c in range(cq // LANES):
                blk16 = ref[:, pl.ds(off + c * LANES, LANES)]
                hit = (blk16 > cb) if strict else (blk16 >= cb)
                cnt = cnt + jnp.where(hit, one16, zero16)
            return cnt
        cnt = lax.fori_loop(0, n_quads, body, jnp.zeros((nq, LANES), jnp.int16))
        return jnp.sum(cnt.astype(jnp.int32), axis=1, keepdims=True)

    def kth16(ref, n_need):
        zero = jnp.zeros((nq, 1), jnp.int32)
        set_cand(zero)
        base = jnp.where(count16(ref, False) >= n_need, zero, _I16_MIN)

        def bit_step(b, base):
            cand = base | jnp.left_shift(jnp.int32(1), 14 - b)
            set_cand(cand)
            return jnp.where(count16(ref, False) >= n_need, cand, base)

        return lax.fori_loop(0, 15, bit_step, base)

    thr_hi = kth16(hi_ref, n_sel)
    set_cand(thr_hi)
    n_need_lo = n_sel - count16(hi_ref, True)

    def keep_members(j, carry):
        off = pl.multiple_of(j * cq, cq)
        cb = cb_ref[...]
        for c in range(cq // LANES):
            sl = pl.ds(off + c * LANES, LANES)
            lo_ref[:, sl] = jnp.where(hi_ref[:, sl] == cb, lo_ref[:, sl], jnp.int16(_I16_MIN))
        return carry

    lax.fori_loop(0, n_quads, keep_members, 0)
    thr_lo = kth16(lo_ref, n_need_lo)
    set_cand(thr_lo)
    need = (n_need_lo - count16(lo_ref, True)).astype(jnp.float32)
    thr = thr_hi * 65536 + (thr_lo - _I16_MIN)
    thr = jnp.maximum(thr, _INT_MIN + 1)

    tri = (lax.broadcasted_iota(jnp.int32, (tk, tk), 0) <= lax.broadcasted_iota(jnp.int32, (tk, tk), 1))
    tri = jnp.where(tri, 1.0, 0.0).astype(_MXU_DTYPE)
    m_ref[...] = jnp.full(m_ref.shape, _NEG_BIG, jnp.float32)
    acc_ref[...] = jnp.zeros(acc_ref.shape, jnp.float32)

    def attn_pair(j, room):
        off = pl.multiple_of(j * ck, ck)
        biases = []
        for sub in range(2):
            key = keys_ref[:, pl.ds(off + sub * tk, tk)]
            eq = key == thr
            pre = jnp.dot(jnp.where(eq, 1.0, 0.0).astype(_MXU_DTYPE), tri, preferred_element_type=jnp.float32)
            sel = (key > thr) | (eq & (pre <= room))
            biases.append(jnp.where(sel, 0.0, _NEG_BIG))
            room = room - pre[:, tk - 1:tk]
        bias = jnp.concatenate(biases, axis=1)
        kt = kT_ref[:, pl.ds(off, ck)]
        vt = v_ref[pl.ds(off, ck), :]
        for h in range(N_HEADS):
            s = jnp.dot(q_ref[:, h * HEAD_DIM:(h + 1) * HEAD_DIM], kt, preferred_element_type=jnp.float32) + bias
            m_old = m_ref[h]
            m_new = jnp.maximum(m_old, jnp.max(s, axis=1, keepdims=True))
            alpha = jnp.exp2(m_old - m_new)
            p = jnp.exp2(s - jnp.concatenate([m_new] * (ck // LANES), axis=1))
            acc_ref[h] = (jnp.concatenate([alpha] * (_V_EXT // LANES), axis=1) * acc_ref[h]
                          + jnp.dot(p.astype(_MXU_DTYPE), vt, preferred_element_type=jnp.float32))
            m_ref[h] = m_new
        return room

    lax.fori_loop(0, n_pairs, attn_pair, need)
    ys = []
    for h in range(N_HEADS):
        a = acc_ref[h]
        ys.append(a[:, :HEAD_DIM] / a[:, HEAD_DIM:HEAD_DIM + 1])
    o_ref[...] = jnp.concatenate(ys, axis=1).astype(o_ref.dtype)


def _dsa_call(q, qi, wi, kT, kiT, v, n_sel, tk):
    b, s, _ = q.shape
    nblk = s // Q_BLOCK
    assert s % (4 * tk) == 0 and tk % Q_BLOCK == 0 and s // LANES < 2 ** 15
    qspec = lambda w: pl.BlockSpec((None, Q_BLOCK, w), lambda bi, i: (bi, i, 0))
    kern = functools.partial(_dsa_kernel, tk=tk, n_sel=n_sel)
    return pl.pallas_call(
        kern,
        grid=(b, nblk),
        in_specs=[qspec(512), qspec(512), qspec(128),
                  pl.BlockSpec((None, HEAD_DIM, s), lambda bi, i: (bi, 0, 0)),
                  pl.BlockSpec((None, LANES, s), lambda bi, i: (bi, 0, 0)),
                  pl.BlockSpec((None, s, _V_EXT), lambda bi, i: (bi, 0, 0))],
        out_specs=qspec(512),
        out_shape=jax.ShapeDtypeStruct((b, s, 512), _MXU_DTYPE),
        scratch_shapes=[pltpu.VMEM((Q_BLOCK, s), jnp.int32),
                        pltpu.VMEM((Q_BLOCK, s), jnp.int16),
                        pltpu.VMEM((Q_BLOCK, s), jnp.int16),
                        pltpu.VMEM((Q_BLOCK, LANES), jnp.int16),
                        pltpu.VMEM((IDX_HEADS, Q_BLOCK, tk), jnp.float32),
                        pltpu.VMEM((N_HEADS, Q_BLOCK, LANES), jnp.float32),
                        pltpu.VMEM((N_HEADS, Q_BLOCK, _V_EXT), jnp.float32)],
        compiler_params=pltpu.CompilerParams(dimension_semantics=("arbitrary", "arbitrary"),
                                             vmem_limit_bytes=_VMEM_LIMIT),
        name="dsa",
    )(q, qi, wi, kT, kiT, v)


def _merge_kernel(x_ref, ma_ref, gb_ref, yb_ref, wb_ref, wo_ref, g2_ref, wq_ref, sk_ref, h_ref, hn_ref, sub_ref):
    merged = ma_ref[...] + gb_ref[...] * jnp.dot(yb_ref[...], wb_ref[...], preferred_element_type=jnp.float32)
    h1 = x_ref[...] + _dot(merged, wo_ref[...])
    h_ref[...] = h1
    hn = _rms(h1, g2_ref[...]).astype(_MXU_DTYPE)
    hn_ref[...] = hn
    qq = jnp.dot(hn, wq_ref[...], preferred_element_type=jnp.float32).astype(_MXU_DTYPE)
    for hp in range(2 * PEER_HEADS):
        sub_ref[hp] = _dot_nt(sk_ref[hp], qq[:, hp * PEER_HALF:(hp + 1) * PEER_HALF])


def _merge_call(x2, ma, gb, yb, wb, wo, g2, wq, sk, tm):
    t, d = x2.shape
    full = lambda a: pl.BlockSpec(a.shape, lambda i: (0,) * a.ndim)
    row = lambda w: pl.BlockSpec((tm, w), lambda i: (i, 0))
    return pl.pallas_call(
        _merge_kernel,
        grid=(t // tm,),
        in_specs=[row(d), row(d), row(d), row(512), full(wb), full(wo), full(g2), full(wq), full(sk)],
        out_specs=[row(d), row(d), pl.BlockSpec((2 * PEER_HEADS, N_KEYS, tm), lambda i: (0, 0, i))],
        out_shape=[jax.ShapeDtypeStruct((t, d), jnp.float32), jax.ShapeDtypeStruct((t, d), _MXU_DTYPE),
                   jax.ShapeDtypeStruct((2 * PEER_HEADS, N_KEYS, t), jnp.float32)],
        compiler_params=pltpu.CompilerParams(dimension_semantics=("arbitrary",), vmem_limit_bytes=_VMEM_LIMIT),
        name="merge",
    )(x2, ma, gb, yb, wb, wo, g2, wq, sk)


def _topk_rows(vals, idx_payload, k):
    n_rows = vals.shape[0]
    rid = lax.broadcasted_iota(jnp.int32, vals.shape, 0)
    tops, pays = [], []
    for _ in range(k):
        m = jnp.max(vals, axis=0, keepdims=True)
        pos = jnp.min(jnp.where(vals == m, rid, n_rows), axis=0, keepdims=True)
        hit = rid == pos
        tops.append(m)
        if idx_payload is None:
            pays.append(pos)
        else:
            pays.append(jnp.max(jnp.where(hit, idx_payload, -1), axis=0, keepdims=True))
        vals = jnp.where(hit, -jnp.inf, vals)
    return tops, pays


_PEER_PAIR_COUNTS = [PEER_TOPK // (a + 1) for a in range(PEER_TOPK)]
_PEER_CAND_ROWS = -(-sum(_PEER_PAIR_COUNTS) // 8) * 8


def _peer_topk_kernel(sub_ref, i_ref, j_ref, g_ref):
    kk = PEER_TOPK
    tt = sub_ref.shape[-1]

    def head(h, carry):
        s1, i1 = _topk_rows(sub_ref[2 * h], None, kk)
        s2, i2 = _topk_rows(sub_ref[2 * h + 1], None, kk)
        s2c = jnp.concatenate(s2, axis=0)
        i2c = jnp.concatenate(i2, axis=0)
        n_pad = _PEER_CAND_ROWS - sum(_PEER_PAIR_COUNTS)
        cand = jnp.concatenate([s1[a] + s2c[0:nb] for a, nb in enumerate(_PEER_PAIR_COUNTS)]
                               + [jnp.full((n_pad, tt), -jnp.inf, jnp.float32)], axis=0)
        cidx = jnp.concatenate([i1[a] * N_KEYS + i2c[0:nb] for a, nb in enumerate(_PEER_PAIR_COUNTS)]
                               + [jnp.zeros((n_pad, tt), jnp.int32)], axis=0)
        best, exp_id = _topk_rows(cand, cidx, kk)
        best = jnp.concatenate(best, axis=0)
        exp_id = jnp.concatenate(exp_id, axis=0)
        e = jnp.exp(best - best[0:1])
        gate = e / jnp.sum(e, axis=0, keepdims=True)
        rows = pl.ds(pl.multiple_of(h * kk, kk), kk)
        i_ref[rows, :] = exp_id >> 7
        j_ref[rows, :] = exp_id & (N_KEYS - 1)
        g_ref[rows, :] = gate
        return carry

    lax.fori_loop(0, PEER_HEADS, head, 0)


def _peer_topk_call(sub, tt):
    _, _, t = sub.shape
    hk = PEER_HEADS * PEER_TOPK
    out = pl.BlockSpec((hk, tt), lambda i: (0, i))
    return pl.pallas_call(
        _peer_topk_kernel,
        grid=(t // tt,),
        in_specs=[pl.BlockSpec((2 * PEER_HEADS, N_KEYS, tt), lambda i: (0, 0, i))],
        out_specs=[out, out, out],
        out_shape=[jax.ShapeDtypeStruct((hk, t), jnp.int32), jax.ShapeDtypeStruct((hk, t), jnp.int32),
                   jax.ShapeDtypeStruct((hk, t), jnp.float32)],
        compiler_params=pltpu.CompilerParams(dimension_semantics=("arbitrary",), vmem_limit_bytes=_VMEM_LIMIT),
        name="peer_topk",
    )(sub)


def _peer_coef_kernel(i_ref, j_ref, g_ref, m_ref):
    tt = i_ref.shape[0]
    rid = lax.broadcasted_iota(jnp.int32, (N_KEYS, LANES), 0)

    def body(t, carry):
        irow = i_ref[pl.ds(t, 1), :]
        jrow = j_ref[pl.ds(t, 1), :]
        grow = g_ref[pl.ds(t, 1), :]
        rt = jnp.where(rid == irow, grow, 0.0)
        ct = jnp.where(rid == jrow, 1.0, 0.0)
        mt = _dot_nt(rt, ct)
        for e in range(N_KEYS // _PEER_IB):
            m_ref[e, t] = mt[e * _PEER_IB:(e + 1) * _PEER_IB, :]
        return carry

    lax.fori_loop(0, tt, body, 0, unroll=8)


def _peer_coef_call(it, jt, gt, tt):
    t, hk = it.shape
    row = pl.BlockSpec((tt, hk), lambda i: (i, 0))
    nb = N_KEYS // _PEER_IB
    return pl.pallas_call(
        _peer_coef_kernel,
        grid=(t // tt,),
        in_specs=[row, row, row],
        out_specs=pl.BlockSpec((nb, tt, _PEER_IB, N_KEYS), lambda i: (0, i, 0, 0)),
        out_shape=jax.ShapeDtypeStruct((nb, t, _PEER_IB, N_KEYS), jnp.float32),
        compiler_params=pltpu.CompilerParams(dimension_semantics=("arbitrary",), vmem_limit_bytes=_VMEM_LIMIT),
        name="peer_coef",
    )(it, jt, gt)


def _peer_dense_kernel(hn_ref, h_ref, ut_ref, v_ref, m_ref, o_ref, *, ib):
    e = pl.program_id(1)

    @pl.when(e == 0)
    def _():
        o_ref[...] = h_ref[...]

    tm = hn_ref.shape[0]
    a = jax.nn.gelu(jnp.dot(hn_ref[...], ut_ref[...], preferred_element_type=jnp.float32))
    c = jnp.concatenate([a[:, ii * N_KEYS:(ii + 1) * N_KEYS] * m_ref[pl.ds(ii, tm, stride=ib), :]
                         for ii in range(ib)], axis=1)
    o_ref[...] += jnp.dot(c.astype(_MXU_DTYPE), v_ref[...], preferred_element_type=jnp.float32)


def _peer_dense_call(hn, h1, ut, v, m3, tm, ib):
    t, d = hn.shape
    te = ib * N_KEYS
    kern = functools.partial(_peer_dense_kernel, ib=ib)
    return pl.pallas_call(
        kern,
        grid=(t // tm, N_KEYS // ib),
        in_specs=[pl.BlockSpec((tm, d), lambda ti, e: (ti, 0)),
                  pl.BlockSpec((tm, d), lambda ti, e: (ti, 0)),
                  pl.BlockSpec((d, te), lambda ti, e: (0, e)),
                  pl.BlockSpec((te, d), lambda ti, e: (e, 0)),
                  pl.BlockSpec((None, tm * ib, N_KEYS), lambda ti, e: (e, ti, 0))],
        out_specs=pl.BlockSpec((tm, d), lambda ti, e: (ti, 0)),
        out_shape=jax.ShapeDtypeStruct((t, d), jnp.float32),
        compiler_params=pltpu.CompilerParams(dimension_semantics=("arbitrary", "arbitrary"),
                                             vmem_limit_bytes=_VMEM_LIMIT),
        name="peer_dense",
    )(hn, h1, ut, v, m3)


def _rope_inv(rot_dim, period):
    half = rot_dim // 2
    inv = jnp.power(jnp.float32(ROPE_THETA), -jnp.arange(half, dtype=jnp.float32) * 2.0 / rot_dim)
    lane = np.arange(LANES)
    in_rot = (lane % period) < rot_dim
    pat = jnp.where(jnp.asarray(in_rot), inv[jnp.asarray(lane % period % half)], 0.0)
    return pat.reshape(1, LANES).astype(jnp.float32)


def _pack_w_in(w):
    d = w.shape[0]
    z = lambda n: jnp.zeros((d, n), w.dtype)
    segs = [w[:, 0:1536 + KV_LATENT]]
    for h in range(IDX_HEADS):
        segs += [w[:, 1792 + h * IDX_DIM:1792 + (h + 1) * IDX_DIM], z(LANES - IDX_DIM)]
    segs += [w[:, 2048:2116], z(LANES - 68)]
    segs += [w[:, 2116:]]
    out = jnp.concatenate(segs, axis=1)
    assert out.shape[1] == _W_COLS
    return out.astype(_MXU_DTYPE)


def kernel(x, positions, norm1_g, w_in, v_norm_g, v_norm_b, spatial_w, spatial_b, kv_norm_g, w_uk, w_uv,
           q_norm_g, k_norm_g, w_a_out, w_b_out, w_o, norm2_g, peer_wq, peer_subkeys, peer_u, peer_v):
    bsz, s, d = x.shape
    t = bsz * s
    depth = w_in.shape[0]
    mx = _MXU_DTYPE
    n_sel = min(TOPK_MAX, s // 4)
    tm = min(256, t)
    tk = min(256, s // 4)
    invq = _rope_inv(ROT_DIM, LANES)
    invi = _rope_inv(IDX_ROT, LANES)
    pos = positions.reshape(t, 1).astype(jnp.float32)
    r2 = lambda a: a.reshape(1, -1)
    h = x.reshape(t, d)
    for l in range(depth):
        ma, gb, q, k, v, qi, ki, wi = _inproj_call(
            h, pos, r2(norm1_g[l]), _pack_w_in(w_in[l]), r2(v_norm_g[l]), r2(v_norm_b[l]), spatial_w[l],
            spatial_b[l].T, r2(kv_norm_g[l]), w_uk[l].astype(mx), w_uv[l].astype(mx), r2(q_norm_g[l]),
            r2(k_norm_g[l]), w_a_out[l].astype(mx), invq, invi, tm)
        b3 = lambda a: a.reshape(bsz, s, a.shape[-1])
        kT = b3(k).transpose(0, 2, 1)
        kiT = b3(ki).transpose(0, 2, 1)
        yb = _dsa_call(b3(q), b3(qi), b3(wi), kT, kiT, b3(v), n_sel, tk).reshape(t, 512)
        sk = peer_subkeys[l].reshape(2 * PEER_HEADS, N_KEYS, PEER_HALF).astype(mx)
        h1, hn, sub = _merge_call(h, ma, gb, yb, w_b_out[l].astype(mx), w_o[l].astype(mx), r2(norm2_g[l]),
                                  peer_wq[l].astype(mx), sk, tm)
        ei, ej, eg = _peer_topk_call(sub, min(128, t))
        m4 = _peer_coef_call(ei.T, ej.T, eg.T, min(32, t))
        m3 = m4.reshape(N_KEYS // _PEER_IB, t * _PEER_IB, N_KEYS)
        h = _peer_dense_call(hn, h1, peer_u[l].T.astype(mx), peer_v[l].astype(mx), m3, min(512, t), _PEER_IB)
    return h.reshape(bsz, s, d)
```

```python
import functools
import math

import jax
import jax.numpy as jnp
import numpy as np
from jax import lax
from jax.experimental import pallas as pl
from jax.experimental.pallas import tpu as pltpu

EPS = 1e-6
ROPE_THETA = 500000.0
CHUNK = 128
A_GROUPS = 4
A_GROUP_DIM = 128
N_HEADS = 4
HEAD_DIM = 128
KV_LATENT = 256
ROT_DIM = HEAD_DIM // 4
IDX_HEADS = 4
IDX_DIM = 64
IDX_ROT = IDX_DIM // 4
TOPK_MAX = 256
Q_BLOCK = 128
PEER_HEADS = 8
PEER_HALF = 128
N_KEYS = 128
PEER_TOPK = 16
LANES = 128
_PEER_IB = 8

_MXU_DTYPE = jnp.bfloat16
_INT_MIN = -2147483648
_KEY_LOWEST_FINITE = -2139095040
_LIST_DEPTH = 12
_NEG_BIG = -1e30
_VMEM_LIMIT = 56 * 1024 * 1024
_V_EXT = 2 * HEAD_DIM


def _dot(a, b):
    return jnp.dot(a.astype(_MXU_DTYPE), b.astype(_MXU_DTYPE), preferred_element_type=jnp.float32)


def _dot_nt(a, b):
    return lax.dot_general(a.astype(_MXU_DTYPE), b.astype(_MXU_DTYPE), (((1,), (1,)), ((), ())),
                           preferred_element_type=jnp.float32)


def _rms(x, g):
    return x * lax.rsqrt(jnp.mean(x * x, axis=-1, keepdims=True) + EPS) * g


def _rope(x, cos_t, sin_lo, sin_hi, half):
    n = x.shape[-1]
    x_up = pltpu.roll(x, n - half, 1)
    x_dn = pltpu.roll(x, half, 1)
    return x * cos_t + x_up * sin_lo + x_dn * sin_hi


def _rope_tables(pos, inv, half):
    ang = pos * inv
    c = jnp.cos(ang)
    s = jnp.sin(ang)
    lane = lax.broadcasted_iota(jnp.int32, ang.shape, 1)
    s_lo = jnp.where(lane < half, -s, 0.0)
    s_hi = jnp.where(lane >= half, s, 0.0)
    return c, s_lo, s_hi


_OFF_U, _OFF_V, _OFF_Q, _OFF_C, _OFF_QI, _OFF_KI, _OFF_G = 0, 512, 1024, 1536, 1792, 2304, 2432
_W_COLS = 2432 + 2048


def _inproj_kernel(x_ref, pos_ref, g1_ref, w_ref, vg_ref, vb_ref, ws_ref, bs_ref, kvg_ref, wuk_ref, wuv_ref,
                   qg_ref, kg_ref, wa_ref, invq_ref, invi_ref,
                   ma_ref, gb_ref, q_ref, k_ref, v_ref, qi_ref, ki_ref, wi_ref):
    tm = x_ref.shape[0]
    x = x_ref[...]
    xn = _rms(x, g1_ref[...]).astype(_MXU_DTYPE)

    def proj(off, width):
        return jnp.dot(xn, w_ref[:, off:off + width], preferred_element_type=jnp.float32)

    u = jax.nn.gelu(proj(_OFF_U, 512))
    v = jax.nn.gelu(proj(_OFF_V, 512))
    mu = jnp.mean(v, axis=-1, keepdims=True)
    vc = v - mu
    v = vc * lax.rsqrt(jnp.mean(vc * vc, axis=-1, keepdims=True) + EPS) * vg_ref[...] + vb_ref[...]
    v = v.astype(_MXU_DTYPE)
    row = lax.broadcasted_iota(jnp.int32, (CHUNK, CHUNK), 0)
    col = lax.broadcasted_iota(jnp.int32, (CHUNK, CHUNK), 1)
    z_chunks = []
    for c in range(tm // CHUNK):
        zg = []
        for g in range(A_GROUPS):
            wt = jnp.where(row >= col, ws_ref[g], 0.0).astype(_MXU_DTYPE)
            vcg = v[c * CHUNK:(c + 1) * CHUNK, g * A_GROUP_DIM:(g + 1) * A_GROUP_DIM]
            zg.append(jnp.dot(wt, vcg, preferred_element_type=jnp.float32) + bs_ref[:, g:g + 1])
        z_chunks.append(jnp.concatenate(zg, axis=1))
    z = jnp.concatenate(z_chunks, axis=0) if len(z_chunks) > 1 else z_chunks[0]
    ya = u * z
    gate_a = jax.nn.sigmoid(proj(_OFF_G, 1024))
    ma_ref[...] = gate_a * _dot(ya, wa_ref[...])
    gb_ref[...] = jax.nn.sigmoid(proj(_OFF_G + 1024, 1024))

    pos = pos_ref[...]
    cq, sq_lo, sq_hi = _rope_tables(pos, invq_ref[...], ROT_DIM // 2)
    ci, si_lo, si_hi = _rope_tables(pos, invi_ref[...], IDX_ROT // 2)

    q = proj(_OFF_Q, 512)
    qg = qg_ref[...]
    qh = [_rms(q[:, h * HEAD_DIM:(h + 1) * HEAD_DIM], qg) for h in range(N_HEADS)]
    qh = [_rope(t, cq, sq_lo, sq_hi, ROT_DIM // 2) for t in qh]
    q_ref[...] = (jnp.concatenate(qh, axis=1) * (HEAD_DIM ** -0.5 * math.log2(math.e))).astype(q_ref.dtype)

    c_n = _rms(proj(_OFF_C, KV_LATENT), kvg_ref[...]).astype(_MXU_DTYPE)
    kk = _rms(jnp.dot(c_n, wuk_ref[...], preferred_element_type=jnp.float32), kg_ref[...])
    k_ref[...] = _rope(kk, cq, sq_lo, sq_hi, ROT_DIM // 2).astype(k_ref.dtype)
    vv = jnp.dot(c_n, wuv_ref[...], preferred_element_type=jnp.float32)
    ones_col = jnp.where(lax.broadcasted_iota(jnp.int32, vv.shape, 1) == 0, 1.0, 0.0)
    v_ref[...] = jnp.concatenate([vv, ones_col], axis=1).astype(v_ref.dtype)

    qi = proj(_OFF_QI, 512)
    qis = [_rope(qi[:, h * LANES:(h + 1) * LANES], ci, si_lo, si_hi, IDX_ROT // 2) for h in range(IDX_HEADS)]
    qi_ref[...] = (jnp.concatenate(qis, axis=1) * (IDX_DIM ** -0.5)).astype(qi_ref.dtype)
    kw = proj(_OFF_KI, LANES)
    lane = lax.broadcasted_iota(jnp.int32, kw.shape, 1)
    ki = _rope(jnp.where(lane < IDX_DIM, kw, 0.0), ci, si_lo, si_hi, IDX_ROT // 2)
    ki_ref[...] = ki.astype(ki_ref.dtype)
    wi_ref[...] = kw * (IDX_HEADS ** -0.5)


def _inproj_call(x2, pos, g1, w_pack, vg, vb, ws, bs_t, kvg, wuk, wuv, qg, kg, wa, invq, invi, tm):
    t, d = x2.shape
    full = lambda a: pl.BlockSpec(a.shape, lambda i: (0,) * a.ndim)
    row = lambda w: pl.BlockSpec((tm, w), lambda i: (i, 0))
    f32, mx = jnp.float32, _MXU_DTYPE
    outs = [(1024, f32), (1024, f32), (512, mx), (128, mx), (_V_EXT, mx), (512, mx), (128, mx), (128, f32)]
    return pl.pallas_call(
        _inproj_kernel,
        grid=(t // tm,),
        in_specs=[row(d), row(1), full(g1), full(w_pack), full(vg), full(vb), full(ws), full(bs_t), full(kvg),
                  full(wuk), full(wuv), full(qg), full(kg), full(wa), full(invq), full(invi)],
        out_specs=[row(w) for w, _ in outs],
        out_shape=[jax.ShapeDtypeStruct((t, w), dt) for w, dt in outs],
        compiler_params=pltpu.CompilerParams(dimension_semantics=("arbitrary",), vmem_limit_bytes=_VMEM_LIMIT),
        name="inproj",
    )(x2, pos, g1, w_pack, vg, vb, ws, bs_t, kvg, wuk, wuv, qg, kg, wa, invq, invi)


def _dsa_kernel(q_ref, qi_ref, wi_ref, kT_ref, kiT_ref, v_ref, o_ref,
                sc_ref, list_ref, listT_ref, thr_ref, need_ref, wb_ref, m_ref, acc_ref, *, tk, n_sel):
    blk = pl.program_id(1)
    nq = Q_BLOCK
    ck = 2 * tk
    n_pairs = (blk * nq + nq + ck - 1) // ck

    wi = wi_ref[...]
    for h in range(IDX_HEADS):
        wb_ref[h] = jnp.broadcast_to(wi[:, IDX_DIM + h:IDX_DIM + h + 1], (nq, tk))

    def score_pair(j, causal_mask):
        for sub in range(2):
            off = pl.multiple_of(j * ck + sub * tk, tk)
            kt = kiT_ref[:, pl.ds(off, tk)]
            sc = None
            for h in range(IDX_HEADS):
                lg = jnp.dot(qi_ref[:, h * LANES:(h + 1) * LANES], kt, preferred_element_type=jnp.float32)
                term = wb_ref[h] * jnp.maximum(lg, 0.0)
                sc = term if sc is None else sc + term
            if causal_mask:
                qpos = blk * nq + lax.broadcasted_iota(jnp.int32, (nq, tk), 0)
                kpos = off + lax.broadcasted_iota(jnp.int32, (nq, tk), 1)
                sc = jnp.where(kpos <= qpos, sc, -jnp.inf)
            for c in range(tk // LANES):
                sc_ref[(j * ck + sub * tk) // LANES + c] = sc[:, c * LANES:(c + 1) * LANES]

    def score_body(j, carry):
        score_pair(j, False)
        return carry

    lax.fori_loop(0, n_pairs - 1, score_body, 0)
    score_pair(n_pairs - 1, True)

    tiles_per_pair = ck // LANES
    depth = _LIST_DEPTH

    def to_key(x):
        bits = pltpu.bitcast(x, jnp.int32)
        return bits ^ ((bits >> 31) & 0x7FFFFFFF)

    def to_score(key):
        return pltpu.bitcast(key ^ ((key >> 31) & 0x7FFFFFFF), jnp.float32)

    def kth_largest(count_ge, shape):
        zero = jnp.zeros(shape, jnp.int32)
        base = jnp.where(count_ge(zero) >= n_sel, zero, _INT_MIN)

        def bit_step(b, base):
            cand = base | jnp.left_shift(jnp.int32(1), 30 - b)
            return jnp.where(count_ge(cand) >= n_sel, cand, base)

        thr = lax.fori_loop(0, 31, bit_step, base)
        return jnp.maximum(thr, _KEY_LOWEST_FINITE)

    rows_per_step = 2 * 8
    for g in range(nq // rows_per_step):
        def insert(j, lists, g=g):
            lists = list(lists)
            for c in range(tiles_per_pair):
                for r in range(2):
                    x = sc_ref[j * tiles_per_pair + c, pl.ds(g * rows_per_step + r * 8, 8), :]
                    for i in range(depth):
                        cur = lists[r * depth + i]
                        lists[r * depth + i] = jnp.maximum(cur, x)
                        x = jnp.minimum(cur, x)
            return tuple(lists)

        lists = lax.fori_loop(0, n_pairs, insert, (jnp.full((8, LANES), -jnp.inf, jnp.float32),) * (2 * depth))
        for r in range(2):
            for i in range(depth):
                list_ref[i, pl.ds(g * rows_per_step + r * 8, 8), :] = lists[r * depth + i]

    for i in range(depth):
        listT_ref[i] = to_key(list_ref[i].T)

    def count_lists(cand_row):
        tot = None
        for i in range(depth):
            hit = (listT_ref[i] >= cand_row).astype(jnp.int32)
            tot = hit if tot is None else tot + hit
        return jnp.sum(tot, axis=0, keepdims=True)

    thr_row = kth_largest(count_lists, (1, nq))
    need_row = n_sel - count_lists(thr_row + 1)
    thr_ref[...] = to_score(jnp.broadcast_to(thr_row, (nq, nq))).T[:, 0:1]
    need_ref[...] = jnp.broadcast_to(need_row, (nq, nq)).astype(jnp.float32).T[:, 0:1]
    overflow = jnp.max(jnp.where(listT_ref[depth - 1] > thr_row, 1, 0))

    @pl.when(overflow > 0)
    def _():
        def count_all(cand):
            cb = jnp.broadcast_to(cand, (nq, LANES))

            def body(j, cnt):
                for c in range(tiles_per_pair):
                    cnt = cnt + (to_key(sc_ref[j * tiles_per_pair + c]) >= cb).astype(jnp.int32)
                return cnt

            cnt = lax.fori_loop(0, n_pairs, body, jnp.zeros((nq, LANES), jnp.int32))
            return jnp.sum(cnt, axis=1, keepdims=True)

        thr_col = kth_largest(count_all, (nq, 1))
        thr_ref[...] = to_score(thr_col)
        need_ref[...] = (n_sel - count_all(thr_col + 1)).astype(jnp.float32)

    thr = jnp.broadcast_to(thr_ref[...], (nq, tk))
    need = need_ref[...]

    tri = (lax.broadcasted_iota(jnp.int32, (tk, tk), 0) <= lax.broadcasted_iota(jnp.int32, (tk, tk), 1))
    tri = jnp.where(tri, 1.0, 0.0).astype(_MXU_DTYPE)
    m_ref[...] = jnp.full(m_ref.shape, _NEG_BIG, jnp.float32)
    acc_ref[...] = jnp.zeros(acc_ref.shape, jnp.float32)

    def attn_pair(j, room):
        off = pl.multiple_of(j * ck, ck)
        biases = []
        for sub in range(2):
            t0 = (j * ck + sub * tk) // LANES
            sc = jnp.concatenate([sc_ref[t0 + c] for c in range(tk // LANES)], axis=1)
            eq = sc == thr
            pre = jnp.dot(jnp.where(eq, 1.0, 0.0).astype(_MXU_DTYPE), tri, preferred_element_type=jnp.float32)
            sel = (sc > thr) | (eq & (pre <= room))
            biases.append(jnp.where(sel, 0.0, _NEG_BIG))
            room = room - pre[:, tk - 1:tk]
        bias = jnp.concatenate(biases, axis=1)
        kt = kT_ref[:, pl.ds(off, ck)]
        vt = v_ref[pl.ds(off, ck), :]
        for h in range(N_HEADS):
            s = jnp.dot(q_ref[:, h * HEAD_DIM:(h + 1) * HEAD_DIM], kt, preferred_element_type=jnp.float32) + bias
            m_old = m_ref[h]
            m_new = jnp.maximum(m_old, jnp.max(s, axis=1, keepdims=True))
            alpha = jnp.exp2(m_old - m_new)
            p = jnp.exp2(s - jnp.concatenate([m_new] * (ck // LANES), axis=1))
            acc_ref[h] = (jnp.concatenate([alpha] * (_V_EXT // LANES), axis=1) * acc_ref[h]
                          + jnp.dot(p.astype(_MXU_DTYPE), vt, preferred_element_type=jnp.float32))
            m_ref[h] = m_new
        return room

    lax.fori_loop(0, n_pairs, attn_pair, need)
    ys = []
    for h in range(N_HEADS):
        a = acc_ref[h]
        ys.append(a[:, :HEAD_DIM] / a[:, HEAD_DIM:HEAD_DIM + 1])
    o_ref[...] = jnp.concatenate(ys, axis=1).astype(o_ref.dtype)


def _dsa_call(q, qi, wi, kT, kiT, v, n_sel, tk):
    b, s, _ = q.shape
    nblk = s // Q_BLOCK
    assert s % (2 * tk) == 0 and tk % Q_BLOCK == 0
    qspec = lambda w: pl.BlockSpec((None, Q_BLOCK, w), lambda bi, i: (bi, i, 0))
    kern = functools.partial(_dsa_kernel, tk=tk, n_sel=n_sel)
    return pl.pallas_call(
        kern,
        grid=(b, nblk),
        in_specs=[qspec(512), qspec(512), qspec(128),
                  pl.BlockSpec((None, HEAD_DIM, s), lambda bi, i: (bi, 0, 0)),
                  pl.BlockSpec((None, LANES, s), lambda bi, i: (bi, 0, 0)),
                  pl.BlockSpec((None, s, _V_EXT), lambda bi, i: (bi, 0, 0))],
        out_specs=qspec(512),
        out_shape=jax.ShapeDtypeStruct((b, s, 512), _MXU_DTYPE),
        scratch_shapes=[pltpu.VMEM((s // LANES, Q_BLOCK, LANES), jnp.float32),
                        pltpu.VMEM((_LIST_DEPTH, Q_BLOCK, LANES), jnp.float32),
                        pltpu.VMEM((_LIST_DEPTH, LANES, Q_BLOCK), jnp.int32),
                        pltpu.VMEM((Q_BLOCK, 1), jnp.float32),
                        pltpu.VMEM((Q_BLOCK, 1), jnp.float32),
                        pltpu.VMEM((IDX_HEADS, Q_BLOCK, tk), jnp.float32),
                        pltpu.VMEM((N_HEADS, Q_BLOCK, LANES), jnp.float32),
                        pltpu.VMEM((N_HEADS, Q_BLOCK, _V_EXT), jnp.float32)],
        compiler_params=pltpu.CompilerParams(dimension_semantics=("arbitrary", "arbitrary"),
                                             vmem_limit_bytes=_VMEM_LIMIT),
        name="dsa",
    )(q, qi, wi, kT, kiT, v)


def _merge_kernel(x_ref, ma_ref, gb_ref, yb_ref, wb_ref, wo_ref, g2_ref, wq_ref, sk_ref, h_ref, hn_ref, sub_ref):
    merged = ma_ref[...] + gb_ref[...] * jnp.dot(yb_ref[...], wb_ref[...], preferred_element_type=jnp.float32)
    h1 = x_ref[...] + _dot(merged, wo_ref[...])
    h_ref[...] = h1
    hn = _rms(h1, g2_ref[...]).astype(_MXU_DTYPE)
    hn_ref[...] = hn
    qq = jnp.dot(hn, wq_ref[...], preferred_element_type=jnp.float32).astype(_MXU_DTYPE)
    for hp in range(2 * PEER_HEADS):
        sub_ref[hp] = _dot_nt(sk_ref[hp], qq[:, hp * PEER_HALF:(hp + 1) * PEER_HALF])


def _merge_call(x2, ma, gb, yb, wb, wo, g2, wq, sk, tm):
    t, d = x2.shape
    full = lambda a: pl.BlockSpec(a.shape, lambda i: (0,) * a.ndim)
    row = lambda w: pl.BlockSpec((tm, w), lambda i: (i, 0))
    return pl.pallas_call(
        _merge_kernel,
        grid=(t // tm,),
        in_specs=[row(d), row(d), row(d), row(512), full(wb), full(wo), full(g2), full(wq), full(sk)],
        out_specs=[row(d), row(d), pl.BlockSpec((2 * PEER_HEADS, N_KEYS, tm), lambda i: (0, 0, i))],
        out_shape=[jax.ShapeDtypeStruct((t, d), jnp.float32), jax.ShapeDtypeStruct((t, d), _MXU_DTYPE),
                   jax.ShapeDtypeStruct((2 * PEER_HEADS, N_KEYS, t), jnp.float32)],
        compiler_params=pltpu.CompilerParams(dimension_semantics=("arbitrary",), vmem_limit_bytes=_VMEM_LIMIT),
        name="merge",
    )(x2, ma, gb, yb, wb, wo, g2, wq, sk)


def _topk_rows(vals, idx_payload, k):
    n_rows = vals.shape[0]
    rid = lax.broadcasted_iota(jnp.int32, vals.shape, 0)
    tops, pays = [], []
    for _ in range(k):
        m = jnp.max(vals, axis=0, keepdims=True)
        pos = jnp.min(jnp.where(vals == m, rid, n_rows), axis=0, keepdims=True)
        hit = rid == pos
        tops.append(m)
        if idx_payload is None:
            pays.append(pos)
        else:
            pays.append(jnp.max(jnp.where(hit, idx_payload, -1), axis=0, keepdims=True))
        vals = jnp.where(hit, -jnp.inf, vals)
    return tops, pays


_PEER_PAIR_COUNTS = [PEER_TOPK // (a + 1) for a in range(PEER_TOPK)]
_PEER_CAND_ROWS = -(-sum(_PEER_PAIR_COUNTS) // 8) * 8


def _peer_topk_kernel(sub_ref, i_ref, j_ref, g_ref):
    kk = PEER_TOPK
    tt = sub_ref.shape[-1]

    def head(h, carry):
        s1, i1 = _topk_rows(sub_ref[2 * h], None, kk)
        s2, i2 = _topk_rows(sub_ref[2 * h + 1], None, kk)
        s2c = jnp.concatenate(s2, axis=0)
        i2c = jnp.concatenate(i2, axis=0)
        n_pad = _PEER_CAND_ROWS - sum(_PEER_PAIR_COUNTS)
        cand = jnp.concatenate([s1[a] + s2c[0:nb] for a, nb in enumerate(_PEER_PAIR_COUNTS)]
                               + [jnp.full((n_pad, tt), -jnp.inf, jnp.float32)], axis=0)
        cidx = jnp.concatenate([i1[a] * N_KEYS + i2c[0:nb] for a, nb in enumerate(_PEER_PAIR_COUNTS)]
                               + [jnp.zeros((n_pad, tt), jnp.int32)], axis=0)
        best, exp_id = _topk_rows(cand, cidx, kk)
        best = jnp.concatenate(best, axis=0)
        exp_id = jnp.concatenate(exp_id, axis=0)
        e = jnp.exp(best - best[0:1])
        gate = e / jnp.sum(e, axis=0, keepdims=True)
        rows = pl.ds(pl.multiple_of(h * kk, kk), kk)
        i_ref[rows, :] = exp_id >> 7
        j_ref[rows, :] = exp_id & (N_KEYS - 1)
        g_ref[rows, :] = gate
        return carry

    lax.fori_loop(0, PEER_HEADS, head, 0)


def _peer_topk_call(sub, tt):
    _, _, t = sub.shape
    hk = PEER_HEADS * PEER_TOPK
    out = pl.BlockSpec((hk, tt), lambda i: (0, i))
    return pl.pallas_call(
        _peer_topk_kernel,
        grid=(t // tt,),
        in_specs=[pl.BlockSpec((2 * PEER_HEADS, N_KEYS, tt), lambda i: (0, 0, i))],
        out_specs=[out, out, out],
        out_shape=[jax.ShapeDtypeStruct((hk, t), jnp.int32), jax.ShapeDtypeStruct((hk, t), jnp.int32),
                   jax.ShapeDtypeStruct((hk, t), jnp.float32)],
        compiler_params=pltpu.CompilerParams(dimension_semantics=("arbitrary",), vmem_limit_bytes=_VMEM_LIMIT),
        name="peer_topk",
    )(sub)


def _peer_coef_kernel(i_ref, j_ref, g_ref, m_ref):
    tt = i_ref.shape[0]
    rid = lax.broadcasted_iota(jnp.int32, (N_KEYS, LANES), 0)

    def body(t, carry):
        irow = i_ref[pl.ds(t, 1), :]
        jrow = j_ref[pl.ds(t, 1), :]
        grow = g_ref[pl.ds(t, 1), :]
        rt = jnp.where(rid == irow, grow, 0.0)
        ct = jnp.where(rid == jrow, 1.0, 0.0)
        mt = _dot_nt(rt, ct)
        for e in range(N_KEYS // _PEER_IB):
            m_ref[e, t] = mt[e * _PEER_IB:(e + 1) * _PEER_IB, :]
        return carry

    lax.fori_loop(0, tt, body, 0, unroll=8)


def _peer_coef_call(it, jt, gt, tt):
    t, hk = it.shape
    row = pl.BlockSpec((tt, hk), lambda i: (i, 0))
    nb = N_KEYS // _PEER_IB
    return pl.pallas_call(
        _peer_coef_kernel,
        grid=(t // tt,),
        in_specs=[row, row, row],
        out_specs=pl.BlockSpec((nb, tt, _PEER_IB, N_KEYS), lambda i: (0, i, 0, 0)),
        out_shape=jax.ShapeDtypeStruct((nb, t, _PEER_IB, N_KEYS), jnp.float32),
        compiler_params=pltpu.CompilerParams(dimension_semantics=("arbitrary",), vmem_limit_bytes=_VMEM_LIMIT),
        name="peer_coef",
    )(it, jt, gt)


def _peer_dense_kernel(hn_ref, h_ref, ut_ref, v_ref, m_ref, o_ref, *, ib):
    e = pl.program_id(1)

    @pl.when(e == 0)
    def _():
        o_ref[...] = h_ref[...]

    tm = hn_ref.shape[0]
    a = jax.nn.gelu(jnp.dot(hn_ref[...], ut_ref[...], preferred_element_type=jnp.float32))
    c = jnp.concatenate([a[:, ii * N_KEYS:(ii + 1) * N_KEYS] * m_ref[pl.ds(ii, tm, stride=ib), :]
                         for ii in range(ib)], axis=1)
    o_ref[...] += jnp.dot(c.astype(_MXU_DTYPE), v_ref[...], preferred_element_type=jnp.float32)


def _peer_dense_call(hn, h1, ut, v, m3, tm, ib):
    t, d = hn.shape
    te = ib * N_KEYS
    kern = functools.partial(_peer_dense_kernel, ib=ib)
    return pl.pallas_call(
        kern,
        grid=(t // tm, N_KEYS // ib),
        in_specs=[pl.BlockSpec((tm, d), lambda ti, e: (ti, 0)),
                  pl.BlockSpec((tm, d), lambda ti, e: (ti, 0)),
                  pl.BlockSpec((d, te), lambda ti, e: (0, e)),
                  pl.BlockSpec((te, d), lambda ti, e: (e, 0)),
                  pl.BlockSpec((None, tm * ib, N_KEYS), lambda ti, e: (e, ti, 0))],
        out_specs=pl.BlockSpec((tm, d), lambda ti, e: (ti, 0)),
        out_shape=jax.ShapeDtypeStruct((t, d), jnp.float32),
        compiler_params=pltpu.CompilerParams(dimension_semantics=("arbitrary", "arbitrary"),
                                             vmem_limit_bytes=_VMEM_LIMIT),
        name="peer_dense",
    )(hn, h1, ut, v, m3)


def _rope_inv(rot_dim, period):
    half = rot_dim // 2
    inv = jnp.power(jnp.float32(ROPE_THETA), -jnp.arange(half, dtype=jnp.float32) * 2.0 / rot_dim)
    lane = np.arange(LANES)
    in_rot = (lane % period) < rot_dim
    pat = jnp.where(jnp.asarray(in_rot), inv[jnp.asarray(lane % period % half)], 0.0)
    return pat.reshape(1, LANES).astype(jnp.float32)


def _pack_w_in(w):
    d = w.shape[0]
    z = lambda n: jnp.zeros((d, n), w.dtype)
    segs = [w[:, 0:1536 + KV_LATENT]]
    for h in range(IDX_HEADS):
        segs += [w[:, 1792 + h * IDX_DIM:1792 + (h + 1) * IDX_DIM], z(LANES - IDX_DIM)]
    segs += [w[:, 2048:2116], z(LANES - 68)]
    segs += [w[:, 2116:]]
    out = jnp.concatenate(segs, axis=1)
    assert out.shape[1] == _W_COLS
    return out.astype(_MXU_DTYPE)


def kernel(x, positions, norm1_g, w_in, v_norm_g, v_norm_b, spatial_w, spatial_b, kv_norm_g, w_uk, w_uv,
           q_norm_g, k_norm_g, w_a_out, w_b_out, w_o, norm2_g, peer_wq, peer_subkeys, peer_u, peer_v):
    bsz, s, d = x.shape
    t = bsz * s
    depth = w_in.shape[0]
    mx = _MXU_DTYPE
    n_sel = min(TOPK_MAX, s // 4)
    tm = min(256, t)
    tk = min(256, s // 4)
    invq = _rope_inv(ROT_DIM, LANES)
    invi = _rope_inv(IDX_ROT, LANES)
    pos = positions.reshape(t, 1).astype(jnp.float32)
    r2 = lambda a: a.reshape(1, -1)
    h = x.reshape(t, d)
    for l in range(depth):
        ma, gb, q, k, v, qi, ki, wi = _inproj_call(
            h, pos, r2(norm1_g[l]), _pack_w_in(w_in[l]), r2(v_norm_g[l]), r2(v_norm_b[l]), spatial_w[l],
            spatial_b[l].T, r2(kv_norm_g[l]), w_uk[l].astype(mx), w_uv[l].astype(mx), r2(q_norm_g[l]),
            r2(k_norm_g[l]), w_a_out[l].astype(mx), invq, invi, tm)
        b3 = lambda a: a.reshape(bsz, s, a.shape[-1])
        kT = b3(k).transpose(0, 2, 1)
        kiT = b3(ki).transpose(0, 2, 1)
        yb = _dsa_call(b3(q), b3(qi), b3(wi), kT, kiT, b3(v), n_sel, tk).reshape(t, 512)
        sk = peer_subkeys[l].reshape(2 * PEER_HEADS, N_KEYS, PEER_HALF).astype(mx)
        h1, hn, sub = _merge_call(h, ma, gb, yb, w_b_out[l].astype(mx), w_o[l].astype(mx), r2(norm2_g[l]),
                                  peer_wq[l].astype(mx), sk, tm)
        ei, ej, eg = _peer_topk_call(sub, min(128, t))
        m4 = _peer_coef_call(ei.T, ej.T, eg.T, min(32, t))
        m3 = m4.reshape(N_KEYS // _PEER_IB, t * _PEER_IB, N_KEYS)
        h = _peer_dense_call(hn, h1, peer_u[l].T.astype(mx), peer_v[l].astype(mx), m3, min(512, t), _PEER_IB)
    return h.reshape(bsz, s, d)
```

```python
import functools
import math

import jax
import jax.numpy as jnp
import numpy as np
from jax import lax
from jax.experimental import pallas as pl
from jax.experimental.pallas import tpu as pltpu

EPS = 1e-6
ROPE_THETA = 500000.0
CHUNK = 128
A_GROUPS = 4
A_GROUP_DIM = 128
N_HEADS = 4
HEAD_DIM = 128
KV_LATENT = 256
ROT_DIM = HEAD_DIM // 4
IDX_HEADS = 4
IDX_DIM = 64
IDX_ROT = IDX_DIM // 4
TOPK_MAX = 256
Q_BLOCK = 128
PEER_HEADS = 8
PEER_HALF = 128
N_KEYS = 128
PEER_TOPK = 16
LANES = 128
_PEER_IB = 8

_MXU_DTYPE = jnp.bfloat16
_INT_MIN = -2147483648
_KEY_LOWEST_FINITE = -2139095040
_LIST_DEPTH = 12
_NEG_BIG = -1e30
_VMEM_LIMIT = 56 * 1024 * 1024
_V_EXT = 2 * HEAD_DIM


def _dot(a, b):
    return jnp.dot(a.astype(_MXU_DTYPE), b.astype(_MXU_DTYPE), preferred_element_type=jnp.float32)


def _dot_nt(a, b):
    return lax.dot_general(a.astype(_MXU_DTYPE), b.astype(_MXU_DTYPE), (((1,), (1,)), ((), ())),
                           preferred_element_type=jnp.float32)


def _rms(x, g):
    return x * lax.rsqrt(jnp.mean(x * x, axis=-1, keepdims=True) + EPS) * g


def _rope(x, cos_t, sin_lo, sin_hi, half):
    n = x.shape[-1]
    x_up = pltpu.roll(x, n - half, 1)
    x_dn = pltpu.roll(x, half, 1)
    return x * cos_t + x_up * sin_lo + x_dn * sin_hi


def _rope_tables(pos, inv, half):
    ang = pos * inv
    c = jnp.cos(ang)
    s = jnp.sin(ang)
    lane = lax.broadcasted_iota(jnp.int32, ang.shape, 1)
    s_lo = jnp.where(lane < half, -s, 0.0)
    s_hi = jnp.where(lane >= half, s, 0.0)
    return c, s_lo, s_hi


_OFF_U, _OFF_V, _OFF_Q, _OFF_C, _OFF_QI, _OFF_KI, _OFF_G = 0, 512, 1024, 1536, 1792, 2304, 2432
_W_COLS = 2432 + 2048


def _inproj_kernel(x_ref, pos_ref, g1_ref, w_ref, vg_ref, vb_ref, ws_ref, bs_ref, kvg_ref, wuk_ref, wuv_ref,
                   qg_ref, kg_ref, wa_ref, invq_ref, invi_ref,
                   ma_ref, gb_ref, q_ref, k_ref, v_ref, qi_ref, ki_ref, wi_ref):
    tm = x_ref.shape[0]
    x = x_ref[...]
    xn = _rms(x, g1_ref[...]).astype(_MXU_DTYPE)

    def proj(off, width):
        return jnp.dot(xn, w_ref[:, off:off + width], preferred_element_type=jnp.float32)

    u = jax.nn.gelu(proj(_OFF_U, 512))
    v = jax.nn.gelu(proj(_OFF_V, 512))
    mu = jnp.mean(v, axis=-1, keepdims=True)
    vc = v - mu
    v = vc * lax.rsqrt(jnp.mean(vc * vc, axis=-1, keepdims=True) + EPS) * vg_ref[...] + vb_ref[...]
    v = v.astype(_MXU_DTYPE)
    row = lax.broadcasted_iota(jnp.int32, (CHUNK, CHUNK), 0)
    col = lax.broadcasted_iota(jnp.int32, (CHUNK, CHUNK), 1)
    z_chunks = []
    for c in range(tm // CHUNK):
        zg = []
        for g in range(A_GROUPS):
            wt = jnp.where(row >= col, ws_ref[g], 0.0).astype(_MXU_DTYPE)
            vcg = v[c * CHUNK:(c + 1) * CHUNK, g * A_GROUP_DIM:(g + 1) * A_GROUP_DIM]
            zg.append(jnp.dot(wt, vcg, preferred_element_type=jnp.float32) + bs_ref[:, g:g + 1])
        z_chunks.append(jnp.concatenate(zg, axis=1))
    z = jnp.concatenate(z_chunks, axis=0) if len(z_chunks) > 1 else z_chunks[0]
    ya = u * z
    gate_a = jax.nn.sigmoid(proj(_OFF_G, 1024))
    ma_ref[...] = gate_a * _dot(ya, wa_ref[...])
    gb_ref[...] = jax.nn.sigmoid(proj(_OFF_G + 1024, 1024))

    pos = pos_ref[...]
    cq, sq_lo, sq_hi = _rope_tables(pos, invq_ref[...], ROT_DIM // 2)
    ci, si_lo, si_hi = _rope_tables(pos, invi_ref[...], IDX_ROT // 2)

    q = proj(_OFF_Q, 512)
    qg = qg_ref[...]
    qh = [_rms(q[:, h * HEAD_DIM:(h + 1) * HEAD_DIM], qg) for h in range(N_HEADS)]
    qh = [_rope(t, cq, sq_lo, sq_hi, ROT_DIM // 2) for t in qh]
    q_ref[...] = (jnp.concatenate(qh, axis=1) * (HEAD_DIM ** -0.5 * math.log2(math.e))).astype(q_ref.dtype)

    c_n = _rms(proj(_OFF_C, KV_LATENT), kvg_ref[...]).astype(_MXU_DTYPE)
    kk = _rms(jnp.dot(c_n, wuk_ref[...], preferred_element_type=jnp.float32), kg_ref[...])
    k_ref[...] = _rope(kk, cq, sq_lo, sq_hi, ROT_DIM // 2).astype(k_ref.dtype)
    vv = jnp.dot(c_n, wuv_ref[...], preferred_element_type=jnp.float32)
    ones_col = jnp.where(lax.broadcasted_iota(jnp.int32, vv.shape, 1) == 0, 1.0, 0.0)
    v_ref[...] = jnp.concatenate([vv, ones_col], axis=1).astype(v_ref.dtype)

    qi = proj(_OFF_QI, 512)
    qis = [_rope(qi[:, h * LANES:(h + 1) * LANES], ci, si_lo, si_hi, IDX_ROT // 2) for h in range(IDX_HEADS)]
    qi_ref[...] = (jnp.concatenate(qis, axis=1) * (IDX_DIM ** -0.5)).astype(qi_ref.dtype)
    kw = proj(_OFF_KI, LANES)
    lane = lax.broadcasted_iota(jnp.int32, kw.shape, 1)
    ki = _rope(jnp.where(lane < IDX_DIM, kw, 0.0), ci, si_lo, si_hi, IDX_ROT // 2)
    ki_ref[...] = ki.astype(ki_ref.dtype)
    wi_ref[...] = kw * (IDX_HEADS ** -0.5)


def _inproj_call(x2, pos, g1, w_pack, vg, vb, ws, bs_t, kvg, wuk, wuv, qg, kg, wa, invq, invi, tm):
    t, d = x2.shape
    full = lambda a: pl.BlockSpec(a.shape, lambda i: (0,) * a.ndim)
    row = lambda w: pl.BlockSpec((tm, w), lambda i: (i, 0))
    f32, mx = jnp.float32, _MXU_DTYPE
    outs = [(1024, f32), (1024, f32), (512, mx), (128, mx), (_V_EXT, mx), (512, mx), (128, mx), (128, f32)]
    return pl.pallas_call(
        _inproj_kernel,
        grid=(t // tm,),
        in_specs=[row(d), row(1), full(g1), full(w_pack), full(vg), full(vb), full(ws), full(bs_t), full(kvg),
                  full(wuk), full(wuv), full(qg), full(kg), full(wa), full(invq), full(invi)],
        out_specs=[row(w) for w, _ in outs],
        out_shape=[jax.ShapeDtypeStruct((t, w), dt) for w, dt in outs],
        compiler_params=pltpu.CompilerParams(dimension_semantics=("arbitrary",), vmem_limit_bytes=_VMEM_LIMIT),
        name="inproj",
    )(x2, pos, g1, w_pack, vg, vb, ws, bs_t, kvg, wuk, wuv, qg, kg, wa, invq, invi)


def _dsa_kernel(q_ref, qi_ref, wi_ref, kT_ref, kiT_ref, v_ref, o_ref,
                sc_ref, list_ref, listT_ref, thr_ref, need_ref, wb_ref, m_ref, acc_ref, *, tk, n_sel):
    blk = pl.program_id(1)
    nq = Q_BLOCK
    ck = 2 * tk
    n_pairs = (blk * nq + nq + ck - 1) // ck

    wi = wi_ref[...]
    for h in range(IDX_HEADS):
        wb_ref[h] = jnp.broadcast_to(wi[:, IDX_DIM + h:IDX_DIM + h + 1], (nq, tk))

    def score_pair(j, causal_mask):
        for sub in range(2):
            off = pl.multiple_of(j * ck + sub * tk, tk)
            kt = kiT_ref[:, pl.ds(off, tk)]
            sc = None
            for h in range(IDX_HEADS):
                lg = jnp.dot(qi_ref[:, h * LANES:(h + 1) * LANES], kt, preferred_element_type=jnp.float32)
                term = wb_ref[h] * jnp.maximum(lg, 0.0)
                sc = term if sc is None else sc + term
            if causal_mask:
                qpos = blk * nq + lax.broadcasted_iota(jnp.int32, (nq, tk), 0)
                kpos = off + lax.broadcasted_iota(jnp.int32, (nq, tk), 1)
                sc = jnp.where(kpos <= qpos, sc, -jnp.inf)
            for c in range(tk // LANES):
                sc_ref[(j * ck + sub * tk) // LANES + c] = sc[:, c * LANES:(c + 1) * LANES]

    def score_body(j, carry):
        score_pair(j, False)
        return carry

    lax.fori_loop(0, n_pairs - 1, score_body, 0)
    score_pair(n_pairs - 1, True)

    tiles_per_pair = ck // LANES
    depth = _LIST_DEPTH

    def to_key(x):
        bits = pltpu.bitcast(x, jnp.int32)
        return bits ^ ((bits >> 31) & 0x7FFFFFFF)

    def to_score(key):
        return pltpu.bitcast(key ^ ((key >> 31) & 0x7FFFFFFF), jnp.float32)

    def kth_largest(count_ge, shape):
        zero = jnp.zeros(shape, jnp.int32)
        base = jnp.where(count_ge(zero) >= n_sel, zero, _INT_MIN)

        def bit_step(b, base):
            cand = base | jnp.left_shift(jnp.int32(1), 30 - b)
            return jnp.where(count_ge(cand) >= n_sel, cand, base)

        thr = lax.fori_loop(0, 31, bit_step, base)
        return jnp.maximum(thr, _KEY_LOWEST_FINITE)

    rows_per_step = 2 * 8
    for g in range(nq // rows_per_step):
        def insert(j, lists, g=g):
            lists = list(lists)
            for c in range(tiles_per_pair):
                for r in range(2):
                    x = sc_ref[j * tiles_per_pair + c, pl.ds(g * rows_per_step + r * 8, 8), :]
                    for i in range(depth):
                        cur = lists[r * depth + i]
                        lists[r * depth + i] = jnp.maximum(cur, x)
                        x = jnp.minimum(cur, x)
            return tuple(lists)

        lists = lax.fori_loop(0, n_pairs, insert, (jnp.full((8, LANES), -jnp.inf, jnp.float32),) * (2 * depth))
        for r in range(2):
            for i in range(depth):
                list_ref[i, pl.ds(g * rows_per_step + r * 8, 8), :] = lists[r * depth + i]

    for i in range(depth):
        listT_ref[i] = to_key(list_ref[i].T)

    def count_lists(cand_row):
        tot = None
        for i in range(depth):
            hit = (listT_ref[i] >= cand_row).astype(jnp.int32)
            tot = hit if tot is None else tot + hit
        return jnp.sum(tot, axis=0, keepdims=True)

    thr_row = kth_largest(count_lists, (1, nq))
    need_row = n_sel - count_lists(thr_row + 1)
    thr_ref[...] = to_score(jnp.broadcast_to(thr_row, (nq, nq))).T[:, 0:1]
    need_ref[...] = jnp.broadcast_to(need_row, (nq, nq)).astype(jnp.float32).T[:, 0:1]
    overflow = jnp.max(jnp.where(listT_ref[depth - 1] > thr_row, 1, 0))

    @pl.when(overflow > 0)
    def _():
        def count_all(cand):
            cb = jnp.broadcast_to(cand, (nq, LANES))

            def body(j, cnt):
                for c in range(tiles_per_pair):
                    cnt = cnt + (to_key(sc_ref[j * tiles_per_pair + c]) >= cb).astype(jnp.int32)
                return cnt

            cnt = lax.fori_loop(0, n_pairs, body, jnp.zeros((nq, LANES), jnp.int32))
            return jnp.sum(cnt, axis=1, keepdims=True)

        thr_col = kth_largest(count_all, (nq, 1))
        thr_ref[...] = to_score(thr_col)
        need_ref[...] = (n_sel - count_all(thr_col + 1)).astype(jnp.float32)

    thr = jnp.broadcast_to(thr_ref[...], (nq, tk))
    need = need_ref[...]

    tri = (lax.broadcasted_iota(jnp.int32, (tk, tk), 0) <= lax.broadcasted_iota(jnp.int32, (tk, tk), 1))
    tri = jnp.where(tri, 1.0, 0.0).astype(_MXU_DTYPE)
    m_ref[...] = jnp.full(m_ref.shape, _NEG_BIG, jnp.float32)
    acc_ref[...] = jnp.zeros(acc_ref.shape, jnp.float32)

    def attn_pair(j, room):
        off = pl.multiple_of(j * ck, ck)
        biases = []
        for sub in range(2):
            t0 = (j * ck + sub * tk) // LANES
            sc = jnp.concatenate([sc_ref[t0 + c] for c in range(tk // LANES)], axis=1)
            eq = sc == thr
            pre = jnp.dot(jnp.where(eq, 1.0, 0.0).astype(_MXU_DTYPE), tri, preferred_element_type=jnp.float32)
            sel = (sc > thr) | (eq & (pre <= room))
            biases.append(jnp.where(sel, 0.0, _NEG_BIG))
            room = room - pre[:, tk - 1:tk]
        bias = jnp.concatenate(biases, axis=1)
        kt = kT_ref[:, pl.ds(off, ck)]
        vt = v_ref[pl.ds(off, ck), :]
        for h in range(N_HEADS):
            s = jnp.dot(q_ref[:, h * HEAD_DIM:(h + 1) * HEAD_DIM], kt, preferred_element_type=jnp.float32) + bias
            m_old = m_ref[h]
            m_new = jnp.maximum(m_old, jnp.max(s, axis=1, keepdims=True))
            alpha = jnp.exp2(m_old - m_new)
            p = jnp.exp2(s - jnp.concatenate([m_new] * (ck // LANES), axis=1))
            acc_ref[h] = (jnp.concatenate([alpha] * (_V_EXT // LANES), axis=1) * acc_ref[h]
                          + jnp.dot(p.astype(_MXU_DTYPE), vt, preferred_element_type=jnp.float32))
            m_ref[h] = m_new
        return room

    lax.fori_loop(0, n_pairs, attn_pair, need)
    ys = []
    for h in range(N_HEADS):
        a = acc_ref[h]
        ys.append(a[:, :HEAD_DIM] / a[:, HEAD_DIM:HEAD_DIM + 1])
    o_ref[...] = jnp.concatenate(ys, axis=1).astype(o_ref.dtype)


def _dsa_call(q, qi, wi, kT, kiT, v, n_sel, tk):
    b, s, _ = q.shape
    nblk = s // Q_BLOCK
    assert s % (2 * tk) == 0 and tk % Q_BLOCK == 0
    qspec = lambda w: pl.BlockSpec((None, Q_BLOCK, w), lambda bi, i: (bi, i, 0))
    kern = functools.partial(_dsa_kernel, tk=tk, n_sel=n_sel)
    return pl.pallas_call(
        kern,
        grid=(b, nblk),
        in_specs=[qspec(512), qspec(512), qspec(128),
                  pl.BlockSpec((None, HEAD_DIM, s), lambda bi, i: (bi, 0, 0)),
                  pl.BlockSpec((None, LANES, s), lambda bi, i: (bi, 0, 0)),
                  pl.BlockSpec((None, s, _V_EXT), lambda bi, i: (bi, 0, 0))],
        out_specs=qspec(512),
        out_shape=jax.ShapeDtypeStruct((b, s, 512), _MXU_DTYPE),
        scratch_shapes=[pltpu.VMEM((s // LANES, Q_BLOCK, LANES), jnp.float32),
                        pltpu.VMEM((_LIST_DEPTH, Q_BLOCK, LANES), jnp.float32),
                        pltpu.VMEM((_LIST_DEPTH, LANES, Q_BLOCK), jnp.int32),
                        pltpu.VMEM((Q_BLOCK, 1), jnp.float32),
                        pltpu.VMEM((Q_BLOCK, 1), jnp.float32),
                        pltpu.VMEM((IDX_HEADS, Q_BLOCK, tk), jnp.float32),
                        pltpu.VMEM((N_HEADS, Q_BLOCK, LANES), jnp.float32),
                        pltpu.VMEM((N_HEADS, Q_BLOCK, _V_EXT), jnp.float32)],
        compiler_params=pltpu.CompilerParams(dimension_semantics=("arbitrary", "arbitrary"),
                                             vmem_limit_bytes=_VMEM_LIMIT),
        name="dsa",
    )(q, qi, wi, kT, kiT, v)


_TOK_CHUNK = 8 * LANES


def _merge_kernel(x_ref, ma_ref, gb_ref, yb_ref, wb_ref, wo_ref, g2_ref, wq_ref, sk_ref, h_ref, hn_ref, sub_ref):
    tm = x_ref.shape[0]
    merged = ma_ref[...] + gb_ref[...] * jnp.dot(yb_ref[...], wb_ref[...], preferred_element_type=jnp.float32)
    h1 = x_ref[...] + _dot(merged, wo_ref[...])
    h_ref[...] = h1
    hn = _rms(h1, g2_ref[...]).astype(_MXU_DTYPE)
    hn_ref[...] = hn
    qq = jnp.dot(hn, wq_ref[...], preferred_element_type=jnp.float32).astype(_MXU_DTYPE)
    groups = tm // LANES
    first = (pl.program_id(0) % (_TOK_CHUNK // tm)) * groups
    for hp in range(2 * PEER_HEADS):
        tile = _dot_nt(sk_ref[hp], qq[:, hp * PEER_HALF:(hp + 1) * PEER_HALF])
        for g in range(groups):
            sub_ref[hp, pl.ds(first + g, N_KEYS, stride=8), :] = tile[:, g * LANES:(g + 1) * LANES]


def _merge_call(x2, ma, gb, yb, wb, wo, g2, wq, sk, tm):
    t, d = x2.shape
    assert t % _TOK_CHUNK == 0 and _TOK_CHUNK % tm == 0
    full = lambda a: pl.BlockSpec(a.shape, lambda i: (0,) * a.ndim)
    row = lambda w: pl.BlockSpec((tm, w), lambda i: (i, 0))
    steps = _TOK_CHUNK // tm
    sub_shape = (2 * PEER_HEADS, t // _TOK_CHUNK, N_KEYS * 8, LANES)
    return pl.pallas_call(
        _merge_kernel,
        grid=(t // tm,),
        in_specs=[row(d), row(d), row(d), row(512), full(wb), full(wo), full(g2), full(wq), full(sk)],
        out_specs=[row(d), row(d),
                   pl.BlockSpec((2 * PEER_HEADS, None, N_KEYS * 8, LANES), lambda i: (0, i // steps, 0, 0))],
        out_shape=[jax.ShapeDtypeStruct((t, d), jnp.float32), jax.ShapeDtypeStruct((t, d), _MXU_DTYPE),
                   jax.ShapeDtypeStruct(sub_shape, jnp.float32)],
        compiler_params=pltpu.CompilerParams(dimension_semantics=("arbitrary",), vmem_limit_bytes=_VMEM_LIMIT),
        name="merge",
    )(x2, ma, gb, yb, wb, wo, g2, wq, sk)


def _insert_sorted(vals, pays, x, xp):
    for lvl in range(len(vals)):
        c = x > vals[lvl]
        vals[lvl], x = jnp.where(c, x, vals[lvl]), jnp.where(c, vals[lvl], x)
        pays[lvl], xp = jnp.where(c, xp, pays[lvl]), jnp.where(c, pays[lvl], xp)


_PEER_PAIRS = [(a, b) for a in range(PEER_TOPK) for b in range(PEER_TOPK // (a + 1))]
_KEYS_PER_STEP = 4


def _peer_topk_kernel(sub_ref, i_ref, j_ref, g_ref):
    kk = PEER_TOPK
    neg = jnp.full((8, LANES), -jnp.inf, jnp.float32)
    zero = jnp.zeros((8, LANES), jnp.int32)

    def sorted_top(hp):
        def step(k4, carry):
            vals, idxs = list(carry[0]), list(carry[1])
            for u in range(_KEYS_PER_STEP):
                key = k4 * _KEYS_PER_STEP + u
                x = sub_ref[hp, pl.ds(pl.multiple_of(key * 8, 8), 8), :]
                _insert_sorted(vals, idxs, x, zero + key)
            return tuple(vals), tuple(idxs)

        return lax.fori_loop(0, N_KEYS // _KEYS_PER_STEP, step, ((neg,) * kk, (zero,) * kk))

    def head(h, carry):
        s1, i1 = sorted_top(2 * h)
        s2, i2 = sorted_top(2 * h + 1)
        best, expert = [neg] * kk, [zero] * kk
        for a, b in _PEER_PAIRS:
            _insert_sorted(best, expert, s1[a] + s2[b], i1[a] * N_KEYS + i2[b])
        e = [jnp.exp(v - best[0]) for v in best]
        denom = e[0]
        for v in e[1:]:
            denom = denom + v
        for k in range(kk):
            i_ref[h * kk + k] = expert[k] >> 7
            j_ref[h * kk + k] = expert[k] & (N_KEYS - 1)
            g_ref[h * kk + k] = e[k] / denom
        return carry

    lax.fori_loop(0, PEER_HEADS, head, 0)


def _peer_topk_call(sub):
    hp, chunks, rows, _ = sub.shape
    hk = PEER_HEADS * PEER_TOPK
    out = pl.BlockSpec((hk, 8, LANES), lambda i: (0, i, 0))
    osd = lambda dt: jax.ShapeDtypeStruct((hk, chunks * 8, LANES), dt)
    return pl.pallas_call(
        _peer_topk_kernel,
        grid=(chunks,),
        in_specs=[pl.BlockSpec((hp, None, rows, LANES), lambda i: (0, i, 0, 0))],
        out_specs=[out, out, out],
        out_shape=[osd(jnp.int32), osd(jnp.int32), osd(jnp.float32)],
        compiler_params=pltpu.CompilerParams(dimension_semantics=("arbitrary",), vmem_limit_bytes=_VMEM_LIMIT),
        name="peer_topk",
    )(sub)


def _peer_coef_kernel(i_ref, j_ref, g_ref, m_ref):
    tt = i_ref.shape[0]
    rid = lax.broadcasted_iota(jnp.int32, (N_KEYS, LANES), 0)

    def body(t, carry):
        irow = i_ref[pl.ds(t, 1), :]
        jrow = j_ref[pl.ds(t, 1), :]
        grow = g_ref[pl.ds(t, 1), :]
        rt = jnp.where(rid == irow, grow, 0.0)
        ct = jnp.where(rid == jrow, 1.0, 0.0)
        mt = _dot_nt(rt, ct)
        for e in range(N_KEYS // _PEER_IB):
            m_ref[e, t] = mt[e * _PEER_IB:(e + 1) * _PEER_IB, :]
        return carry

    lax.fori_loop(0, tt, body, 0, unroll=8)


def _peer_coef_call(it, jt, gt, tt):
    t, hk = it.shape
    row = pl.BlockSpec((tt, hk), lambda i: (i, 0))
    nb = N_KEYS // _PEER_IB
    return pl.pallas_call(
        _peer_coef_kernel,
        grid=(t // tt,),
        in_specs=[row, row, row],
        out_specs=pl.BlockSpec((nb, tt, _PEER_IB, N_KEYS), lambda i: (0, i, 0, 0)),
        out_shape=jax.ShapeDtypeStruct((nb, t, _PEER_IB, N_KEYS), jnp.float32),
        compiler_params=pltpu.CompilerParams(dimension_semantics=("arbitrary",), vmem_limit_bytes=_VMEM_LIMIT),
        name="peer_coef",
    )(it, jt, gt)


def _peer_dense_kernel(hn_ref, h_ref, ut_ref, v_ref, m_ref, o_ref, *, ib):
    e = pl.program_id(1)

    @pl.when(e == 0)
    def _():
        o_ref[...] = h_ref[...]

    tm = hn_ref.shape[0]
    a = jax.nn.gelu(jnp.dot(hn_ref[...], ut_ref[...], preferred_element_type=jnp.float32))
    c = jnp.concatenate([a[:, ii * N_KEYS:(ii + 1) * N_KEYS] * m_ref[pl.ds(ii, tm, stride=ib), :]
                         for ii in range(ib)], axis=1)
    o_ref[...] += jnp.dot(c.astype(_MXU_DTYPE), v_ref[...], preferred_element_type=jnp.float32)


def _peer_dense_call(hn, h1, ut, v, m3, tm, ib):
    t, d = hn.shape
    te = ib * N_KEYS
    kern = functools.partial(_peer_dense_kernel, ib=ib)
    return pl.pallas_call(
        kern,
        grid=(t // tm, N_KEYS // ib),
        in_specs=[pl.BlockSpec((tm, d), lambda ti, e: (ti, 0)),
                  pl.BlockSpec((tm, d), lambda ti, e: (ti, 0)),
                  pl.BlockSpec((d, te), lambda ti, e: (0, e)),
                  pl.BlockSpec((te, d), lambda ti, e: (e, 0)),
                  pl.BlockSpec((None, tm * ib, N_KEYS), lambda ti, e: (e, ti, 0))],
        out_specs=pl.BlockSpec((tm, d), lambda ti, e: (ti, 0)),
        out_shape=jax.ShapeDtypeStruct((t, d), jnp.float32),
        compiler_params=pltpu.CompilerParams(dimension_semantics=("arbitrary", "arbitrary"),
                                             vmem_limit_bytes=_VMEM_LIMIT),
        name="peer_dense",
    )(hn, h1, ut, v, m3)


def _rope_inv(rot_dim, period):
    half = rot_dim // 2
    inv = jnp.power(jnp.float32(ROPE_THETA), -jnp.arange(half, dtype=jnp.float32) * 2.0 / rot_dim)
    lane = np.arange(LANES)
    in_rot = (lane % period) < rot_dim
    pat = jnp.where(jnp.asarray(in_rot), inv[jnp.asarray(lane % period % half)], 0.0)
    return pat.reshape(1, LANES).astype(jnp.float32)


def _pack_w_in(w):
    d = w.shape[0]
    z = lambda n: jnp.zeros((d, n), w.dtype)
    segs = [w[:, 0:1536 + KV_LATENT]]
    for h in range(IDX_HEADS):
        segs += [w[:, 1792 + h * IDX_DIM:1792 + (h + 1) * IDX_DIM], z(LANES - IDX_DIM)]
    segs += [w[:, 2048:2116], z(LANES - 68)]
    segs += [w[:, 2116:]]
    out = jnp.concatenate(segs, axis=1)
    assert out.shape[1] == _W_COLS
    return out.astype(_MXU_DTYPE)


def kernel(x, positions, norm1_g, w_in, v_norm_g, v_norm_b, spatial_w, spatial_b, kv_norm_g, w_uk, w_uv,
           q_norm_g, k_norm_g, w_a_out, w_b_out, w_o, norm2_g, peer_wq, peer_subkeys, peer_u, peer_v):
    bsz, s, d = x.shape
    t = bsz * s
    depth = w_in.shape[0]
    mx = _MXU_DTYPE
    n_sel = min(TOPK_MAX, s // 4)
    tm = min(256, t)
    tk = min(256, s // 4)
    invq = _rope_inv(ROT_DIM, LANES)
    invi = _rope_inv(IDX_ROT, LANES)
    pos = positions.reshape(t, 1).astype(jnp.float32)
    r2 = lambda a: a.reshape(1, -1)
    h = x.reshape(t, d)
    for l in range(depth):
        ma, gb, q, k, v, qi, ki, wi = _inproj_call(
            h, pos, r2(norm1_g[l]), _pack_w_in(w_in[l]), r2(v_norm_g[l]), r2(v_norm_b[l]), spatial_w[l],
            spatial_b[l].T, r2(kv_norm_g[l]), w_uk[l].astype(mx), w_uv[l].astype(mx), r2(q_norm_g[l]),
            r2(k_norm_g[l]), w_a_out[l].astype(mx), invq, invi, tm)
        b3 = lambda a: a.reshape(bsz, s, a.shape[-1])
        kT = b3(k).transpose(0, 2, 1)
        kiT = b3(ki).transpose(0, 2, 1)
        yb = _dsa_call(b3(q), b3(qi), b3(wi), kT, kiT, b3(v), n_sel, tk).reshape(t, 512)
        sk = peer_subkeys[l].reshape(2 * PEER_HEADS, N_KEYS, PEER_HALF).astype(mx)
        h1, hn, sub = _merge_call(h, ma, gb, yb, w_b_out[l].astype(mx), w_o[l].astype(mx), r2(norm2_g[l]),
                                  peer_wq[l].astype(mx), sk, tm)
        hk = PEER_HEADS * PEER_TOPK
        ei, ej, eg = (a.reshape(hk, t).T for a in _peer_topk_call(sub))
        m4 = _peer_coef_call(ei, ej, eg, min(32, t))
        m3 = m4.reshape(N_KEYS // _PEER_IB, t * _PEER_IB, N_KEYS)
        h = _peer_dense_call(hn, h1, peer_u[l].T.astype(mx), peer_v[l].astype(mx), m3, min(512, t), _PEER_IB)
    return h.reshape(bsz, s, d)
```

```python
import functools
import math

import jax
import jax.numpy as jnp
import numpy as np
from jax import lax
from jax.experimental import pallas as pl
from jax.experimental.pallas import tpu as pltpu

EPS = 1e-6
ROPE_THETA = 500000.0
CHUNK = 128
A_GROUPS = 4
A_GROUP_DIM = 128
N_HEADS = 4
HEAD_DIM = 128
KV_LATENT = 256
ROT_DIM = HEAD_DIM // 4
IDX_HEADS = 4
IDX_DIM = 64
IDX_ROT = IDX_DIM // 4
TOPK_MAX = 256
Q_BLOCK = 128
PEER_HEADS = 8
PEER_HALF = 128
N_KEYS = 128
PEER_TOPK = 16
LANES = 128
_PEER_IB = 8

_MXU_DTYPE = jnp.bfloat16
_INT_MIN = -2147483648
_KEY_LOWEST_FINITE = -2139095040
_LIST_DEPTH = 12
_MIN_DENOM = 2.0 ** -100
_KN_LANE = IDX_DIM + IDX_HEADS
_NEG_BIG = -1e30
_VMEM_LIMIT = 56 * 1024 * 1024
_V_EXT = 2 * HEAD_DIM


def _dot(a, b):
    return jnp.dot(a.astype(_MXU_DTYPE), b.astype(_MXU_DTYPE), preferred_element_type=jnp.float32)


def _dot_nt(a, b):
    return lax.dot_general(a.astype(_MXU_DTYPE), b.astype(_MXU_DTYPE), (((1,), (1,)), ((), ())),
                           preferred_element_type=jnp.float32)


def _rms(x, g):
    return x * lax.rsqrt(jnp.mean(x * x, axis=-1, keepdims=True) + EPS) * g


def _rope(x, cos_t, sin_lo, sin_hi, half):
    n = x.shape[-1]
    x_up = pltpu.roll(x, n - half, 1)
    x_dn = pltpu.roll(x, half, 1)
    return x * cos_t + x_up * sin_lo + x_dn * sin_hi


def _rope_tables(pos, inv, half):
    ang = pos * inv
    c = jnp.cos(ang)
    s = jnp.sin(ang)
    lane = lax.broadcasted_iota(jnp.int32, ang.shape, 1)
    s_lo = jnp.where(lane < half, -s, 0.0)
    s_hi = jnp.where(lane >= half, s, 0.0)
    return c, s_lo, s_hi


_OFF_U, _OFF_V, _OFF_Q, _OFF_C, _OFF_QI, _OFF_KI, _OFF_G = 0, 512, 1024, 1536, 1792, 2304, 2432
_W_COLS = 2432 + 2048


def _inproj_kernel(x_ref, pos_ref, g1_ref, w_ref, vg_ref, vb_ref, ws_ref, bs_ref, kvg_ref, wuk_ref, wuv_ref,
                   qg_ref, kg_ref, wa_ref, invq_ref, invi_ref,
                   ma_ref, gb_ref, q_ref, k_ref, v_ref, qi_ref, ki_ref, wi_ref):
    tm = x_ref.shape[0]
    x = x_ref[...]
    xn = _rms(x, g1_ref[...]).astype(_MXU_DTYPE)

    def proj(off, width):
        return jnp.dot(xn, w_ref[:, off:off + width], preferred_element_type=jnp.float32)

    u = jax.nn.gelu(proj(_OFF_U, 512))
    v = jax.nn.gelu(proj(_OFF_V, 512))
    mu = jnp.mean(v, axis=-1, keepdims=True)
    vc = v - mu
    v = vc * lax.rsqrt(jnp.mean(vc * vc, axis=-1, keepdims=True) + EPS) * vg_ref[...] + vb_ref[...]
    v = v.astype(_MXU_DTYPE)
    row = lax.broadcasted_iota(jnp.int32, (CHUNK, CHUNK), 0)
    col = lax.broadcasted_iota(jnp.int32, (CHUNK, CHUNK), 1)
    z_chunks = []
    for c in range(tm // CHUNK):
        zg = []
        for g in range(A_GROUPS):
            wt = jnp.where(row >= col, ws_ref[g], 0.0).astype(_MXU_DTYPE)
            vcg = v[c * CHUNK:(c + 1) * CHUNK, g * A_GROUP_DIM:(g + 1) * A_GROUP_DIM]
            zg.append(jnp.dot(wt, vcg, preferred_element_type=jnp.float32) + bs_ref[:, g:g + 1])
        z_chunks.append(jnp.concatenate(zg, axis=1))
    z = jnp.concatenate(z_chunks, axis=0) if len(z_chunks) > 1 else z_chunks[0]
    ya = u * z
    gate_a = jax.nn.sigmoid(proj(_OFF_G, 1024))
    ma_ref[...] = gate_a * _dot(ya, wa_ref[...])
    gb_ref[...] = jax.nn.sigmoid(proj(_OFF_G + 1024, 1024))

    pos = pos_ref[...]
    cq, sq_lo, sq_hi = _rope_tables(pos, invq_ref[...], ROT_DIM // 2)
    ci, si_lo, si_hi = _rope_tables(pos, invi_ref[...], IDX_ROT // 2)

    q = proj(_OFF_Q, 512)
    qg = qg_ref[...]
    qh = [_rms(q[:, h * HEAD_DIM:(h + 1) * HEAD_DIM], qg) for h in range(N_HEADS)]
    qh = [_rope(t, cq, sq_lo, sq_hi, ROT_DIM // 2) for t in qh]
    q_ref[...] = (jnp.concatenate(qh, axis=1) * (HEAD_DIM ** -0.5 * math.log2(math.e))).astype(q_ref.dtype)

    c_n = _rms(proj(_OFF_C, KV_LATENT), kvg_ref[...]).astype(_MXU_DTYPE)
    kk = _rms(jnp.dot(c_n, wuk_ref[...], preferred_element_type=jnp.float32), kg_ref[...])
    k_out = _rope(kk, cq, sq_lo, sq_hi, ROT_DIM // 2).astype(k_ref.dtype)
    k_ref[...] = k_out
    k32 = k_out.astype(jnp.float32)
    k_norm2 = jnp.sum(k32 * k32, axis=-1, keepdims=True)
    vv = jnp.dot(c_n, wuv_ref[...], preferred_element_type=jnp.float32)
    ones_col = jnp.where(lax.broadcasted_iota(jnp.int32, vv.shape, 1) == 0, 1.0, 0.0)
    v_ref[...] = jnp.concatenate([vv, ones_col], axis=1).astype(v_ref.dtype)

    qi = proj(_OFF_QI, 512)
    qis = [_rope(qi[:, h * LANES:(h + 1) * LANES], ci, si_lo, si_hi, IDX_ROT // 2) for h in range(IDX_HEADS)]
    qi_ref[...] = (jnp.concatenate(qis, axis=1) * (IDX_DIM ** -0.5)).astype(qi_ref.dtype)
    kw = proj(_OFF_KI, LANES)
    lane = lax.broadcasted_iota(jnp.int32, kw.shape, 1)
    ki = _rope(jnp.where(lane < IDX_DIM, kw, 0.0), ci, si_lo, si_hi, IDX_ROT // 2)
    ki_ref[...] = ki.astype(ki_ref.dtype)
    wi_ref[...] = jnp.where(lane == _KN_LANE, k_norm2, kw * (IDX_HEADS ** -0.5))


def _inproj_call(x2, pos, g1, w_pack, vg, vb, ws, bs_t, kvg, wuk, wuv, qg, kg, wa, invq, invi, tm):
    t, d = x2.shape
    full = lambda a: pl.BlockSpec(a.shape, lambda i: (0,) * a.ndim)
    row = lambda w: pl.BlockSpec((tm, w), lambda i: (i, 0))
    f32, mx = jnp.float32, _MXU_DTYPE
    outs = [(1024, f32), (1024, f32), (512, mx), (128, mx), (_V_EXT, mx), (512, mx), (128, mx), (128, f32)]
    return pl.pallas_call(
        _inproj_kernel,
        grid=(t // tm,),
        in_specs=[row(d), row(1), full(g1), full(w_pack), full(vg), full(vb), full(ws), full(bs_t), full(kvg),
                  full(wuk), full(wuv), full(qg), full(kg), full(wa), full(invq), full(invi)],
        out_specs=[row(w) for w, _ in outs],
        out_shape=[jax.ShapeDtypeStruct((t, w), dt) for w, dt in outs],
        compiler_params=pltpu.CompilerParams(dimension_semantics=("arbitrary",), vmem_limit_bytes=_VMEM_LIMIT),
        name="inproj",
    )(x2, pos, g1, w_pack, vg, vb, ws, bs_t, kvg, wuk, wuv, qg, kg, wa, invq, invi)


def _dsa_kernel(q_ref, qi_ref, wi_ref, kT_ref, kiT_ref, v_ref, kn_ref, o_ref,
                sc_ref, list_ref, listT_ref, thr_ref, need_ref, wb_ref, qa_ref, shift_ref, p_ref, m_ref, acc_ref,
                *, tk, n_sel):
    blk = pl.program_id(1)
    nq = Q_BLOCK
    ck = 2 * tk
    n_pairs = (blk * nq + nq + ck - 1) // ck

    wi = wi_ref[...]
    for h in range(IDX_HEADS):
        wb_ref[h] = jnp.broadcast_to(wi[:, IDX_DIM + h:IDX_DIM + h + 1], (nq, tk))

    def score_pair(j, causal_mask):
        for sub in range(2):
            off = pl.multiple_of(j * ck + sub * tk, tk)
            kt = kiT_ref[:, pl.ds(off, tk)]
            sc = None
            for h in range(IDX_HEADS):
                lg = jnp.dot(qi_ref[:, h * LANES:(h + 1) * LANES], kt, preferred_element_type=jnp.float32)
                term = wb_ref[h] * jnp.maximum(lg, 0.0)
                sc = term if sc is None else sc + term
            if causal_mask:
                qpos = blk * nq + lax.broadcasted_iota(jnp.int32, (nq, tk), 0)
                kpos = off + lax.broadcasted_iota(jnp.int32, (nq, tk), 1)
                sc = jnp.where(kpos <= qpos, sc, -jnp.inf)
            for c in range(tk // LANES):
                sc_ref[(j * ck + sub * tk) // LANES + c] = sc[:, c * LANES:(c + 1) * LANES]

    def score_body(j, carry):
        score_pair(j, False)
        return carry

    lax.fori_loop(0, n_pairs - 1, score_body, 0)
    score_pair(n_pairs - 1, True)

    @pl.when(n_pairs % 2 == 1)
    def _():
        for c in range(ck // LANES):
            sc_ref[n_pairs * (ck // LANES) + c] = jnp.full((nq, LANES), -jnp.inf, jnp.float32)

    tiles_per_pair = ck // LANES
    depth = _LIST_DEPTH

    def to_key(x):
        bits = pltpu.bitcast(x, jnp.int32)
        return bits ^ ((bits >> 31) & 0x7FFFFFFF)

    def to_score(key):
        return pltpu.bitcast(key ^ ((key >> 31) & 0x7FFFFFFF), jnp.float32)

    def kth_largest(count_ge, shape):
        zero = jnp.zeros(shape, jnp.int32)
        base = jnp.where(count_ge(zero) >= n_sel, zero, _INT_MIN)

        def bit_step(b, base):
            cand = base | jnp.left_shift(jnp.int32(1), 30 - b)
            return jnp.where(count_ge(cand) >= n_sel, cand, base)

        thr = lax.fori_loop(0, 31, bit_step, base)
        return jnp.maximum(thr, _KEY_LOWEST_FINITE)

    rows_per_step = 2 * 8
    for g in range(nq // rows_per_step):
        def insert(j, lists, g=g):
            lists = list(lists)
            for c in range(tiles_per_pair):
                for r in range(2):
                    x = sc_ref[j * tiles_per_pair + c, pl.ds(g * rows_per_step + r * 8, 8), :]
                    for i in range(depth):
                        cur = lists[r * depth + i]
                        lists[r * depth + i] = jnp.maximum(cur, x)
                        x = jnp.minimum(cur, x)
            return tuple(lists)

        lists = lax.fori_loop(0, n_pairs, insert, (jnp.full((8, LANES), -jnp.inf, jnp.float32),) * (2 * depth))
        for r in range(2):
            for i in range(depth):
                list_ref[i, pl.ds(g * rows_per_step + r * 8, 8), :] = lists[r * depth + i]

    for i in range(depth):
        listT_ref[i] = to_key(list_ref[i].T)

    def count_lists(cand_row):
        tot = None
        for i in range(depth):
            hit = (listT_ref[i] >= cand_row).astype(jnp.int32)
            tot = hit if tot is None else tot + hit
        return jnp.sum(tot, axis=0, keepdims=True)

    thr_row = kth_largest(count_lists, (1, nq))
    need_row = n_sel - count_lists(thr_row + 1)
    thr_ref[...] = to_score(jnp.broadcast_to(thr_row, (nq, nq))).T[:, 0:1]
    need_ref[...] = jnp.broadcast_to(need_row, (nq, nq)).astype(jnp.float32).T[:, 0:1]
    overflow = jnp.max(jnp.where(listT_ref[depth - 1] > thr_row, 1, 0))

    @pl.when(overflow > 0)
    def _():
        def count_all(cand):
            cb = jnp.broadcast_to(cand, (nq, LANES))

            def body(j, cnt):
                for c in range(tiles_per_pair):
                    cnt = cnt + (to_key(sc_ref[j * tiles_per_pair + c]) >= cb).astype(jnp.int32)
                return cnt

            cnt = lax.fori_loop(0, n_pairs, body, jnp.zeros((nq, LANES), jnp.int32))
            return jnp.sum(cnt, axis=1, keepdims=True)

        thr_col = kth_largest(count_all, (nq, 1))
        thr_ref[...] = to_score(thr_col)
        need_ref[...] = (n_sel - count_all(thr_col + 1)).astype(jnp.float32)

    thr = jnp.broadcast_to(thr_ref[...], (nq, tk))
    need = need_ref[...]

    tri = (lax.broadcasted_iota(jnp.int32, (tk, tk), 0) <= lax.broadcasted_iota(jnp.int32, (tk, tk), 1))
    tri = jnp.where(tri, 1.0, 0.0).astype(_MXU_DTYPE)

    def selected(j, sub, room):
        t0 = (j * ck + sub * tk) // LANES
        sc = jnp.concatenate([sc_ref[t0 + c] for c in range(tk // LANES)], axis=1)
        eq = sc == thr
        pre = jnp.dot(jnp.where(eq, 1.0, 0.0).astype(_MXU_DTYPE), tri, preferred_element_type=jnp.float32)
        return (sc > thr) | (eq & (pre <= room)), room - pre[:, tk - 1:tk]

    q32 = q_ref[...].astype(jnp.float32)
    qn2 = None
    for h in range(N_HEADS):
        qh = q32[:, h * HEAD_DIM:(h + 1) * HEAD_DIM]
        n2 = jnp.sum(qh * qh, axis=1, keepdims=True)
        qn2 = n2 if qn2 is None else jnp.maximum(qn2, n2)
    kn2 = jnp.max(kn_ref[...], axis=1, keepdims=True)
    neg_bound = jnp.broadcast_to(-jnp.sqrt(qn2 * kn2), (nq, tk))
    eye = (lax.broadcasted_iota(jnp.int32, (nq, nq), 0) == lax.broadcasted_iota(jnp.int32, (nq, nq), 1))
    eye = jnp.where(eye, 1.0, 0.0).astype(_MXU_DTYPE)
    for h in range(N_HEADS):
        qa_ref[h] = jnp.concatenate([q_ref[:, h * HEAD_DIM:(h + 1) * HEAD_DIM], eye], axis=1)
    acc_ref[...] = jnp.zeros(acc_ref.shape, jnp.float32)

    cq = 2 * ck
    n_quads = (n_pairs + 1) // 2
    last = n_quads - 1

    def stage_mask(qd, room):
        for sub in range(cq // tk):
            sel, room = selected(2 * qd, sub, room)
            shift_ref[qd % 2, :, sub * tk:(sub + 1) * tk] = jnp.where(sel, neg_bound, _NEG_BIG).astype(_MXU_DTYPE)
        return room

    def stage_weights(qd):
        rhs = jnp.concatenate([kT_ref[:, pl.ds(pl.multiple_of(qd * cq, cq), cq)], shift_ref[qd % 2]], axis=0)
        for h in range(N_HEADS):
            p_ref[qd % 2, h] = jnp.exp2(jnp.dot(qa_ref[h], rhs, preferred_element_type=jnp.float32)).astype(_MXU_DTYPE)

    def stage_values(qd):
        vt = v_ref[pl.ds(pl.multiple_of(qd * cq, cq), cq), :]
        for h in range(N_HEADS):
            acc_ref[h] += jnp.dot(p_ref[qd % 2, h], vt, preferred_element_type=jnp.float32)

    def shifted_step(j, room):
        stage_values(j - 1)
        stage_weights(j)
        return stage_mask(jnp.minimum(j + 1, last), room)

    room = stage_mask(0, need)
    stage_weights(0)
    room = stage_mask(jnp.minimum(1, last), room)
    lax.fori_loop(1, n_quads, shifted_step, room)
    stage_values(last)
    denom = acc_ref[0][:, HEAD_DIM:HEAD_DIM + 1]
    for h in range(1, N_HEADS):
        denom = jnp.minimum(denom, acc_ref[h][:, HEAD_DIM:HEAD_DIM + 1])
    l_min = jnp.min(denom)

    @pl.when(l_min < _MIN_DENOM)
    def _():
        m_ref[...] = jnp.full(m_ref.shape, _NEG_BIG, jnp.float32)
        acc_ref[...] = jnp.zeros(acc_ref.shape, jnp.float32)

        def online_pair(j, room):
            off = pl.multiple_of(j * ck, ck)
            biases = []
            for sub in range(2):
                sel, room = selected(j, sub, room)
                biases.append(jnp.where(sel, 0.0, _NEG_BIG))
            bias = jnp.concatenate(biases, axis=1)
            kt = kT_ref[:, pl.ds(off, ck)]
            vt = v_ref[pl.ds(off, ck), :]
            for h in range(N_HEADS):
                s = jnp.dot(q_ref[:, h * HEAD_DIM:(h + 1) * HEAD_DIM], kt, preferred_element_type=jnp.float32) + bias
                m_old = m_ref[h]
                m_new = jnp.maximum(m_old, jnp.max(s, axis=1, keepdims=True))
                alpha = jnp.exp2(m_old - m_new)
                p = jnp.exp2(s - jnp.concatenate([m_new] * (ck // LANES), axis=1))
                acc_ref[h] = (jnp.concatenate([alpha] * (_V_EXT // LANES), axis=1) * acc_ref[h]
                              + jnp.dot(p.astype(_MXU_DTYPE), vt, preferred_element_type=jnp.float32))
                m_ref[h] = m_new
            return room

        lax.fori_loop(0, n_pairs, online_pair, need)

    ys = []
    for h in range(N_HEADS):
        a = acc_ref[h]
        ys.append(a[:, :HEAD_DIM] / a[:, HEAD_DIM:HEAD_DIM + 1])
    o_ref[...] = jnp.concatenate(ys, axis=1).astype(o_ref.dtype)


def _dsa_call(q, qi, wi, kT, kiT, v, kn, n_sel, tk):
    b, s, _ = q.shape
    nblk = s // Q_BLOCK
    assert s % (4 * tk) == 0 and tk % Q_BLOCK == 0
    qspec = lambda w: pl.BlockSpec((None, Q_BLOCK, w), lambda bi, i: (bi, i, 0))
    kern = functools.partial(_dsa_kernel, tk=tk, n_sel=n_sel)
    return pl.pallas_call(
        kern,
        grid=(b, nblk),
        in_specs=[qspec(512), qspec(512), qspec(128),
                  pl.BlockSpec((None, HEAD_DIM, s), lambda bi, i: (bi, 0, 0)),
                  pl.BlockSpec((None, LANES, s), lambda bi, i: (bi, 0, 0)),
                  pl.BlockSpec((None, s, _V_EXT), lambda bi, i: (bi, 0, 0)),
                  pl.BlockSpec((None, 1, s), lambda bi, i: (bi, 0, 0))],
        out_specs=qspec(512),
        out_shape=jax.ShapeDtypeStruct((b, s, 512), _MXU_DTYPE),
        scratch_shapes=[pltpu.VMEM((s // LANES, Q_BLOCK, LANES), jnp.float32),
                        pltpu.VMEM((_LIST_DEPTH, Q_BLOCK, LANES), jnp.float32),
                        pltpu.VMEM((_LIST_DEPTH, LANES, Q_BLOCK), jnp.int32),
                        pltpu.VMEM((Q_BLOCK, 1), jnp.float32),
                        pltpu.VMEM((Q_BLOCK, 1), jnp.float32),
                        pltpu.VMEM((IDX_HEADS, Q_BLOCK, tk), jnp.float32),
                        pltpu.VMEM((N_HEADS, Q_BLOCK, HEAD_DIM + Q_BLOCK), _MXU_DTYPE),
                        pltpu.VMEM((2, Q_BLOCK, 4 * tk), _MXU_DTYPE),
                        pltpu.VMEM((2, N_HEADS, Q_BLOCK, 4 * tk), _MXU_DTYPE),
                        pltpu.VMEM((N_HEADS, Q_BLOCK, LANES), jnp.float32),
                        pltpu.VMEM((N_HEADS, Q_BLOCK, _V_EXT), jnp.float32)],
        compiler_params=pltpu.CompilerParams(dimension_semantics=("arbitrary", "arbitrary"),
                                             vmem_limit_bytes=_VMEM_LIMIT),
        name="dsa",
    )(q, qi, wi, kT, kiT, v, kn)


_TOK_CHUNK = 8 * LANES


def _merge_kernel(x_ref, ma_ref, gb_ref, yb_ref, wb_ref, wo_ref, g2_ref, wq_ref, sk_ref, h_ref, hn_ref, sub_ref):
    tm = x_ref.shape[0]
    merged = ma_ref[...] + gb_ref[...] * jnp.dot(yb_ref[...], wb_ref[...], preferred_element_type=jnp.float32)
    h1 = x_ref[...] + _dot(merged, wo_ref[...])
    h_ref[...] = h1
    hn = _rms(h1, g2_ref[...]).astype(_MXU_DTYPE)
    hn_ref[...] = hn
    qq = jnp.dot(hn, wq_ref[...], preferred_element_type=jnp.float32).astype(_MXU_DTYPE)
    groups = tm // LANES
    first = (pl.program_id(0) % (_TOK_CHUNK // tm)) * groups
    for hp in range(2 * PEER_HEADS):
        tile = _dot_nt(sk_ref[hp], qq[:, hp * PEER_HALF:(hp + 1) * PEER_HALF])
        for g in range(groups):
            sub_ref[hp, pl.ds(first + g, N_KEYS, stride=8), :] = tile[:, g * LANES:(g + 1) * LANES]


def _merge_call(x2, ma, gb, yb, wb, wo, g2, wq, sk, tm):
    t, d = x2.shape
    assert t % _TOK_CHUNK == 0 and _TOK_CHUNK % tm == 0
    full = lambda a: pl.BlockSpec(a.shape, lambda i: (0,) * a.ndim)
    row = lambda w: pl.BlockSpec((tm, w), lambda i: (i, 0))
    steps = _TOK_CHUNK // tm
    sub_shape = (2 * PEER_HEADS, t // _TOK_CHUNK, N_KEYS * 8, LANES)
    return pl.pallas_call(
        _merge_kernel,
        grid=(t // tm,),
        in_specs=[row(d), row(d), row(d), row(512), full(wb), full(wo), full(g2), full(wq), full(sk)],
        out_specs=[row(d), row(d),
                   pl.BlockSpec((2 * PEER_HEADS, None, N_KEYS * 8, LANES), lambda i: (0, i // steps, 0, 0))],
        out_shape=[jax.ShapeDtypeStruct((t, d), jnp.float32), jax.ShapeDtypeStruct((t, d), _MXU_DTYPE),
                   jax.ShapeDtypeStruct(sub_shape, jnp.float32)],
        compiler_params=pltpu.CompilerParams(dimension_semantics=("arbitrary",), vmem_limit_bytes=_VMEM_LIMIT),
        name="merge",
    )(x2, ma, gb, yb, wb, wo, g2, wq, sk)


def _insert_sorted(vals, pays, x, xp):
    for lvl in range(len(vals)):
        c = x > vals[lvl]
        vals[lvl], x = jnp.where(c, x, vals[lvl]), jnp.where(c, vals[lvl], x)
        pays[lvl], xp = jnp.where(c, xp, pays[lvl]), jnp.where(c, pays[lvl], xp)


_PEER_PAIRS = [(a, b) for a in range(PEER_TOPK) for b in range(PEER_TOPK // (a + 1))]
_KEYS_PER_STEP = 4


def _peer_topk_kernel(sub_ref, i_ref, j_ref, g_ref):
    kk = PEER_TOPK
    neg = jnp.full((8, LANES), -jnp.inf, jnp.float32)
    zero = jnp.zeros((8, LANES), jnp.int32)

    def sorted_top(hp):
        def step(k4, carry):
            vals, idxs = list(carry[0]), list(carry[1])
            for u in range(_KEYS_PER_STEP):
                key = k4 * _KEYS_PER_STEP + u
                x = sub_ref[hp, pl.ds(pl.multiple_of(key * 8, 8), 8), :]
                _insert_sorted(vals, idxs, x, zero + key)
            return tuple(vals), tuple(idxs)

        return lax.fori_loop(0, N_KEYS // _KEYS_PER_STEP, step, ((neg,) * kk, (zero,) * kk))

    def head(h, carry):
        s1, i1 = sorted_top(2 * h)
        s2, i2 = sorted_top(2 * h + 1)
        best, expert = [neg] * kk, [zero] * kk
        for a, b in _PEER_PAIRS:
            _insert_sorted(best, expert, s1[a] + s2[b], i1[a] * N_KEYS + i2[b])
        e = [jnp.exp(v - best[0]) for v in best]
        denom = e[0]
        for v in e[1:]:
            denom = denom + v
        for k in range(kk):
            i_ref[h * kk + k] = expert[k] >> 7
            j_ref[h * kk + k] = expert[k] & (N_KEYS - 1)
            g_ref[h * kk + k] = e[k] / denom
        return carry

    lax.fori_loop(0, PEER_HEADS, head, 0)


def _peer_topk_call(sub):
    hp, chunks, rows, _ = sub.shape
    hk = PEER_HEADS * PEER_TOPK
    out = pl.BlockSpec((hk, 8, LANES), lambda i: (0, i, 0))
    osd = lambda dt: jax.ShapeDtypeStruct((hk, chunks * 8, LANES), dt)
    return pl.pallas_call(
        _peer_topk_kernel,
        grid=(chunks,),
        in_specs=[pl.BlockSpec((hp, None, rows, LANES), lambda i: (0, i, 0, 0))],
        out_specs=[out, out, out],
        out_shape=[osd(jnp.int32), osd(jnp.int32), osd(jnp.float32)],
        compiler_params=pltpu.CompilerParams(dimension_semantics=("arbitrary",), vmem_limit_bytes=_VMEM_LIMIT),
        name="peer_topk",
    )(sub)


def _peer_coef_kernel(i_ref, j_ref, g_ref, m_ref):
    tt = i_ref.shape[0]
    rid = lax.broadcasted_iota(jnp.int32, (N_KEYS, LANES), 0)

    def body(t, carry):
        irow = i_ref[pl.ds(t, 1), :]
        jrow = j_ref[pl.ds(t, 1), :]
        grow = g_ref[pl.ds(t, 1), :]
        rt = jnp.where(rid == irow, grow, 0.0)
        ct = jnp.where(rid == jrow, 1.0, 0.0)
        mt = _dot_nt(rt, ct)
        for e in range(N_KEYS // _PEER_IB):
            m_ref[e, t] = mt[e * _PEER_IB:(e + 1) * _PEER_IB, :]
        return carry

    lax.fori_loop(0, tt, body, 0, unroll=16)


def _peer_coef_call(it, jt, gt, tt):
    t, hk = it.shape
    row = pl.BlockSpec((tt, hk), lambda i: (i, 0))
    nb = N_KEYS // _PEER_IB
    return pl.pallas_call(
        _peer_coef_kernel,
        grid=(t // tt,),
        in_specs=[row, row, row],
        out_specs=pl.BlockSpec((nb, tt, _PEER_IB, N_KEYS), lambda i: (0, i, 0, 0)),
        out_shape=jax.ShapeDtypeStruct((nb, t, _PEER_IB, N_KEYS), jnp.float32),
        compiler_params=pltpu.CompilerParams(dimension_semantics=("arbitrary",), vmem_limit_bytes=_VMEM_LIMIT),
        name="peer_coef",
    )(it, jt, gt)


def _peer_dense_kernel(hn_ref, h_ref, ut_ref, v_ref, m_ref, o_ref, *, ib):
    e = pl.program_id(1)

    @pl.when(e == 0)
    def _():
        o_ref[...] = h_ref[...]

    tm = hn_ref.shape[0]
    a = jax.nn.gelu(jnp.dot(hn_ref[...], ut_ref[...], preferred_element_type=jnp.float32))
    c = jnp.concatenate([a[:, (b * ib + ii) * N_KEYS:(b * ib + ii + 1) * N_KEYS] * m_ref[b, pl.ds(ii, tm, stride=ib), :]
                         for b in range(m_ref.shape[0]) for ii in range(ib)], axis=1)
    o_ref[...] += jnp.dot(c.astype(_MXU_DTYPE), v_ref[...], preferred_element_type=jnp.float32)


def _peer_dense_call(hn, h1, ut, v, m3, tm, ib, nb):
    t, d = hn.shape
    te = nb * ib * N_KEYS
    kern = functools.partial(_peer_dense_kernel, ib=ib)
    return pl.pallas_call(
        kern,
        grid=(t // tm, N_KEYS // (nb * ib)),
        in_specs=[pl.BlockSpec((tm, d), lambda ti, e: (ti, 0)),
                  pl.BlockSpec((tm, d), lambda ti, e: (ti, 0)),
                  pl.BlockSpec((d, te), lambda ti, e: (0, e)),
                  pl.BlockSpec((te, d), lambda ti, e: (e, 0)),
                  pl.BlockSpec((nb, tm * ib, N_KEYS), lambda ti, e: (e, ti, 0))],
        out_specs=pl.BlockSpec((tm, d), lambda ti, e: (ti, 0)),
        out_shape=jax.ShapeDtypeStruct((t, d), jnp.float32),
        compiler_params=pltpu.CompilerParams(dimension_semantics=("arbitrary", "arbitrary"),
                                             vmem_limit_bytes=_VMEM_LIMIT),
        name="peer_dense",
    )(hn, h1, ut, v, m3)


def _rope_inv(rot_dim, period):
    half = rot_dim // 2
    inv = jnp.power(jnp.float32(ROPE_THETA), -jnp.arange(half, dtype=jnp.float32) * 2.0 / rot_dim)
    lane = np.arange(LANES)
    in_rot = (lane % period) < rot_dim
    pat = jnp.where(jnp.asarray(in_rot), inv[jnp.asarray(lane % period % half)], 0.0)
    return pat.reshape(1, LANES).astype(jnp.float32)


def _pack_w_in(w):
    d = w.shape[0]
    z = lambda n: jnp.zeros((d, n), w.dtype)
    segs = [w[:, 0:1536 + KV_LATENT]]
    for h in range(IDX_HEADS):
        segs += [w[:, 1792 + h * IDX_DIM:1792 + (h + 1) * IDX_DIM], z(LANES - IDX_DIM)]
    segs += [w[:, 2048:2116], z(LANES - 68)]
    segs += [w[:, 2116:]]
    out = jnp.concatenate(segs, axis=1)
    assert out.shape[1] == _W_COLS
    return out.astype(_MXU_DTYPE)


def kernel(x, positions, norm1_g, w_in, v_norm_g, v_norm_b, spatial_w, spatial_b, kv_norm_g, w_uk, w_uv,
           q_norm_g, k_norm_g, w_a_out, w_b_out, w_o, norm2_g, peer_wq, peer_subkeys, peer_u, peer_v):
    bsz, s, d = x.shape
    t = bsz * s
    depth = w_in.shape[0]
    mx = _MXU_DTYPE
    n_sel = min(TOPK_MAX, s // 4)
    tm = min(256, t)
    tk = min(256, s // 4)
    invq = _rope_inv(ROT_DIM, LANES)
    invi = _rope_inv(IDX_ROT, LANES)
    pos = positions.reshape(t, 1).astype(jnp.float32)
    r2 = lambda a: a.reshape(1, -1)
    h = x.reshape(t, d)
    for l in range(depth):
        ma, gb, q, k, v, qi, ki, wi = _inproj_call(
            h, pos, r2(norm1_g[l]), _pack_w_in(w_in[l]), r2(v_norm_g[l]), r2(v_norm_b[l]), spatial_w[l],
            spatial_b[l].T, r2(kv_norm_g[l]), w_uk[l].astype(mx), w_uv[l].astype(mx), r2(q_norm_g[l]),
            r2(k_norm_g[l]), w_a_out[l].astype(mx), invq, invi, tm)
        b3 = lambda a: a.reshape(bsz, s, a.shape[-1])
        kT = b3(k).transpose(0, 2, 1)
        kiT = b3(ki).transpose(0, 2, 1)
        kn = wi[:, _KN_LANE].reshape(bsz, 1, s)
        yb = _dsa_call(b3(q), b3(qi), b3(wi), kT, kiT, b3(v), kn, n_sel, tk).reshape(t, 512)
        sk = peer_subkeys[l].reshape(2 * PEER_HEADS, N_KEYS, PEER_HALF).astype(mx)
        h1, hn, sub = _merge_call(h, ma, gb, yb, w_b_out[l].astype(mx), w_o[l].astype(mx), r2(norm2_g[l]),
                                  peer_wq[l].astype(mx), sk, tm)
        hk = PEER_HEADS * PEER_TOPK
        ei, ej, eg = (a.reshape(hk, t).T for a in _peer_topk_call(sub))
        m4 = _peer_coef_call(ei, ej, eg, min(32, t))
        m3 = m4.reshape(N_KEYS // _PEER_IB, t * _PEER_IB, N_KEYS)
        h = _peer_dense_call(hn, h1, peer_u[l].T.astype(mx), peer_v[l].astype(mx), m3, min(512, t), _PEER_IB, 2)
    return h.reshape(bsz, s, d)
```

```python
import functools
import math

import jax
import jax.numpy as jnp
import numpy as np
from jax import lax
from jax.experimental import pallas as pl
from jax.experimental.pallas import tpu as pltpu

EPS = 1e-6
ROPE_THETA = 500000.0
CHUNK = 128
A_GROUPS = 4
A_GROUP_DIM = 128
N_HEADS = 4
HEAD_DIM = 128
KV_LATENT = 256
ROT_DIM = HEAD_DIM // 4
IDX_HEADS = 4
IDX_DIM = 64
IDX_ROT = IDX_DIM // 4
TOPK_MAX = 256
Q_BLOCK = 128
PEER_HEADS = 8
PEER_HALF = 128
N_KEYS = 128
PEER_TOPK = 16
LANES = 128
_PEER_IB = 8

_MXU_DTYPE = jnp.bfloat16
_INT_MIN = -2147483648
_KEY_LOWEST_FINITE = -2139095040
_LIST_DEPTH = 12
_MIN_DENOM = 2.0 ** -100
_KN_LANE = IDX_DIM + IDX_HEADS
_NEG_BIG = -1e30
_VMEM_LIMIT = 56 * 1024 * 1024
_V_EXT = 2 * HEAD_DIM


def _dot(a, b):
    return jnp.dot(a.astype(_MXU_DTYPE), b.astype(_MXU_DTYPE), preferred_element_type=jnp.float32)


def _dot_nt(a, b):
    return lax.dot_general(a.astype(_MXU_DTYPE), b.astype(_MXU_DTYPE), (((1,), (1,)), ((), ())),
                           preferred_element_type=jnp.float32)


def _rms(x, g):
    return x * lax.rsqrt(jnp.mean(x * x, axis=-1, keepdims=True) + EPS) * g


def _rope(x, cos_t, sin_lo, sin_hi, half):
    n = x.shape[-1]
    x_up = pltpu.roll(x, n - half, 1)
    x_dn = pltpu.roll(x, half, 1)
    return x * cos_t + x_up * sin_lo + x_dn * sin_hi


def _rope_tables(pos, inv, half):
    ang = pos * inv
    c = jnp.cos(ang)
    s = jnp.sin(ang)
    lane = lax.broadcasted_iota(jnp.int32, ang.shape, 1)
    s_lo = jnp.where(lane < half, -s, 0.0)
    s_hi = jnp.where(lane >= half, s, 0.0)
    return c, s_lo, s_hi


_OFF_U, _OFF_V, _OFF_Q, _OFF_C, _OFF_QI, _OFF_KI, _OFF_G = 0, 512, 1024, 1536, 1792, 2304, 2432
_W_COLS = 2432 + 2048


def _inproj_kernel(x_ref, pos_ref, g1_ref, w_ref, vg_ref, vb_ref, ws_ref, bs_ref, kvg_ref, wuk_ref, wuv_ref,
                   qg_ref, kg_ref, wa_ref, invq_ref, invi_ref,
                   ma_ref, gb_ref, q_ref, k_ref, v_ref, qi_ref, ki_ref, wi_ref):
    tm = x_ref.shape[0]
    x = x_ref[...]
    xn = _rms(x, g1_ref[...]).astype(_MXU_DTYPE)

    def proj(off, width):
        return jnp.dot(xn, w_ref[:, off:off + width], preferred_element_type=jnp.float32)

    u = jax.nn.gelu(proj(_OFF_U, 512))
    v = jax.nn.gelu(proj(_OFF_V, 512))
    mu = jnp.mean(v, axis=-1, keepdims=True)
    vc = v - mu
    v = vc * lax.rsqrt(jnp.mean(vc * vc, axis=-1, keepdims=True) + EPS) * vg_ref[...] + vb_ref[...]
    v = v.astype(_MXU_DTYPE)
    row = lax.broadcasted_iota(jnp.int32, (CHUNK, CHUNK), 0)
    col = lax.broadcasted_iota(jnp.int32, (CHUNK, CHUNK), 1)
    z_chunks = []
    for c in range(tm // CHUNK):
        zg = []
        for g in range(A_GROUPS):
            wt = jnp.where(row >= col, ws_ref[g], 0.0).astype(_MXU_DTYPE)
            vcg = v[c * CHUNK:(c + 1) * CHUNK, g * A_GROUP_DIM:(g + 1) * A_GROUP_DIM]
            zg.append(jnp.dot(wt, vcg, preferred_element_type=jnp.float32) + bs_ref[:, g:g + 1])
        z_chunks.append(jnp.concatenate(zg, axis=1))
    z = jnp.concatenate(z_chunks, axis=0) if len(z_chunks) > 1 else z_chunks[0]
    ya = u * z
    gate_a = jax.nn.sigmoid(proj(_OFF_G, 1024))
    ma_ref[...] = gate_a * _dot(ya, wa_ref[...])
    gb_ref[...] = jax.nn.sigmoid(proj(_OFF_G + 1024, 1024))

    pos = pos_ref[...]
    cq, sq_lo, sq_hi = _rope_tables(pos, invq_ref[...], ROT_DIM // 2)
    ci, si_lo, si_hi = _rope_tables(pos, invi_ref[...], IDX_ROT // 2)

    q = proj(_OFF_Q, 512)
    qg = qg_ref[...]
    qh = [_rms(q[:, h * HEAD_DIM:(h + 1) * HEAD_DIM], qg) for h in range(N_HEADS)]
    qh = [_rope(t, cq, sq_lo, sq_hi, ROT_DIM // 2) for t in qh]
    q_ref[...] = (jnp.concatenate(qh, axis=1) * (HEAD_DIM ** -0.5 * math.log2(math.e))).astype(q_ref.dtype)

    c_n = _rms(proj(_OFF_C, KV_LATENT), kvg_ref[...]).astype(_MXU_DTYPE)
    kk = _rms(jnp.dot(c_n, wuk_ref[...], preferred_element_type=jnp.float32), kg_ref[...])
    k_out = _rope(kk, cq, sq_lo, sq_hi, ROT_DIM // 2).astype(k_ref.dtype)
    k_ref[...] = k_out
    k32 = k_out.astype(jnp.float32)
    k_norm2 = jnp.sum(k32 * k32, axis=-1, keepdims=True)
    vv = jnp.dot(c_n, wuv_ref[...], preferred_element_type=jnp.float32)
    ones_col = jnp.where(lax.broadcasted_iota(jnp.int32, vv.shape, 1) == 0, 1.0, 0.0)
    v_ref[...] = jnp.concatenate([vv, ones_col], axis=1).astype(v_ref.dtype)

    qi = proj(_OFF_QI, 512)
    qis = [_rope(qi[:, h * LANES:(h + 1) * LANES], ci, si_lo, si_hi, IDX_ROT // 2) for h in range(IDX_HEADS)]
    qi_ref[...] = (jnp.concatenate(qis, axis=1) * (IDX_DIM ** -0.5)).astype(qi_ref.dtype)
    kw = proj(_OFF_KI, LANES)
    lane = lax.broadcasted_iota(jnp.int32, kw.shape, 1)
    ki = _rope(jnp.where(lane < IDX_DIM, kw, 0.0), ci, si_lo, si_hi, IDX_ROT // 2)
    ki_ref[...] = ki.astype(ki_ref.dtype)
    wi_ref[...] = jnp.where(lane == _KN_LANE, k_norm2, kw * (IDX_HEADS ** -0.5))


def _inproj_call(x2, pos, g1, w_pack, vg, vb, ws, bs_t, kvg, wuk, wuv, qg, kg, wa, invq, invi, tm):
    t, d = x2.shape
    full = lambda a: pl.BlockSpec(a.shape, lambda i: (0,) * a.ndim)
    row = lambda w: pl.BlockSpec((tm, w), lambda i: (i, 0))
    f32, mx = jnp.float32, _MXU_DTYPE
    outs = [(1024, f32), (1024, f32), (512, mx), (128, mx), (_V_EXT, mx), (512, mx), (128, mx), (128, f32)]
    return pl.pallas_call(
        _inproj_kernel,
        grid=(t // tm,),
        in_specs=[row(d), row(1), full(g1), full(w_pack), full(vg), full(vb), full(ws), full(bs_t), full(kvg),
                  full(wuk), full(wuv), full(qg), full(kg), full(wa), full(invq), full(invi)],
        out_specs=[row(w) for w, _ in outs],
        out_shape=[jax.ShapeDtypeStruct((t, w), dt) for w, dt in outs],
        compiler_params=pltpu.CompilerParams(dimension_semantics=("arbitrary",), vmem_limit_bytes=_VMEM_LIMIT),
        name="inproj",
    )(x2, pos, g1, w_pack, vg, vb, ws, bs_t, kvg, wuk, wuv, qg, kg, wa, invq, invi)


def _dsa_kernel(q_ref, qi_ref, wi_ref, kT_ref, kiT_ref, v_ref, kn_ref, o_ref,
                sc_ref, list_ref, listT_ref, thr_ref, need_ref, wb_ref, qa_ref, shift_ref, p_ref, m_ref, acc_ref,
                *, tk, n_sel):
    blk = pl.program_id(1)
    nq = Q_BLOCK
    ck = 2 * tk
    n_pairs = (blk * nq + nq + ck - 1) // ck

    wi = wi_ref[...]
    for h in range(IDX_HEADS):
        wb_ref[h] = jnp.broadcast_to(wi[:, IDX_DIM + h:IDX_DIM + h + 1], (nq, tk))

    cq = 2 * ck
    n_quads = (n_pairs + 1) // 2
    tiles_per_pair = ck // LANES
    tiles_per_quad = cq // LANES
    depth = _LIST_DEPTH
    qpos = blk * nq + lax.broadcasted_iota(jnp.int32, (nq, tk), 0)
    lane_pos = lax.broadcasted_iota(jnp.int32, (nq, tk), 1)

    def score_quad(qd):
        for sub in range(cq // tk):
            off = pl.multiple_of(qd * cq + sub * tk, tk)
            kt = kiT_ref[:, pl.ds(off, tk)]
            sc = None
            for h in range(IDX_HEADS):
                lg = jnp.dot(qi_ref[:, h * LANES:(h + 1) * LANES], kt, preferred_element_type=jnp.float32)
                term = wb_ref[h] * jnp.maximum(lg, 0.0)
                sc = term if sc is None else sc + term
            sc = jnp.where(off + lane_pos <= qpos, sc, -jnp.inf)
            for c in range(tk // LANES):
                sc_ref[qd * tiles_per_quad + sub * (tk // LANES) + c] = sc[:, c * LANES:(c + 1) * LANES]

    def insert_quad(qd):
        for g in range(nq // 8):
            rows = pl.ds(g * 8, 8)
            lists = [list_ref[i, rows, :] for i in range(depth)]
            for c in range(tiles_per_quad):
                x = sc_ref[qd * tiles_per_quad + c, rows, :]
                for i in range(depth):
                    lists[i], x = jnp.maximum(lists[i], x), jnp.minimum(lists[i], x)
            for i in range(depth):
                list_ref[i, rows, :] = lists[i]

    def score_step(qd, carry):
        score_quad(qd)
        insert_quad(qd - 1)
        return carry

    list_ref[...] = jnp.full(list_ref.shape, -jnp.inf, jnp.float32)
    score_quad(0)
    lax.fori_loop(1, n_quads, score_step, 0)
    insert_quad(n_quads - 1)


    def to_key(x):
        bits = pltpu.bitcast(x, jnp.int32)
        return bits ^ ((bits >> 31) & 0x7FFFFFFF)

    def to_score(key):
        return pltpu.bitcast(key ^ ((key >> 31) & 0x7FFFFFFF), jnp.float32)

    def kth_largest(count_ge, shape):
        zero = jnp.zeros(shape, jnp.int32)
        base = jnp.where(count_ge(zero) >= n_sel, zero, _INT_MIN)

        def bit_step(b, base):
            cand = base | jnp.left_shift(jnp.int32(1), 30 - b)
            return jnp.where(count_ge(cand) >= n_sel, cand, base)

        thr = lax.fori_loop(0, 31, bit_step, base)
        return jnp.maximum(thr, _KEY_LOWEST_FINITE)

    for i in range(depth):
        listT_ref[i] = to_key(list_ref[i].T)

    def count_lists(cand_row):
        tot = None
        for i in range(depth):
            hit = (listT_ref[i] >= cand_row).astype(jnp.int32)
            tot = hit if tot is None else tot + hit
        return jnp.sum(tot, axis=0, keepdims=True)

    thr_row = kth_largest(count_lists, (1, nq))
    need_row = n_sel - count_lists(thr_row + 1)
    thr_ref[...] = to_score(jnp.broadcast_to(thr_row, (nq, nq))).T[:, 0:1]
    need_ref[...] = jnp.broadcast_to(need_row, (nq, nq)).astype(jnp.float32).T[:, 0:1]
    overflow = jnp.max(jnp.where(listT_ref[depth - 1] > thr_row, 1, 0))

    @pl.when(overflow > 0)
    def _():
        def count_all(cand):
            cb = jnp.broadcast_to(cand, (nq, LANES))

            def body(j, cnt):
                for c in range(tiles_per_pair):
                    cnt = cnt + (to_key(sc_ref[j * tiles_per_pair + c]) >= cb).astype(jnp.int32)
                return cnt

            cnt = lax.fori_loop(0, n_pairs, body, jnp.zeros((nq, LANES), jnp.int32))
            return jnp.sum(cnt, axis=1, keepdims=True)

        thr_col = kth_largest(count_all, (nq, 1))
        thr_ref[...] = to_score(thr_col)
        need_ref[...] = (n_sel - count_all(thr_col + 1)).astype(jnp.float32)

    thr = jnp.broadcast_to(thr_ref[...], (nq, tk))
    need = need_ref[...]

    tri = (lax.broadcasted_iota(jnp.int32, (tk, tk), 0) <= lax.broadcasted_iota(jnp.int32, (tk, tk), 1))
    tri = jnp.where(tri, 1.0, 0.0).astype(_MXU_DTYPE)

    def selected(j, sub, room):
        t0 = (j * ck + sub * tk) // LANES
        sc = jnp.concatenate([sc_ref[t0 + c] for c in range(tk // LANES)], axis=1)
        eq = sc == thr
        pre = jnp.dot(jnp.where(eq, 1.0, 0.0).astype(_MXU_DTYPE), tri, preferred_element_type=jnp.float32)
        return (sc > thr) | (eq & (pre <= room)), room - pre[:, tk - 1:tk]

    q32 = q_ref[...].astype(jnp.float32)
    qn2 = None
    for h in range(N_HEADS):
        qh = q32[:, h * HEAD_DIM:(h + 1) * HEAD_DIM]
        n2 = jnp.sum(qh * qh, axis=1, keepdims=True)
        qn2 = n2 if qn2 is None else jnp.maximum(qn2, n2)
    kn2 = jnp.max(kn_ref[...], axis=1, keepdims=True)
    neg_bound = jnp.broadcast_to(-jnp.sqrt(qn2 * kn2), (nq, tk))
    eye = (lax.broadcasted_iota(jnp.int32, (nq, nq), 0) == lax.broadcasted_iota(jnp.int32, (nq, nq), 1))
    eye = jnp.where(eye, 1.0, 0.0).astype(_MXU_DTYPE)
    for h in range(N_HEADS):
        qa_ref[h] = jnp.concatenate([q_ref[:, h * HEAD_DIM:(h + 1) * HEAD_DIM], eye], axis=1)
    acc_ref[...] = jnp.zeros(acc_ref.shape, jnp.float32)

    last = n_quads - 1

    def stage_mask(qd, room):
        for sub in range(cq // tk):
            sel, room = selected(2 * qd, sub, room)
            shift_ref[qd % 2, :, sub * tk:(sub + 1) * tk] = jnp.where(sel, neg_bound, _NEG_BIG).astype(_MXU_DTYPE)
        return room

    def stage_weights(qd):
        rhs = jnp.concatenate([kT_ref[:, pl.ds(pl.multiple_of(qd * cq, cq), cq)], shift_ref[qd % 2]], axis=0)
        for h in range(N_HEADS):
            p_ref[qd % 2, h] = jnp.exp2(jnp.dot(qa_ref[h], rhs, preferred_element_type=jnp.float32)).astype(_MXU_DTYPE)

    def stage_values(qd):
        vt = v_ref[pl.ds(pl.multiple_of(qd * cq, cq), cq), :]
        for h in range(N_HEADS):
            acc_ref[h] += jnp.dot(p_ref[qd % 2, h], vt, preferred_element_type=jnp.float32)

    def shifted_step(j, room):
        stage_values(j - 1)
        stage_weights(j)
        return stage_mask(jnp.minimum(j + 1, last), room)

    room = stage_mask(0, need)
    stage_weights(0)
    room = stage_mask(jnp.minimum(1, last), room)
    lax.fori_loop(1, n_quads, shifted_step, room)
    stage_values(last)
    denom = acc_ref[0][:, HEAD_DIM:HEAD_DIM + 1]
    for h in range(1, N_HEADS):
        denom = jnp.minimum(denom, acc_ref[h][:, HEAD_DIM:HEAD_DIM + 1])
    l_min = jnp.min(denom)

    @pl.when(l_min < _MIN_DENOM)
    def _():
        m_ref[...] = jnp.full(m_ref.shape, _NEG_BIG, jnp.float32)
        acc_ref[...] = jnp.zeros(acc_ref.shape, jnp.float32)

        def online_pair(j, room):
            off = pl.multiple_of(j * ck, ck)
            biases = []
            for sub in range(2):
                sel, room = selected(j, sub, room)
                biases.append(jnp.where(sel, 0.0, _NEG_BIG))
            bias = jnp.concatenate(biases, axis=1)
            kt = kT_ref[:, pl.ds(off, ck)]
            vt = v_ref[pl.ds(off, ck), :]
            for h in range(N_HEADS):
                s = jnp.dot(q_ref[:, h * HEAD_DIM:(h + 1) * HEAD_DIM], kt, preferred_element_type=jnp.float32) + bias
                m_old = m_ref[h]
                m_new = jnp.maximum(m_old, jnp.max(s, axis=1, keepdims=True))
                alpha = jnp.exp2(m_old - m_new)
                p = jnp.exp2(s - jnp.concatenate([m_new] * (ck // LANES), axis=1))
                acc_ref[h] = (jnp.concatenate([alpha] * (_V_EXT // LANES), axis=1) * acc_ref[h]
                              + jnp.dot(p.astype(_MXU_DTYPE), vt, preferred_element_type=jnp.float32))
                m_ref[h] = m_new
            return room

        lax.fori_loop(0, n_pairs, online_pair, need)

    ys = []
    for h in range(N_HEADS):
        a = acc_ref[h]
        ys.append(a[:, :HEAD_DIM] / a[:, HEAD_DIM:HEAD_DIM + 1])
    o_ref[...] = jnp.concatenate(ys, axis=1).astype(o_ref.dtype)


def _dsa_call(q, qi, wi, kT, kiT, v, kn, n_sel, tk):
    b, s, _ = q.shape
    nblk = s // Q_BLOCK
    assert s % (4 * tk) == 0 and tk % Q_BLOCK == 0
    qspec = lambda w: pl.BlockSpec((None, Q_BLOCK, w), lambda bi, i: (bi, i, 0))
    kern = functools.partial(_dsa_kernel, tk=tk, n_sel=n_sel)
    return pl.pallas_call(
        kern,
        grid=(b, nblk),
        in_specs=[qspec(512), qspec(512), qspec(128),
                  pl.BlockSpec((None, HEAD_DIM, s), lambda bi, i: (bi, 0, 0)),
                  pl.BlockSpec((None, LANES, s), lambda bi, i: (bi, 0, 0)),
                  pl.BlockSpec((None, s, _V_EXT), lambda bi, i: (bi, 0, 0)),
                  pl.BlockSpec((None, 1, s), lambda bi, i: (bi, 0, 0))],
        out_specs=qspec(512),
        out_shape=jax.ShapeDtypeStruct((b, s, 512), _MXU_DTYPE),
        scratch_shapes=[pltpu.VMEM((s // LANES, Q_BLOCK, LANES), jnp.float32),
                        pltpu.VMEM((_LIST_DEPTH, Q_BLOCK, LANES), jnp.float32),
                        pltpu.VMEM((_LIST_DEPTH, LANES, Q_BLOCK), jnp.int32),
                        pltpu.VMEM((Q_BLOCK, 1), jnp.float32),
                        pltpu.VMEM((Q_BLOCK, 1), jnp.float32),
                        pltpu.VMEM((IDX_HEADS, Q_BLOCK, tk), jnp.float32),
                        pltpu.VMEM((N_HEADS, Q_BLOCK, HEAD_DIM + Q_BLOCK), _MXU_DTYPE),
                        pltpu.VMEM((2, Q_BLOCK, 4 * tk), _MXU_DTYPE),
                        pltpu.VMEM((2, N_HEADS, Q_BLOCK, 4 * tk), _MXU_DTYPE),
                        pltpu.VMEM((N_HEADS, Q_BLOCK, LANES), jnp.float32),
                        pltpu.VMEM((N_HEADS, Q_BLOCK, _V_EXT), jnp.float32)],
        compiler_params=pltpu.CompilerParams(dimension_semantics=("arbitrary", "arbitrary"),
                                             vmem_limit_bytes=_VMEM_LIMIT),
        name="dsa",
    )(q, qi, wi, kT, kiT, v, kn)


_TOK_CHUNK = 8 * LANES


def _merge_kernel(x_ref, ma_ref, gb_ref, yb_ref, wb_ref, wo_ref, g2_ref, wq_ref, sk_ref, h_ref, hn_ref, sub_ref):
    tm = x_ref.shape[0]
    merged = ma_ref[...] + gb_ref[...] * jnp.dot(yb_ref[...], wb_ref[...], preferred_element_type=jnp.float32)
    h1 = x_ref[...] + _dot(merged, wo_ref[...])
    h_ref[...] = h1
    hn = _rms(h1, g2_ref[...]).astype(_MXU_DTYPE)
    hn_ref[...] = hn
    qq = jnp.dot(hn, wq_ref[...], preferred_element_type=jnp.float32).astype(_MXU_DTYPE)
    groups = tm // LANES
    first = (pl.program_id(0) % (_TOK_CHUNK // tm)) * groups
    for hp in range(2 * PEER_HEADS):
        tile = _dot_nt(sk_ref[hp], qq[:, hp * PEER_HALF:(hp + 1) * PEER_HALF])
        for g in range(groups):
            sub_ref[hp, pl.ds(first + g, N_KEYS, stride=8), :] = tile[:, g * LANES:(g + 1) * LANES]


def _merge_call(x2, ma, gb, yb, wb, wo, g2, wq, sk, tm):
    t, d = x2.shape
    assert t % _TOK_CHUNK == 0 and _TOK_CHUNK % tm == 0
    full = lambda a: pl.BlockSpec(a.shape, lambda i: (0,) * a.ndim)
    row = lambda w: pl.BlockSpec((tm, w), lambda i: (i, 0))
    steps = _TOK_CHUNK // tm
    sub_shape = (2 * PEER_HEADS, t // _TOK_CHUNK, N_KEYS * 8, LANES)
    return pl.pallas_call(
        _merge_kernel,
        grid=(t // tm,),
        in_specs=[row(d), row(d), row(d), row(512), full(wb), full(wo), full(g2), full(wq), full(sk)],
        out_specs=[row(d), row(d),
                   pl.BlockSpec((2 * PEER_HEADS, None, N_KEYS * 8, LANES), lambda i: (0, i // steps, 0, 0))],
        out_shape=[jax.ShapeDtypeStruct((t, d), jnp.float32), jax.ShapeDtypeStruct((t, d), _MXU_DTYPE),
                   jax.ShapeDtypeStruct(sub_shape, jnp.float32)],
        compiler_params=pltpu.CompilerParams(dimension_semantics=("arbitrary",), vmem_limit_bytes=_VMEM_LIMIT),
        name="merge",
    )(x2, ma, gb, yb, wb, wo, g2, wq, sk)


def _insert_sorted(vals, pays, x, xp):
    for lvl in range(len(vals)):
        c = x > vals[lvl]
        vals[lvl], x = jnp.where(c, x, vals[lvl]), jnp.where(c, vals[lvl], x)
        pays[lvl], xp = jnp.where(c, xp, pays[lvl]), jnp.where(c, pays[lvl], xp)


_PEER_PAIRS = [(a, b) for a in range(PEER_TOPK) for b in range(PEER_TOPK // (a + 1))]
_KEYS_PER_STEP = 4


def _peer_topk_kernel(sub_ref, i_ref, j_ref, g_ref):
    kk = PEER_TOPK
    neg = jnp.full((8, LANES), -jnp.inf, jnp.float32)
    zero = jnp.zeros((8, LANES), jnp.int32)

    def sorted_top(hp):
        def step(k4, carry):
            vals, idxs = list(carry[0]), list(carry[1])
            for u in range(_KEYS_PER_STEP):
                key = k4 * _KEYS_PER_STEP + u
                x = sub_ref[hp, pl.ds(pl.multiple_of(key * 8, 8), 8), :]
                _insert_sorted(vals, idxs, x, zero + key)
            return tuple(vals), tuple(idxs)

        return lax.fori_loop(0, N_KEYS // _KEYS_PER_STEP, step, ((neg,) * kk, (zero,) * kk))

    def head(h, carry):
        s1, i1 = sorted_top(2 * h)
        s2, i2 = sorted_top(2 * h + 1)
        best, expert = [neg] * kk, [zero] * kk
        for a, b in _PEER_PAIRS:
            _insert_sorted(best, expert, s1[a] + s2[b], i1[a] * N_KEYS + i2[b])
        e = [jnp.exp(v - best[0]) for v in best]
        denom = e[0]
        for v in e[1:]:
            denom = denom + v
        for k in range(kk):
            i_ref[h * kk + k] = expert[k] >> 7
            j_ref[h * kk + k] = expert[k] & (N_KEYS - 1)
            g_ref[h * kk + k] = e[k] / denom
        return carry

    lax.fori_loop(0, PEER_HEADS, head, 0)


def _peer_topk_call(sub):
    hp, chunks, rows, _ = sub.shape
    hk = PEER_HEADS * PEER_TOPK
    out = pl.BlockSpec((hk, 8, LANES), lambda i: (0, i, 0))
    osd = lambda dt: jax.ShapeDtypeStruct((hk, chunks * 8, LANES), dt)
    return pl.pallas_call(
        _peer_topk_kernel,
        grid=(chunks,),
        in_specs=[pl.BlockSpec((hp, None, rows, LANES), lambda i: (0, i, 0, 0))],
        out_specs=[out, out, out],
        out_shape=[osd(jnp.int32), osd(jnp.int32), osd(jnp.float32)],
        compiler_params=pltpu.CompilerParams(dimension_semantics=("arbitrary",), vmem_limit_bytes=_VMEM_LIMIT),
        name="peer_topk",
    )(sub)


def _peer_coef_kernel(i_ref, j_ref, g_ref, m_ref):
    tt = i_ref.shape[0]
    rid = lax.broadcasted_iota(jnp.int32, (N_KEYS, LANES), 0)

    def body(t, carry):
        irow = i_ref[pl.ds(t, 1), :]
        jrow = j_ref[pl.ds(t, 1), :]
        grow = g_ref[pl.ds(t, 1), :]
        rt = jnp.where(rid == irow, grow, 0.0)
        ct = jnp.where(rid == jrow, 1.0, 0.0)
        mt = _dot_nt(rt, ct)
        for e in range(N_KEYS // _PEER_IB):
            m_ref[e, t] = mt[e * _PEER_IB:(e + 1) * _PEER_IB, :]
        return carry

    lax.fori_loop(0, tt, body, 0, unroll=16)


def _peer_coef_call(it, jt, gt, tt):
    t, hk = it.shape
    row = pl.BlockSpec((tt, hk), lambda i: (i, 0))
    nb = N_KEYS // _PEER_IB
    return pl.pallas_call(
        _peer_coef_kernel,
        grid=(t // tt,),
        in_specs=[row, row, row],
        out_specs=pl.BlockSpec((nb, tt, _PEER_IB, N_KEYS), lambda i: (0, i, 0, 0)),
        out_shape=jax.ShapeDtypeStruct((nb, t, _PEER_IB, N_KEYS), jnp.float32),
        compiler_params=pltpu.CompilerParams(dimension_semantics=("arbitrary",), vmem_limit_bytes=_VMEM_LIMIT),
        name="peer_coef",
    )(it, jt, gt)


def _peer_dense_kernel(hn_ref, h_ref, ut_ref, v_ref, m_ref, o_ref, *, ib):
    e = pl.program_id(1)

    @pl.when(e == 0)
    def _():
        o_ref[...] = h_ref[...]

    tm = hn_ref.shape[0]
    a = jax.nn.gelu(jnp.dot(hn_ref[...], ut_ref[...], preferred_element_type=jnp.float32))
    c = jnp.concatenate([a[:, (b * ib + ii) * N_KEYS:(b * ib + ii + 1) * N_KEYS] * m_ref[b, pl.ds(ii, tm, stride=ib), :]
                         for b in range(m_ref.shape[0]) for ii in range(ib)], axis=1)
    o_ref[...] += jnp.dot(c.astype(_MXU_DTYPE), v_ref[...], preferred_element_type=jnp.float32)


def _peer_dense_call(hn, h1, ut, v, m3, tm, ib, nb):
    t, d = hn.shape
    te = nb * ib * N_KEYS
    kern = functools.partial(_peer_dense_kernel, ib=ib)
    return pl.pallas_call(
        kern,
        grid=(t // tm, N_KEYS // (nb * ib)),
        in_specs=[pl.BlockSpec((tm, d), lambda ti, e: (ti, 0)),
                  pl.BlockSpec((tm, d), lambda ti, e: (ti, 0)),
                  pl.BlockSpec((d, te), lambda ti, e: (0, e)),
                  pl.BlockSpec((te, d), lambda ti, e: (e, 0)),
                  pl.BlockSpec((nb, tm * ib, N_KEYS), lambda ti, e: (e, ti, 0))],
        out_specs=pl.BlockSpec((tm, d), lambda ti, e: (ti, 0)),
        out_shape=jax.ShapeDtypeStruct((t, d), jnp.float32),
        compiler_params=pltpu.CompilerParams(dimension_semantics=("arbitrary", "arbitrary"),
                                             vmem_limit_bytes=_VMEM_LIMIT),
        name="peer_dense",
    )(hn, h1, ut, v, m3)


def _rope_inv(rot_dim, period):
    half = rot_dim // 2
    inv = jnp.power(jnp.float32(ROPE_THETA), -jnp.arange(half, dtype=jnp.float32) * 2.0 / rot_dim)
    lane = np.arange(LANES)
    in_rot = (lane % period) < rot_dim
    pat = jnp.where(jnp.asarray(in_rot), inv[jnp.asarray(lane % period % half)], 0.0)
    return pat.reshape(1, LANES).astype(jnp.float32)


def _pack_w_in(w):
    d = w.shape[0]
    z = lambda n: jnp.zeros((d, n), w.dtype)
    segs = [w[:, 0:1536 + KV_LATENT]]
    for h in range(IDX_HEADS):
        segs += [w[:, 1792 + h * IDX_DIM:1792 + (h + 1) * IDX_DIM], z(LANES - IDX_DIM)]
    segs += [w[:, 2048:2116], z(LANES - 68)]
    segs += [w[:, 2116:]]
    out = jnp.concatenate(segs, axis=1)
    assert out.shape[1] == _W_COLS
    return out.astype(_MXU_DTYPE)


def kernel(x, positions, norm1_g, w_in, v_norm_g, v_norm_b, spatial_w, spatial_b, kv_norm_g, w_uk, w_uv,
           q_norm_g, k_norm_g, w_a_out, w_b_out, w_o, norm2_g, peer_wq, peer_subkeys, peer_u, peer_v):
    bsz, s, d = x.shape
    t = bsz * s
    depth = w_in.shape[0]
    mx = _MXU_DTYPE
    n_sel = min(TOPK_MAX, s // 4)
    tm = min(256, t)
    tk = min(256, s // 4)
    invq = _rope_inv(ROT_DIM, LANES)
    invi = _rope_inv(IDX_ROT, LANES)
    pos = positions.reshape(t, 1).astype(jnp.float32)
    r2 = lambda a: a.reshape(1, -1)
    h = x.reshape(t, d)
    for l in range(depth):
        ma, gb, q, k, v, qi, ki, wi = _inproj_call(
            h, pos, r2(norm1_g[l]), _pack_w_in(w_in[l]), r2(v_norm_g[l]), r2(v_norm_b[l]), spatial_w[l],
            spatial_b[l].T, r2(kv_norm_g[l]), w_uk[l].astype(mx), w_uv[l].astype(mx), r2(q_norm_g[l]),
            r2(k_norm_g[l]), w_a_out[l].astype(mx), invq, invi, tm)
        b3 = lambda a: a.reshape(bsz, s, a.shape[-1])
        kT = b3(k).transpose(0, 2, 1)
        kiT = b3(ki).transpose(0, 2, 1)
        kn = wi[:, _KN_LANE].reshape(bsz, 1, s)
        yb = _dsa_call(b3(q), b3(qi), b3(wi), kT, kiT, b3(v), kn, n_sel, tk).reshape(t, 512)
        sk = peer_subkeys[l].reshape(2 * PEER_HEADS, N_KEYS, PEER_HALF).astype(mx)
        h1, hn, sub = _merge_call(h, ma, gb, yb, w_b_out[l].astype(mx), w_o[l].astype(mx), r2(norm2_g[l]),
                                  peer_wq[l].astype(mx), sk, tm)
        hk = PEER_HEADS * PEER_TOPK
        ei, ej, eg = (a.reshape(hk, t).T for a in _peer_topk_call(sub))
        m4 = _peer_coef_call(ei, ej, eg, min(32, t))
        m3 = m4.reshape(N_KEYS // _PEER_IB, t * _PEER_IB, N_KEYS)
        h = _peer_dense_call(hn, h1, peer_u[l].T.astype(mx), peer_v[l].astype(mx), m3, min(512, t), _PEER_IB, 2)
    return h.reshape(bsz, s, d)
```

```python
import functools
import math

import jax
import jax.numpy as jnp
import numpy as np
from jax import lax
from jax.experimental import pallas as pl
from jax.experimental.pallas import tpu as pltpu

EPS = 1e-6
ROPE_THETA = 500000.0
CHUNK = 128
A_GROUPS = 4
A_GROUP_DIM = 128
N_HEADS = 4
HEAD_DIM = 128
KV_LATENT = 256
ROT_DIM = HEAD_DIM // 4
IDX_HEADS = 4
IDX_DIM = 64
IDX_ROT = IDX_DIM // 4
TOPK_MAX = 256
Q_BLOCK = 128
PEER_HEADS = 8
PEER_HALF = 128
N_KEYS = 128
PEER_TOPK = 16
LANES = 128
_PEER_IB = 8

_MXU_DTYPE = jnp.bfloat16
_INT_MIN = -2147483648
_KEY_LOWEST_FINITE = -2139095040
_LIST_DEPTH = 12
_MIN_DENOM = 2.0 ** -100
_KN_LANE = IDX_DIM + IDX_HEADS
_NEG_BIG = -1e30
_VMEM_LIMIT = 56 * 1024 * 1024
_V_EXT = 2 * HEAD_DIM


def _dot(a, b):
    return jnp.dot(a.astype(_MXU_DTYPE), b.astype(_MXU_DTYPE), preferred_element_type=jnp.float32)


def _dot_nt(a, b):
    return lax.dot_general(a.astype(_MXU_DTYPE), b.astype(_MXU_DTYPE), (((1,), (1,)), ((), ())),
                           preferred_element_type=jnp.float32)


def _rms(x, g):
    return x * lax.rsqrt(jnp.mean(x * x, axis=-1, keepdims=True) + EPS) * g


def _rope(x, cos_t, sin_lo, sin_hi, half):
    n = x.shape[-1]
    x_up = pltpu.roll(x, n - half, 1)
    x_dn = pltpu.roll(x, half, 1)
    return x * cos_t + x_up * sin_lo + x_dn * sin_hi


def _rope_tables(pos, inv):
    ang = pos * inv
    c = jnp.cos(ang)
    s = jnp.sin(ang)
    lane = lax.broadcasted_iota(jnp.int32, ang.shape, 1)

    def tables(c, s, rot):
        half = rot // 2
        return (jnp.where(lane < rot, c, 1.0), jnp.where(lane < half, -s, 0.0),
                jnp.where((lane >= half) & (lane < rot), s, 0.0))

    back = LANES - ROT_DIM
    return tables(c, s, ROT_DIM), tables(pltpu.roll(c, back, 1), pltpu.roll(s, back, 1), IDX_ROT)


_OFF_U, _OFF_V, _OFF_Q, _OFF_C, _OFF_QI, _OFF_KI, _OFF_G = 0, 512, 1024, 1536, 1792, 2304, 2432
_W_COLS = 2432 + 2048


def _inproj_kernel(x_ref, pos_ref, g1_ref, w_ref, vg_ref, vb_ref, ws_ref, bs_ref, kvg_ref, wuk_ref, wuv_ref,
                   qg_ref, kg_ref, wa_ref, inv_ref,
                   ma_ref, gb_ref, q_ref, k_ref, v_ref, qi_ref, ki_ref, wi_ref):
    tm = x_ref.shape[0]
    x = x_ref[...]
    xn = _rms(x, g1_ref[...]).astype(_MXU_DTYPE)

    def proj(off, width):
        return jnp.dot(xn, w_ref[:, off:off + width], preferred_element_type=jnp.float32)

    u = jax.nn.gelu(proj(_OFF_U, 512))
    v = jax.nn.gelu(proj(_OFF_V, 512))
    mu = jnp.mean(v, axis=-1, keepdims=True)
    vc = v - mu
    v = vc * lax.rsqrt(jnp.mean(vc * vc, axis=-1, keepdims=True) + EPS) * vg_ref[...] + vb_ref[...]
    v = v.astype(_MXU_DTYPE)
    row = lax.broadcasted_iota(jnp.int32, (CHUNK, CHUNK), 0)
    col = lax.broadcasted_iota(jnp.int32, (CHUNK, CHUNK), 1)
    z_chunks = []
    for c in range(tm // CHUNK):
        zg = []
        for g in range(A_GROUPS):
            wt = jnp.where(row >= col, ws_ref[g], 0.0).astype(_MXU_DTYPE)
            vcg = v[c * CHUNK:(c + 1) * CHUNK, g * A_GROUP_DIM:(g + 1) * A_GROUP_DIM]
            zg.append(jnp.dot(wt, vcg, preferred_element_type=jnp.float32) + bs_ref[:, g:g + 1])
        z_chunks.append(jnp.concatenate(zg, axis=1))
    z = jnp.concatenate(z_chunks, axis=0) if len(z_chunks) > 1 else z_chunks[0]
    ya = u * z
    gate_a = jax.nn.sigmoid(proj(_OFF_G, 1024))
    ma_ref[...] = gate_a * _dot(ya, wa_ref[...])
    gb_ref[...] = jax.nn.sigmoid(proj(_OFF_G + 1024, 1024))

    pos = pos_ref[...]
    (cq, sq_lo, sq_hi), (ci, si_lo, si_hi) = _rope_tables(pos, inv_ref[...])

    q = proj(_OFF_Q, 512)
    qg = qg_ref[...]
    qh = [_rms(q[:, h * HEAD_DIM:(h + 1) * HEAD_DIM], qg) for h in range(N_HEADS)]
    qh = [_rope(t, cq, sq_lo, sq_hi, ROT_DIM // 2) for t in qh]
    q_ref[...] = (jnp.concatenate(qh, axis=1) * (HEAD_DIM ** -0.5 * math.log2(math.e))).astype(q_ref.dtype)

    c_n = _rms(proj(_OFF_C, KV_LATENT), kvg_ref[...]).astype(_MXU_DTYPE)
    kk = _rms(jnp.dot(c_n, wuk_ref[...], preferred_element_type=jnp.float32), kg_ref[...])
    k_out = _rope(kk, cq, sq_lo, sq_hi, ROT_DIM // 2).astype(k_ref.dtype)
    k_ref[...] = k_out
    k32 = k_out.astype(jnp.float32)
    k_norm2 = jnp.sum(k32 * k32, axis=-1, keepdims=True)
    vv = jnp.dot(c_n, wuv_ref[...], preferred_element_type=jnp.float32)
    ones_col = jnp.where(lax.broadcasted_iota(jnp.int32, vv.shape, 1) == 0, 1.0, 0.0)
    v_ref[...] = jnp.concatenate([vv, ones_col], axis=1).astype(v_ref.dtype)

    qi = proj(_OFF_QI, 512)
    qis = [_rope(qi[:, h * LANES:(h + 1) * LANES], ci, si_lo, si_hi, IDX_ROT // 2) for h in range(IDX_HEADS)]
    qi_ref[...] = (jnp.concatenate(qis, axis=1) * (IDX_DIM ** -0.5)).astype(qi_ref.dtype)
    kw = proj(_OFF_KI, LANES)
    lane = lax.broadcasted_iota(jnp.int32, kw.shape, 1)
    ki = _rope(jnp.where(lane < IDX_DIM, kw, 0.0), ci, si_lo, si_hi, IDX_ROT // 2)
    ki_ref[...] = ki.astype(ki_ref.dtype)
    wi_ref[...] = jnp.where(lane == _KN_LANE, k_norm2, kw * (IDX_HEADS ** -0.5))


def _inproj_call(x2, pos, g1, w_pack, vg, vb, ws, bs_t, kvg, wuk, wuv, qg, kg, wa, inv, tm):
    t, d = x2.shape
    full = lambda a: pl.BlockSpec(a.shape, lambda i: (0,) * a.ndim)
    row = lambda w: pl.BlockSpec((tm, w), lambda i: (i, 0))
    f32, mx = jnp.float32, _MXU_DTYPE
    outs = [(1024, f32), (1024, f32), (512, mx), (128, mx), (_V_EXT, mx), (512, mx), (128, mx), (128, f32)]
    return pl.pallas_call(
        _inproj_kernel,
        grid=(t // tm,),
        in_specs=[row(d), row(1), full(g1), full(w_pack), full(vg), full(vb), full(ws), full(bs_t), full(kvg),
                  full(wuk), full(wuv), full(qg), full(kg), full(wa), full(inv)],
        out_specs=[row(w) for w, _ in outs],
        out_shape=[jax.ShapeDtypeStruct((t, w), dt) for w, dt in outs],
        compiler_params=pltpu.CompilerParams(dimension_semantics=("arbitrary",), vmem_limit_bytes=_VMEM_LIMIT),
        name="inproj",
    )(x2, pos, g1, w_pack, vg, vb, ws, bs_t, kvg, wuk, wuv, qg, kg, wa, inv)


def _dsa_kernel(q_ref, qi_ref, wi_ref, kT_ref, kiT_ref, v_ref, kn_ref, o_ref,
                sc_ref, list_ref, listT_ref, thr_ref, need_ref, wb_ref, qa_ref, shift_ref, p_ref, m_ref, acc_ref,
                *, tk, n_sel):
    blk = pl.program_id(1)
    nq = Q_BLOCK
    ck = 2 * tk
    n_pairs = (blk * nq + nq + ck - 1) // ck

    wi = wi_ref[...]
    for h in range(IDX_HEADS):
        wb_ref[h] = jnp.broadcast_to(wi[:, IDX_DIM + h:IDX_DIM + h + 1], (nq, tk))

    cq = 2 * ck
    n_quads = (n_pairs + 1) // 2
    tiles_per_pair = ck // LANES
    tiles_per_quad = cq // LANES
    depth = _LIST_DEPTH
    qpos = blk * nq + lax.broadcasted_iota(jnp.int32, (nq, tk), 0)
    lane_pos = lax.broadcasted_iota(jnp.int32, (nq, tk), 1)

    def score_quad(qd):
        for sub in range(cq // tk):
            off = pl.multiple_of(qd * cq + sub * tk, tk)
            kt = kiT_ref[:, pl.ds(off, tk)]
            sc = None
            for h in range(IDX_HEADS):
                lg = jnp.dot(qi_ref[:, h * LANES:(h + 1) * LANES], kt, preferred_element_type=jnp.float32)
                term = wb_ref[h] * jnp.maximum(lg, 0.0)
                sc = term if sc is None else sc + term
            sc = jnp.where(off + lane_pos <= qpos, sc, -jnp.inf)
            for c in range(tk // LANES):
                sc_ref[qd * tiles_per_quad + sub * (tk // LANES) + c] = sc[:, c * LANES:(c + 1) * LANES]

    def insert_quad(qd):
        for g in range(nq // 8):
            rows = pl.ds(g * 8, 8)
            lists = [list_ref[i, rows, :] for i in range(depth)]
            for c in range(tiles_per_quad):
                x = sc_ref[qd * tiles_per_quad + c, rows, :]
                for i in range(depth):
                    lists[i], x = jnp.maximum(lists[i], x), jnp.minimum(lists[i], x)
            for i in range(depth):
                list_ref[i, rows, :] = lists[i]

    def score_step(qd, carry):
        score_quad(qd)
        insert_quad(qd - 1)
        return carry

    list_ref[...] = jnp.full(list_ref.shape, -jnp.inf, jnp.float32)
    score_quad(0)
    lax.fori_loop(1, n_quads, score_step, 0)
    insert_quad(n_quads - 1)


    def to_key(x):
        bits = pltpu.bitcast(x, jnp.int32)
        return bits ^ ((bits >> 31) & 0x7FFFFFFF)

    def to_score(key):
        return pltpu.bitcast(key ^ ((key >> 31) & 0x7FFFFFFF), jnp.float32)

    def kth_largest(count_ge, shape):
        zero = jnp.zeros(shape, jnp.int32)
        base = jnp.where(count_ge(zero) >= n_sel, zero, _INT_MIN)

        def bit_step(b, base):
            cand = base | jnp.left_shift(jnp.int32(1), 30 - b)
            return jnp.where(count_ge(cand) >= n_sel, cand, base)

        thr = lax.fori_loop(0, 31, bit_step, base)
        return jnp.maximum(thr, _KEY_LOWEST_FINITE)

    for i in range(depth):
        listT_ref[i] = to_key(list_ref[i].T)

    def count_lists(cand_row):
        tot = None
        for i in range(depth):
            hit = (listT_ref[i] >= cand_row).astype(jnp.int32)
            tot = hit if tot is None else tot + hit
        return jnp.sum(tot, axis=0, keepdims=True)

    thr_row = kth_largest(count_lists, (1, nq))
    need_row = n_sel - count_lists(thr_row + 1)
    thr_ref[...] = to_score(jnp.broadcast_to(thr_row, (nq, nq))).T[:, 0:1]
    need_ref[...] = jnp.broadcast_to(need_row, (nq, nq)).astype(jnp.float32).T[:, 0:1]
    overflow = jnp.max(jnp.where(listT_ref[depth - 1] > thr_row, 1, 0))

    @pl.when(overflow > 0)
    def _():
        def count_all(cand):
            cb = jnp.broadcast_to(cand, (nq, LANES))

            def body(j, cnt):
                for c in range(tiles_per_pair):
                    cnt = cnt + (to_key(sc_ref[j * tiles_per_pair + c]) >= cb).astype(jnp.int32)
                return cnt

            cnt = lax.fori_loop(0, n_pairs, body, jnp.zeros((nq, LANES), jnp.int32))
            return jnp.sum(cnt, axis=1, keepdims=True)

        thr_col = kth_largest(count_all, (nq, 1))
        thr_ref[...] = to_score(thr_col)
        need_ref[...] = (n_sel - count_all(thr_col + 1)).astype(jnp.float32)

    thr = jnp.broadcast_to(thr_ref[...], (nq, tk))
    need = need_ref[...]

    tri = (lax.broadcasted_iota(jnp.int32, (tk, tk), 0) <= lax.broadcasted_iota(jnp.int32, (tk, tk), 1))
    tri = jnp.where(tri, 1.0, 0.0).astype(_MXU_DTYPE)

    def selected(j, sub, room):
        t0 = (j * ck + sub * tk) // LANES
        sc = jnp.concatenate([sc_ref[t0 + c] for c in range(tk // LANES)], axis=1)
        eq = sc == thr
        pre = jnp.dot(jnp.where(eq, 1.0, 0.0).astype(_MXU_DTYPE), tri, preferred_element_type=jnp.float32)
        return (sc > thr) | (eq & (pre <= room)), room - pre[:, tk - 1:tk]

    q32 = q_ref[...].astype(jnp.float32)
    qn2 = None
    for h in range(N_HEADS):
        qh = q32[:, h * HEAD_DIM:(h + 1) * HEAD_DIM]
        n2 = jnp.sum(qh * qh, axis=1, keepdims=True)
        qn2 = n2 if qn2 is None else jnp.maximum(qn2, n2)
    kn2 = jnp.max(kn_ref[...], axis=1, keepdims=True)
    neg_bound = jnp.broadcast_to(-jnp.sqrt(qn2 * kn2), (nq, tk))
    eye = (lax.broadcasted_iota(jnp.int32, (nq, nq), 0) == lax.broadcasted_iota(jnp.int32, (nq, nq), 1))
    eye = jnp.where(eye, 1.0, 0.0).astype(_MXU_DTYPE)
    for h in range(N_HEADS):
        qa_ref[h] = jnp.concatenate([q_ref[:, h * HEAD_DIM:(h + 1) * HEAD_DIM], eye], axis=1)
    acc_ref[...] = jnp.zeros(acc_ref.shape, jnp.float32)

    last = n_quads - 1

    def stage_mask(qd, room):
        for sub in range(cq // tk):
            sel, room = selected(2 * qd, sub, room)
            shift_ref[qd % 2, :, sub * tk:(sub + 1) * tk] = jnp.where(sel, neg_bound, _NEG_BIG).astype(_MXU_DTYPE)
        return room

    def stage_weights(qd):
        rhs = jnp.concatenate([kT_ref[:, pl.ds(pl.multiple_of(qd * cq, cq), cq)], shift_ref[qd % 2]], axis=0)
        for h in range(N_HEADS):
            p_ref[qd % 2, h] = jnp.exp2(jnp.dot(qa_ref[h], rhs, preferred_element_type=jnp.float32)).astype(_MXU_DTYPE)

    def stage_values(qd):
        vt = v_ref[pl.ds(pl.multiple_of(qd * cq, cq), cq), :]
        for h in range(N_HEADS):
            acc_ref[h] += jnp.dot(p_ref[qd % 2, h], vt, preferred_element_type=jnp.float32)

    def shifted_step(j, room):
        stage_values(j - 1)
        stage_weights(j)
        return stage_mask(jnp.minimum(j + 1, last), room)

    room = stage_mask(0, need)
    stage_weights(0)
    room = stage_mask(jnp.minimum(1, last), room)
    lax.fori_loop(1, n_quads, shifted_step, room)
    stage_values(last)
    denom = acc_ref[0][:, HEAD_DIM:HEAD_DIM + 1]
    for h in range(1, N_HEADS):
        denom = jnp.minimum(denom, acc_ref[h][:, HEAD_DIM:HEAD_DIM + 1])
    l_min = jnp.min(denom)

    @pl.when(l_min < _MIN_DENOM)
    def _():
        m_ref[...] = jnp.full(m_ref.shape, _NEG_BIG, jnp.float32)
        acc_ref[...] = jnp.zeros(acc_ref.shape, jnp.float32)

        def online_pair(j, room):
            off = pl.multiple_of(j * ck, ck)
            biases = []
            for sub in range(2):
                sel, room = selected(j, sub, room)
                biases.append(jnp.where(sel, 0.0, _NEG_BIG))
            bias = jnp.concatenate(biases, axis=1)
            kt = kT_ref[:, pl.ds(off, ck)]
            vt = v_ref[pl.ds(off, ck), :]
            for h in range(N_HEADS):
                s = jnp.dot(q_ref[:, h * HEAD_DIM:(h + 1) * HEAD_DIM], kt, preferred_element_type=jnp.float32) + bias
                m_old = m_ref[h]
                m_new = jnp.maximum(m_old, jnp.max(s, axis=1, keepdims=True))
                alpha = jnp.exp2(m_old - m_new)
                p = jnp.exp2(s - jnp.concatenate([m_new] * (ck // LANES), axis=1))
                acc_ref[h] = (jnp.concatenate([alpha] * (_V_EXT // LANES), axis=1) * acc_ref[h]
                              + jnp.dot(p.astype(_MXU_DTYPE), vt, preferred_element_type=jnp.float32))
                m_ref[h] = m_new
            return room

        lax.fori_loop(0, n_pairs, online_pair, need)

    ys = []
    for h in range(N_HEADS):
        a = acc_ref[h]
        ys.append(a[:, :HEAD_DIM] / a[:, HEAD_DIM:HEAD_DIM + 1])
    o_ref[...] = jnp.concatenate(ys, axis=1).astype(o_ref.dtype)


def _dsa_call(q, qi, wi, kT, kiT, v, kn, n_sel, tk):
    b, s, _ = q.shape
    nblk = s // Q_BLOCK
    assert s % (4 * tk) == 0 and tk % Q_BLOCK == 0
    qspec = lambda w: pl.BlockSpec((None, Q_BLOCK, w), lambda bi, i: (bi, i, 0))
    kern = functools.partial(_dsa_kernel, tk=tk, n_sel=n_sel)
    return pl.pallas_call(
        kern,
        grid=(b, nblk),
        in_specs=[qspec(512), qspec(512), qspec(128),
                  pl.BlockSpec((None, HEAD_DIM, s), lambda bi, i: (bi, 0, 0)),
                  pl.BlockSpec((None, LANES, s), lambda bi, i: (bi, 0, 0)),
                  pl.BlockSpec((None, s, _V_EXT), lambda bi, i: (bi, 0, 0)),
                  pl.BlockSpec((None, 1, s), lambda bi, i: (bi, 0, 0))],
        out_specs=qspec(512),
        out_shape=jax.ShapeDtypeStruct((b, s, 512), _MXU_DTYPE),
        scratch_shapes=[pltpu.VMEM((s // LANES, Q_BLOCK, LANES), jnp.float32),
                        pltpu.VMEM((_LIST_DEPTH, Q_BLOCK, LANES), jnp.float32),
                        pltpu.VMEM((_LIST_DEPTH, LANES, Q_BLOCK), jnp.int32),
                        pltpu.VMEM((Q_BLOCK, 1), jnp.float32),
                        pltpu.VMEM((Q_BLOCK, 1), jnp.float32),
                        pltpu.VMEM((IDX_HEADS, Q_BLOCK, tk), jnp.float32),
                        pltpu.VMEM((N_HEADS, Q_BLOCK, HEAD_DIM + Q_BLOCK), _MXU_DTYPE),
                        pltpu.VMEM((2, Q_BLOCK, 4 * tk), _MXU_DTYPE),
                        pltpu.VMEM((2, N_HEADS, Q_BLOCK, 4 * tk), _MXU_DTYPE),
                        pltpu.VMEM((N_HEADS, Q_BLOCK, LANES), jnp.float32),
                        pltpu.VMEM((N_HEADS, Q_BLOCK, _V_EXT), jnp.float32)],
        compiler_params=pltpu.CompilerParams(dimension_semantics=("arbitrary", "arbitrary"),
                                             vmem_limit_bytes=_VMEM_LIMIT),
        name="dsa",
    )(q, qi, wi, kT, kiT, v, kn)


_TOK_CHUNK = 8 * LANES


def _merge_kernel(x_ref, ma_ref, gb_ref, yb_ref, wb_ref, wo_ref, g2_ref, wq_ref, sk_ref, h_ref, hn_ref, sub_ref):
    tm = x_ref.shape[0]
    merged = ma_ref[...] + gb_ref[...] * jnp.dot(yb_ref[...], wb_ref[...], preferred_element_type=jnp.float32)
    h1 = x_ref[...] + _dot(merged, wo_ref[...])
    h_ref[...] = h1
    hn = _rms(h1, g2_ref[...]).astype(_MXU_DTYPE)
    hn_ref[...] = hn
    qq = jnp.dot(hn, wq_ref[...], preferred_element_type=jnp.float32).astype(_MXU_DTYPE)
    groups = tm // LANES
    first = (pl.program_id(0) % (_TOK_CHUNK // tm)) * groups
    for hp in range(2 * PEER_HEADS):
        tile = _dot_nt(sk_ref[hp], qq[:, hp * PEER_HALF:(hp + 1) * PEER_HALF])
        for g in range(groups):
            sub_ref[hp, pl.ds(first + g, N_KEYS, stride=8), :] = tile[:, g * LANES:(g + 1) * LANES]


def _merge_call(x2, ma, gb, yb, wb, wo, g2, wq, sk, tm):
    t, d = x2.shape
    assert t % _TOK_CHUNK == 0 and _TOK_CHUNK % tm == 0
    full = lambda a: pl.BlockSpec(a.shape, lambda i: (0,) * a.ndim)
    row = lambda w: pl.BlockSpec((tm, w), lambda i: (i, 0))
    steps = _TOK_CHUNK // tm
    sub_shape = (2 * PEER_HEADS, t // _TOK_CHUNK, N_KEYS * 8, LANES)
    return pl.pallas_call(
        _merge_kernel,
        grid=(t // tm,),
        in_specs=[row(d), row(d), row(d), row(512), full(wb), full(wo), full(g2), full(wq), full(sk)],
        out_specs=[row(d), row(d),
                   pl.BlockSpec((2 * PEER_HEADS, None, N_KEYS * 8, LANES), lambda i: (0, i // steps, 0, 0))],
        out_shape=[jax.ShapeDtypeStruct((t, d), jnp.float32), jax.ShapeDtypeStruct((t, d), _MXU_DTYPE),
                   jax.ShapeDtypeStruct(sub_shape, jnp.float32)],
        compiler_params=pltpu.CompilerParams(dimension_semantics=("arbitrary",), vmem_limit_bytes=_VMEM_LIMIT),
        name="merge",
    )(x2, ma, gb, yb, wb, wo, g2, wq, sk)


def _insert_sorted(vals, pays, x, xp):
    for lvl in range(len(vals)):
        c = x > vals[lvl]
        vals[lvl], x = jnp.where(c, x, vals[lvl]), jnp.where(c, vals[lvl], x)
        pays[lvl], xp = jnp.where(c, xp, pays[lvl]), jnp.where(c, pays[lvl], xp)


_PEER_PAIRS = [(a, b) for a in range(PEER_TOPK) for b in range(PEER_TOPK // (a + 1))]
_KEYS_PER_STEP = 4


def _peer_topk_kernel(sub_ref, i_ref, j_ref, g_ref):
    kk = PEER_TOPK
    neg = jnp.full((8, LANES), -jnp.inf, jnp.float32)
    zero = jnp.zeros((8, LANES), jnp.int32)

    def sorted_top(hp):
        def step(k4, carry):
            vals, idxs = list(carry[0]), list(carry[1])
            for u in range(_KEYS_PER_STEP):
                key = k4 * _KEYS_PER_STEP + u
                x = sub_ref[hp, pl.ds(pl.multiple_of(key * 8, 8), 8), :]
                _insert_sorted(vals, idxs, x, zero + key)
            return tuple(vals), tuple(idxs)

        return lax.fori_loop(0, N_KEYS // _KEYS_PER_STEP, step, ((neg,) * kk, (zero,) * kk))

    def head(h, carry):
        s1, i1 = sorted_top(2 * h)
        s2, i2 = sorted_top(2 * h + 1)
        best, expert = [neg] * kk, [zero] * kk
        for a, b in _PEER_PAIRS:
            _insert_sorted(best, expert, s1[a] + s2[b], i1[a] * N_KEYS + i2[b])
        e = [jnp.exp(v - best[0]) for v in best]
        denom = e[0]
        for v in e[1:]:
            denom = denom + v
        for k in range(kk):
            i_ref[h * kk + k] = expert[k] >> 7
            j_ref[h * kk + k] = expert[k] & (N_KEYS - 1)
            g_ref[h * kk + k] = e[k] / denom
        return carry

    lax.fori_loop(0, PEER_HEADS, head, 0)


def _peer_topk_call(sub):
    hp, chunks, rows, _ = sub.shape
    hk = PEER_HEADS * PEER_TOPK
    out = pl.BlockSpec((hk, 8, LANES), lambda i: (0, i, 0))
    osd = lambda dt: jax.ShapeDtypeStruct((hk, chunks * 8, LANES), dt)
    return pl.pallas_call(
        _peer_topk_kernel,
        grid=(chunks,),
        in_specs=[pl.BlockSpec((hp, None, rows, LANES), lambda i: (0, i, 0, 0))],
        out_specs=[out, out, out],
        out_shape=[osd(jnp.int32), osd(jnp.int32), osd(jnp.float32)],
        compiler_params=pltpu.CompilerParams(dimension_semantics=("arbitrary",), vmem_limit_bytes=_VMEM_LIMIT),
        name="peer_topk",
    )(sub)


def _peer_coef_kernel(i_ref, j_ref, g_ref, m_ref):
    tt = i_ref.shape[0]
    rid = lax.broadcasted_iota(jnp.int32, (N_KEYS, LANES), 0)

    def body(t, carry):
        irow = i_ref[pl.ds(t, 1), :]
        jrow = j_ref[pl.ds(t, 1), :]
        grow = g_ref[pl.ds(t, 1), :]
        rt = jnp.where(rid == irow, grow, 0.0)
        ct = jnp.where(rid == jrow, 1.0, 0.0)
        mt = _dot_nt(rt, ct)
        for e in range(N_KEYS // _PEER_IB):
            m_ref[e, t] = mt[e * _PEER_IB:(e + 1) * _PEER_IB, :]
        return carry

    lax.fori_loop(0, tt, body, 0, unroll=32)


def _peer_coef_call(it, jt, gt, tt):
    t, hk = it.shape
    row = pl.BlockSpec((tt, hk), lambda i: (i, 0))
    nb = N_KEYS // _PEER_IB
    return pl.pallas_call(
        _peer_coef_kernel,
        grid=(t // tt,),
        in_specs=[row, row, row],
        out_specs=pl.BlockSpec((nb, tt, _PEER_IB, N_KEYS), lambda i: (0, i, 0, 0)),
        out_shape=jax.ShapeDtypeStruct((nb, t, _PEER_IB, N_KEYS), jnp.float32),
        compiler_params=pltpu.CompilerParams(dimension_semantics=("arbitrary",), vmem_limit_bytes=_VMEM_LIMIT),
        name="peer_coef",
    )(it, jt, gt)


def _peer_dense_kernel(hn_ref, h_ref, ut_ref, v_ref, m_ref, o_ref, *, ib):
    e = pl.program_id(1)

    @pl.when(e == 0)
    def _():
        o_ref[...] = h_ref[...]

    tm = hn_ref.shape[0]
    a = jax.nn.gelu(jnp.dot(hn_ref[...], ut_ref[...], preferred_element_type=jnp.float32))
    c = jnp.concatenate([a[:, (b * ib + ii) * N_KEYS:(b * ib + ii + 1) * N_KEYS] * m_ref[b, pl.ds(ii, tm, stride=ib), :]
                         for b in range(m_ref.shape[0]) for ii in range(ib)], axis=1)
    o_ref[...] += jnp.dot(c.astype(_MXU_DTYPE), v_ref[...], preferred_element_type=jnp.float32)


def _peer_dense_call(hn, h1, ut, v, m3, tm, ib, nb):
    t, d = hn.shape
    te = nb * ib * N_KEYS
    kern = functools.partial(_peer_dense_kernel, ib=ib)
    return pl.pallas_call(
        kern,
        grid=(t // tm, N_KEYS // (nb * ib)),
        in_specs=[pl.BlockSpec((tm, d), lambda ti, e: (ti, 0)),
                  pl.BlockSpec((tm, d), lambda ti, e: (ti, 0)),
                  pl.BlockSpec((d, te), lambda ti, e: (0, e)),
                  pl.BlockSpec((te, d), lambda ti, e: (e, 0)),
                  pl.BlockSpec((nb, tm * ib, N_KEYS), lambda ti, e: (e, ti, 0))],
        out_specs=pl.BlockSpec((tm, d), lambda ti, e: (ti, 0)),
        out_shape=jax.ShapeDtypeStruct((t, d), jnp.float32),
        compiler_params=pltpu.CompilerParams(dimension_semantics=("arbitrary", "arbitrary"),
                                             vmem_limit_bytes=_VMEM_LIMIT),
        name="peer_dense",
    )(hn, h1, ut, v, m3)


def _rope_inv(rot_dim, period):
    half = rot_dim // 2
    inv = jnp.power(jnp.float32(ROPE_THETA), -jnp.arange(half, dtype=jnp.float32) * 2.0 / rot_dim)
    lane = np.arange(LANES)
    in_rot = (lane % period) < rot_dim
    pat = jnp.where(jnp.asarray(in_rot), inv[jnp.asarray(lane % period % half)], 0.0)
    return pat.reshape(1, LANES).astype(jnp.float32)


def _pack_w_in(w):
    d = w.shape[0]
    z = lambda n: jnp.zeros((d, n), w.dtype)
    segs = [w[:, 0:1536 + KV_LATENT]]
    for h in range(IDX_HEADS):
        segs += [w[:, 1792 + h * IDX_DIM:1792 + (h + 1) * IDX_DIM], z(LANES - IDX_DIM)]
    segs += [w[:, 2048:2116], z(LANES - 68)]
    segs += [w[:, 2116:]]
    out = jnp.concatenate(segs, axis=1)
    assert out.shape[1] == _W_COLS
    return out.astype(_MXU_DTYPE)


def kernel(x, positions, norm1_g, w_in, v_norm_g, v_norm_b, spatial_w, spatial_b, kv_norm_g, w_uk, w_uv,
           q_norm_g, k_norm_g, w_a_out, w_b_out, w_o, norm2_g, peer_wq, peer_subkeys, peer_u, peer_v):
    bsz, s, d = x.shape
    t = bsz * s
    depth = w_in.shape[0]
    mx = _MXU_DTYPE
    n_sel = min(TOPK_MAX, s // 4)
    tm = min(256, t)
    tk = min(256, s // 4)
    inv = _rope_inv(ROT_DIM, LANES) + jnp.roll(_rope_inv(IDX_ROT, LANES), ROT_DIM, axis=1)
    pos = positions.reshape(t, 1).astype(jnp.float32)
    r2 = lambda a: a.reshape(1, -1)
    h = x.reshape(t, d)
    for l in range(depth):
        ma, gb, q, k, v, qi, ki, wi = _inproj_call(
            h, pos, r2(norm1_g[l]), _pack_w_in(w_in[l]), r2(v_norm_g[l]), r2(v_norm_b[l]), spatial_w[l],
            spatial_b[l].T, r2(kv_norm_g[l]), w_uk[l].astype(mx), w_uv[l].astype(mx), r2(q_norm_g[l]),
            r2(k_norm_g[l]), w_a_out[l].astype(mx), inv, tm)
        b3 = lambda a: a.reshape(bsz, s, a.shape[-1])
        kT = b3(k).transpose(0, 2, 1)
        kiT = b3(ki).transpose(0, 2, 1)
        kn = wi[:, _KN_LANE].reshape(bsz, 1, s)
        yb = _dsa_call(b3(q), b3(qi), b3(wi), kT, kiT, b3(v), kn, n_sel, tk).reshape(t, 512)
        sk = peer_subkeys[l].reshape(2 * PEER_HEADS, N_KEYS, PEER_HALF).astype(mx)
        h1, hn, sub = _merge_call(h, ma, gb, yb, w_b_out[l].astype(mx), w_o[l].astype(mx), r2(norm2_g[l]),
                                  peer_wq[l].astype(mx), sk, tm)
        hk = PEER_HEADS * PEER_TOPK
        ei, ej, eg = (a.reshape(hk, t).T for a in _peer_topk_call(sub))
        m4 = _peer_coef_call(ei, ej, eg, min(32, t))
        m3 = m4.reshape(N_KEYS // _PEER_IB, t * _PEER_IB, N_KEYS)
        h = _peer_dense_call(hn, h1, peer_u[l].T.astype(mx), peer_v[l].astype(mx), m3, min(512, t), _PEER_IB, 2)
    return h.reshape(bsz, s, d)
```

```python
import functools
import math

import jax
import jax.numpy as jnp
import numpy as np
from jax import lax
from jax.experimental import pallas as pl
from jax.experimental.pallas import tpu as pltpu

EPS = 1e-6
ROPE_THETA = 500000.0
CHUNK = 128
A_GROUPS = 4
A_GROUP_DIM = 128
N_HEADS = 4
HEAD_DIM = 128
KV_LATENT = 256
ROT_DIM = HEAD_DIM // 4
IDX_HEADS = 4
IDX_DIM = 64
IDX_ROT = IDX_DIM // 4
TOPK_MAX = 256
Q_BLOCK = 128
PEER_HEADS = 8
PEER_HALF = 128
N_KEYS = 128
PEER_TOPK = 16
LANES = 128
_PEER_IB = 8

_MXU_DTYPE = jnp.bfloat16
_INT_MIN = -2147483648
_KEY_LOWEST_FINITE = -2139095040
_LIST_DEPTH = 12
_MIN_DENOM = 2.0 ** -100
_KN_LANE = IDX_DIM + IDX_HEADS
_NEG_BIG = -1e30
_VMEM_LIMIT = 56 * 1024 * 1024
_V_EXT = 2 * HEAD_DIM


def _dot(a, b):
    return jnp.dot(a.astype(_MXU_DTYPE), b.astype(_MXU_DTYPE), preferred_element_type=jnp.float32)


def _dot_nt(a, b):
    return lax.dot_general(a.astype(_MXU_DTYPE), b.astype(_MXU_DTYPE), (((1,), (1,)), ((), ())),
                           preferred_element_type=jnp.float32)


def _rms(x, g):
    return x * lax.rsqrt(jnp.mean(x * x, axis=-1, keepdims=True) + EPS) * g


def _rope(x, cos_t, sin_lo, sin_hi, half):
    n = x.shape[-1]
    x_up = pltpu.roll(x, n - half, 1)
    x_dn = pltpu.roll(x, half, 1)
    return x * cos_t + x_up * sin_lo + x_dn * sin_hi


def _rope_tables(pos, inv):
    ang = pos * inv
    c = jnp.cos(ang)
    s = jnp.sin(ang)
    lane = lax.broadcasted_iota(jnp.int32, ang.shape, 1)

    def tables(c, s, rot):
        half = rot // 2
        return (jnp.where(lane < rot, c, 1.0), jnp.where(lane < half, -s, 0.0),
                jnp.where((lane >= half) & (lane < rot), s, 0.0))

    back = LANES - ROT_DIM
    return tables(c, s, ROT_DIM), tables(pltpu.roll(c, back, 1), pltpu.roll(s, back, 1), IDX_ROT)


_OFF_U, _OFF_V, _OFF_Q, _OFF_C, _OFF_QI, _OFF_KI, _OFF_G = 0, 512, 1024, 1536, 1792, 2304, 2432
_W_COLS = 2432 + 2048


def _inproj_kernel(x_ref, pos_ref, g1_ref, w_ref, vg_ref, vb_ref, ws_ref, bs_ref, kvg_ref, wuk_ref, wuv_ref,
                   qg_ref, kg_ref, wa_ref, inv_ref,
                   ma_ref, gb_ref, q_ref, k_ref, v_ref, qi_ref, ki_ref, wi_ref):
    tm = x_ref.shape[0]
    x = x_ref[...]
    xn = _rms(x, g1_ref[...]).astype(_MXU_DTYPE)

    def proj(off, width):
        return jnp.dot(xn, w_ref[:, off:off + width], preferred_element_type=jnp.float32)

    u = jax.nn.gelu(proj(_OFF_U, 512))
    v = jax.nn.gelu(proj(_OFF_V, 512))
    mu = jnp.mean(v, axis=-1, keepdims=True)
    vc = v - mu
    v = vc * lax.rsqrt(jnp.mean(vc * vc, axis=-1, keepdims=True) + EPS) * vg_ref[...] + vb_ref[...]
    v = v.astype(_MXU_DTYPE)
    row = lax.broadcasted_iota(jnp.int32, (CHUNK, CHUNK), 0)
    col = lax.broadcasted_iota(jnp.int32, (CHUNK, CHUNK), 1)
    z_chunks = []
    for c in range(tm // CHUNK):
        zg = []
        for g in range(A_GROUPS):
            wt = jnp.where(row >= col, ws_ref[g], 0.0).astype(_MXU_DTYPE)
            vcg = v[c * CHUNK:(c + 1) * CHUNK, g * A_GROUP_DIM:(g + 1) * A_GROUP_DIM]
            zg.append(jnp.dot(wt, vcg, preferred_element_type=jnp.float32) + bs_ref[:, g:g + 1])
        z_chunks.append(jnp.concatenate(zg, axis=1))
    z = jnp.concatenate(z_chunks, axis=0) if len(z_chunks) > 1 else z_chunks[0]
    ya = u * z
    gate_a = jax.nn.sigmoid(proj(_OFF_G, 1024))
    ma_ref[...] = gate_a * _dot(ya, wa_ref[...])
    gb_ref[...] = jax.nn.sigmoid(proj(_OFF_G + 1024, 1024))

    pos = pos_ref[...]
    (cq, sq_lo, sq_hi), (ci, si_lo, si_hi) = _rope_tables(pos, inv_ref[...])

    q = proj(_OFF_Q, 512)
    qg = qg_ref[...]
    qh = [_rms(q[:, h * HEAD_DIM:(h + 1) * HEAD_DIM], qg) for h in range(N_HEADS)]
    qh = [_rope(t, cq, sq_lo, sq_hi, ROT_DIM // 2) for t in qh]
    q_ref[...] = (jnp.concatenate(qh, axis=1) * (HEAD_DIM ** -0.5 * math.log2(math.e))).astype(q_ref.dtype)

    c_n = _rms(proj(_OFF_C, KV_LATENT), kvg_ref[...]).astype(_MXU_DTYPE)
    kk = _rms(jnp.dot(c_n, wuk_ref[...], preferred_element_type=jnp.float32), kg_ref[...])
    k_out = _rope(kk, cq, sq_lo, sq_hi, ROT_DIM // 2).astype(k_ref.dtype)
    k_ref[...] = k_out
    k32 = k_out.astype(jnp.float32)
    k_norm2 = jnp.sum(k32 * k32, axis=-1, keepdims=True)
    vv = jnp.dot(c_n, wuv_ref[...], preferred_element_type=jnp.float32)
    ones_col = jnp.where(lax.broadcasted_iota(jnp.int32, vv.shape, 1) == 0, 1.0, 0.0)
    v_ref[...] = jnp.concatenate([vv, ones_col], axis=1).astype(v_ref.dtype)

    qi = proj(_OFF_QI, 512)
    qis = [_rope(qi[:, h * LANES:(h + 1) * LANES], ci, si_lo, si_hi, IDX_ROT // 2) for h in range(IDX_HEADS)]
    qi_ref[...] = (jnp.concatenate(qis, axis=1) * (IDX_DIM ** -0.5)).astype(qi_ref.dtype)
    kw = proj(_OFF_KI, LANES)
    lane = lax.broadcasted_iota(jnp.int32, kw.shape, 1)
    ki = _rope(jnp.where(lane < IDX_DIM, kw, 0.0), ci, si_lo, si_hi, IDX_ROT // 2)
    ki_ref[...] = ki.astype(ki_ref.dtype)
    wi_ref[...] = jnp.where(lane == _KN_LANE, k_norm2, kw * (IDX_HEADS ** -0.5))


def _inproj_call(x2, pos, g1, w_pack, vg, vb, ws, bs_t, kvg, wuk, wuv, qg, kg, wa, inv, tm):
    t, d = x2.shape
    full = lambda a: pl.BlockSpec(a.shape, lambda i: (0,) * a.ndim)
    row = lambda w: pl.BlockSpec((tm, w), lambda i: (i, 0))
    f32, mx = jnp.float32, _MXU_DTYPE
    outs = [(1024, f32), (1024, f32), (512, mx), (128, mx), (_V_EXT, mx), (512, mx), (128, mx), (128, f32)]
    return pl.pallas_call(
        _inproj_kernel,
        grid=(t // tm,),
        in_specs=[row(d), row(1), full(g1), full(w_pack), full(vg), full(vb), full(ws), full(bs_t), full(kvg),
                  full(wuk), full(wuv), full(qg), full(kg), full(wa), full(inv)],
        out_specs=[row(w) for w, _ in outs],
        out_shape=[jax.ShapeDtypeStruct((t, w), dt) for w, dt in outs],
        compiler_params=pltpu.CompilerParams(dimension_semantics=("arbitrary",), vmem_limit_bytes=_VMEM_LIMIT),
        name="inproj",
    )(x2, pos, g1, w_pack, vg, vb, ws, bs_t, kvg, wuk, wuv, qg, kg, wa, inv)


def _dsa_kernel(q_ref, qi_ref, wi_ref, kT_ref, kiT_ref, v_ref, kn_ref, o_ref,
                sc_ref, list_ref, listT_ref, thr_ref, need_ref, wb_ref, qa_ref, shift_ref, p_ref, m_ref, acc_ref,
                *, tk, n_sel):
    blk = pl.program_id(1)
    nq = Q_BLOCK
    ck = 2 * tk
    n_pairs = (blk * nq + nq + ck - 1) // ck

    wi = wi_ref[...]
    for h in range(IDX_HEADS):
        wb_ref[h] = jnp.broadcast_to(wi[:, IDX_DIM + h:IDX_DIM + h + 1], (nq, tk))

    cq = 2 * ck
    n_quads = (n_pairs + 1) // 2
    tiles_per_pair = ck // LANES
    tiles_per_quad = cq // LANES
    depth = _LIST_DEPTH
    qpos = blk * nq + lax.broadcasted_iota(jnp.int32, (nq, tk), 0)
    lane_pos = lax.broadcasted_iota(jnp.int32, (nq, tk), 1)

    def score_quad(qd):
        for sub in range(cq // tk):
            off = pl.multiple_of(qd * cq + sub * tk, tk)
            kt = kiT_ref[:, pl.ds(off, tk)]
            sc = None
            for h in range(IDX_HEADS):
                lg = jnp.dot(qi_ref[:, h * LANES:(h + 1) * LANES], kt, preferred_element_type=jnp.float32)
                term = wb_ref[h] * jnp.maximum(lg, 0.0)
                sc = term if sc is None else sc + term
            sc = jnp.where(off + lane_pos <= qpos, sc, -jnp.inf)
            for c in range(tk // LANES):
                sc_ref[qd * tiles_per_quad + sub * (tk // LANES) + c] = sc[:, c * LANES:(c + 1) * LANES]

    def insert_quad(qd):
        for g in range(nq // 8):
            rows = pl.ds(g * 8, 8)
            lists = [list_ref[i, rows, :] for i in range(depth)]
            for c in range(tiles_per_quad):
                x = sc_ref[qd * tiles_per_quad + c, rows, :]
                for i in range(depth):
                    lists[i], x = jnp.maximum(lists[i], x), jnp.minimum(lists[i], x)
            for i in range(depth):
                list_ref[i, rows, :] = lists[i]

    def score_step(qd, carry):
        score_quad(qd)
        insert_quad(qd - 1)
        return carry

    list_ref[...] = jnp.full(list_ref.shape, -jnp.inf, jnp.float32)
    score_quad(0)
    lax.fori_loop(1, n_quads, score_step, 0)
    insert_quad(n_quads - 1)


    def to_key(x):
        bits = pltpu.bitcast(x, jnp.int32)
        return bits ^ ((bits >> 31) & 0x7FFFFFFF)

    def to_score(key):
        return pltpu.bitcast(key ^ ((key >> 31) & 0x7FFFFFFF), jnp.float32)

    def kth_largest(count_ge, shape):
        zero = jnp.zeros(shape, jnp.int32)
        base = jnp.where(count_ge(zero) >= n_sel, zero, _INT_MIN)

        def bit_step(b, base):
            cand = base | jnp.left_shift(jnp.int32(1), 30 - b)
            return jnp.where(count_ge(cand) >= n_sel, cand, base)

        thr = lax.fori_loop(0, 31, bit_step, base)
        return jnp.maximum(thr, _KEY_LOWEST_FINITE)

    for i in range(depth):
        listT_ref[i] = to_key(list_ref[i].T)

    def count_lists(cand_row):
        tot = None
        for i in range(depth):
            hit = (listT_ref[i] >= cand_row).astype(jnp.int32)
            tot = hit if tot is None else tot + hit
        return jnp.sum(tot, axis=0, keepdims=True)

    thr_row = kth_largest(count_lists, (1, nq))
    need_row = n_sel - count_lists(thr_row + 1)
    thr_ref[...] = to_score(jnp.broadcast_to(thr_row, (nq, nq))).T[:, 0:1]
    need_ref[...] = jnp.broadcast_to(need_row, (nq, nq)).astype(jnp.float32).T[:, 0:1]
    overflow = jnp.max(jnp.where(listT_ref[depth - 1] > thr_row, 1, 0))

    @pl.when(overflow > 0)
    def _():
        def count_all(cand):
            cb = jnp.broadcast_to(cand, (nq, LANES))

            def body(j, cnt):
                for c in range(tiles_per_pair):
                    cnt = cnt + (to_key(sc_ref[j * tiles_per_pair + c]) >= cb).astype(jnp.int32)
                return cnt

            cnt = lax.fori_loop(0, n_pairs, body, jnp.zeros((nq, LANES), jnp.int32))
            return jnp.sum(cnt, axis=1, keepdims=True)

        thr_col = kth_largest(count_all, (nq, 1))
        thr_ref[...] = to_score(thr_col)
        need_ref[...] = (n_sel - count_all(thr_col + 1)).astype(jnp.float32)

    thr = jnp.broadcast_to(thr_ref[...], (nq, tk))
    need = need_ref[...]

    tri = (lax.broadcasted_iota(jnp.int32, (tk, tk), 0) <= lax.broadcasted_iota(jnp.int32, (tk, tk), 1))
    tri = jnp.where(tri, 1.0, 0.0).astype(_MXU_DTYPE)

    def selected(j, sub, room):
        t0 = (j * ck + sub * tk) // LANES
        sc = jnp.concatenate([sc_ref[t0 + c] for c in range(tk // LANES)], axis=1)
        eq = sc == thr
        pre = jnp.dot(jnp.where(eq, 1.0, 0.0).astype(_MXU_DTYPE), tri, preferred_element_type=jnp.float32)
        return (sc > thr) | (eq & (pre <= room)), room - pre[:, tk - 1:tk]

    q32 = q_ref[...].astype(jnp.float32)
    qn2 = None
    for h in range(N_HEADS):
        qh = q32[:, h * HEAD_DIM:(h + 1) * HEAD_DIM]
        n2 = jnp.sum(qh * qh, axis=1, keepdims=True)
        qn2 = n2 if qn2 is None else jnp.maximum(qn2, n2)
    kn2 = jnp.max(kn_ref[...], axis=1, keepdims=True)
    neg_bound = jnp.broadcast_to(-jnp.sqrt(qn2 * kn2), (nq, tk))
    eye = (lax.broadcasted_iota(jnp.int32, (nq, nq), 0) == lax.broadcasted_iota(jnp.int32, (nq, nq), 1))
    eye = jnp.where(eye, 1.0, 0.0).astype(_MXU_DTYPE)
    for h in range(N_HEADS):
        qa_ref[h] = jnp.concatenate([q_ref[:, h * HEAD_DIM:(h + 1) * HEAD_DIM], eye], axis=1)
    acc_ref[...] = jnp.zeros(acc_ref.shape, jnp.float32)

    last = n_quads - 1

    def stage_mask(qd, room):
        for sub in range(cq // tk):
            sel, room = selected(2 * qd, sub, room)
            shift_ref[qd % 2, :, sub * tk:(sub + 1) * tk] = jnp.where(sel, neg_bound, _NEG_BIG).astype(_MXU_DTYPE)
        return room

    def stage_weights(qd):
        rhs = jnp.concatenate([kT_ref[:, pl.ds(pl.multiple_of(qd * cq, cq), cq)], shift_ref[qd % 2]], axis=0)
        for h in range(N_HEADS):
            p_ref[qd % 2, h] = jnp.exp2(jnp.dot(qa_ref[h], rhs, preferred_element_type=jnp.float32)).astype(_MXU_DTYPE)

    def stage_values(qd):
        vt = v_ref[pl.ds(pl.multiple_of(qd * cq, cq), cq), :]
        for h in range(N_HEADS):
            acc_ref[h] += jnp.dot(p_ref[qd % 2, h], vt, preferred_element_type=jnp.float32)

    def shifted_step(j, room):
        stage_values(j - 1)
        stage_weights(j)
        return stage_mask(jnp.minimum(j + 1, last), room)

    room = stage_mask(0, need)
    stage_weights(0)
    room = stage_mask(jnp.minimum(1, last), room)
    lax.fori_loop(1, n_quads, shifted_step, room)
    stage_values(last)
    denom = acc_ref[0][:, HEAD_DIM:HEAD_DIM + 1]
    for h in range(1, N_HEADS):
        denom = jnp.minimum(denom, acc_ref[h][:, HEAD_DIM:HEAD_DIM + 1])
    l_min = jnp.min(denom)

    @pl.when(l_min < _MIN_DENOM)
    def _():
        m_ref[...] = jnp.full(m_ref.shape, _NEG_BIG, jnp.float32)
        acc_ref[...] = jnp.zeros(acc_ref.shape, jnp.float32)

        def online_pair(j, room):
            off = pl.multiple_of(j * ck, ck)
            biases = []
            for sub in range(2):
                sel, room = selected(j, sub, room)
                biases.append(jnp.where(sel, 0.0, _NEG_BIG))
            bias = jnp.concatenate(biases, axis=1)
            kt = kT_ref[:, pl.ds(off, ck)]
            vt = v_ref[pl.ds(off, ck), :]
            for h in range(N_HEADS):
                s = jnp.dot(q_ref[:, h * HEAD_DIM:(h + 1) * HEAD_DIM], kt, preferred_element_type=jnp.float32) + bias
                m_old = m_ref[h]
                m_new = jnp.maximum(m_old, jnp.max(s, axis=1, keepdims=True))
                alpha = jnp.exp2(m_old - m_new)
                p = jnp.exp2(s - jnp.concatenate([m_new] * (ck // LANES), axis=1))
                acc_ref[h] = (jnp.concatenate([alpha] * (_V_EXT // LANES), axis=1) * acc_ref[h]
                              + jnp.dot(p.astype(_MXU_DTYPE), vt, preferred_element_type=jnp.float32))
                m_ref[h] = m_new
            return room

        lax.fori_loop(0, n_pairs, online_pair, need)

    ys = []
    for h in range(N_HEADS):
        a = acc_ref[h]
        ys.append(a[:, :HEAD_DIM] / a[:, HEAD_DIM:HEAD_DIM + 1])
    o_ref[...] = jnp.concatenate(ys, axis=1).astype(o_ref.dtype)


def _dsa_call(q, qi, wi, kT, kiT, v, kn, n_sel, tk):
    b, s, _ = q.shape
    nblk = s // Q_BLOCK
    assert s % (4 * tk) == 0 and tk % Q_BLOCK == 0
    qspec = lambda w: pl.BlockSpec((None, Q_BLOCK, w), lambda bi, i: (bi, i, 0))
    kern = functools.partial(_dsa_kernel, tk=tk, n_sel=n_sel)
    return pl.pallas_call(
        kern,
        grid=(b, nblk),
        in_specs=[qspec(512), qspec(512), qspec(128),
                  pl.BlockSpec((None, HEAD_DIM, s), lambda bi, i: (bi, 0, 0)),
                  pl.BlockSpec((None, LANES, s), lambda bi, i: (bi, 0, 0)),
                  pl.BlockSpec((None, s, _V_EXT), lambda bi, i: (bi, 0, 0)),
                  pl.BlockSpec((None, 1, s), lambda bi, i: (bi, 0, 0))],
        out_specs=qspec(512),
        out_shape=jax.ShapeDtypeStruct((b, s, 512), _MXU_DTYPE),
        scratch_shapes=[pltpu.VMEM((s // LANES, Q_BLOCK, LANES), jnp.float32),
                        pltpu.VMEM((_LIST_DEPTH, Q_BLOCK, LANES), jnp.float32),
                        pltpu.VMEM((_LIST_DEPTH, LANES, Q_BLOCK), jnp.int32),
                        pltpu.VMEM((Q_BLOCK, 1), jnp.float32),
                        pltpu.VMEM((Q_BLOCK, 1), jnp.float32),
                        pltpu.VMEM((IDX_HEADS, Q_BLOCK, tk), jnp.float32),
                        pltpu.VMEM((N_HEADS, Q_BLOCK, HEAD_DIM + Q_BLOCK), _MXU_DTYPE),
                        pltpu.VMEM((2, Q_BLOCK, 4 * tk), _MXU_DTYPE),
                        pltpu.VMEM((2, N_HEADS, Q_BLOCK, 4 * tk), _MXU_DTYPE),
                        pltpu.VMEM((N_HEADS, Q_BLOCK, LANES), jnp.float32),
                        pltpu.VMEM((N_HEADS, Q_BLOCK, _V_EXT), jnp.float32)],
        compiler_params=pltpu.CompilerParams(dimension_semantics=("arbitrary", "arbitrary"),
                                             vmem_limit_bytes=_VMEM_LIMIT),
        name="dsa",
    )(q, qi, wi, kT, kiT, v, kn)


_TOK_CHUNK = 8 * LANES


def _merge_kernel(x_ref, ma_ref, gb_ref, yb_ref, wb_ref, wo_ref, g2_ref, wq_ref, sk_ref, h_ref, hn_ref, sub_ref):
    tm = x_ref.shape[0]
    merged = ma_ref[...] + gb_ref[...] * jnp.dot(yb_ref[...], wb_ref[...], preferred_element_type=jnp.float32)
    h1 = x_ref[...] + _dot(merged, wo_ref[...])
    h_ref[...] = h1
    hn = _rms(h1, g2_ref[...]).astype(_MXU_DTYPE)
    hn_ref[...] = hn
    qq = jnp.dot(hn, wq_ref[...], preferred_element_type=jnp.float32).astype(_MXU_DTYPE)
    groups = tm // LANES
    first = (pl.program_id(0) % (_TOK_CHUNK // tm)) * groups
    for hp in range(2 * PEER_HEADS):
        tile = _dot_nt(sk_ref[hp], qq[:, hp * PEER_HALF:(hp + 1) * PEER_HALF])
        for g in range(groups):
            sub_ref[hp, pl.ds(first + g, N_KEYS, stride=8), :] = tile[:, g * LANES:(g + 1) * LANES]


def _merge_call(x2, ma, gb, yb, wb, wo, g2, wq, sk, tm):
    t, d = x2.shape
    assert t % _TOK_CHUNK == 0 and _TOK_CHUNK % tm == 0
    full = lambda a: pl.BlockSpec(a.shape, lambda i: (0,) * a.ndim)
    row = lambda w: pl.BlockSpec((tm, w), lambda i: (i, 0))
    steps = _TOK_CHUNK // tm
    sub_shape = (2 * PEER_HEADS, t // _TOK_CHUNK, N_KEYS * 8, LANES)
    return pl.pallas_call(
        _merge_kernel,
        grid=(t // tm,),
        in_specs=[row(d), row(d), row(d), row(512), full(wb), full(wo), full(g2), full(wq), full(sk)],
        out_specs=[row(d), row(d),
                   pl.BlockSpec((2 * PEER_HEADS, None, N_KEYS * 8, LANES), lambda i: (0, i // steps, 0, 0))],
        out_shape=[jax.ShapeDtypeStruct((t, d), jnp.float32), jax.ShapeDtypeStruct((t, d), _MXU_DTYPE),
                   jax.ShapeDtypeStruct(sub_shape, jnp.float32)],
        compiler_params=pltpu.CompilerParams(dimension_semantics=("arbitrary",), vmem_limit_bytes=_VMEM_LIMIT),
        name="merge",
    )(x2, ma, gb, yb, wb, wo, g2, wq, sk)


def _insert_sorted(vals, pays, x, xp, first=0):
    for lvl in range(first, len(vals)):
        c = x > vals[lvl]
        vals[lvl], x = jnp.where(c, x, vals[lvl]), jnp.where(c, vals[lvl], x)
        pays[lvl], xp = jnp.where(c, xp, pays[lvl]), jnp.where(c, pays[lvl], xp)


_PEER_PAIRS = [(a, b) for a in range(PEER_TOPK) for b in range(PEER_TOPK // (a + 1))]
_KEYS_PER_STEP = 4


def _peer_topk_kernel(sub_ref, i_ref, j_ref, g_ref):
    kk = PEER_TOPK
    neg = jnp.full((8, LANES), -jnp.inf, jnp.float32)
    zero = jnp.zeros((8, LANES), jnp.int32)

    def sorted_top(hp):
        def step(k4, carry):
            vals, idxs = list(carry[0]), list(carry[1])
            for u in range(_KEYS_PER_STEP):
                key = k4 * _KEYS_PER_STEP + u
                x = sub_ref[hp, pl.ds(pl.multiple_of(key * 8, 8), 8), :]
                _insert_sorted(vals, idxs, x, zero + key)
            return tuple(vals), tuple(idxs)

        return lax.fori_loop(0, N_KEYS // _KEYS_PER_STEP, step, ((neg,) * kk, (zero,) * kk))

    def head(h, carry):
        s1, i1 = sorted_top(2 * h)
        s2, i2 = sorted_top(2 * h + 1)
        best, expert = [neg] * kk, [zero] * kk
        for a, b in _PEER_PAIRS:
            _insert_sorted(best, expert, s1[a] + s2[b], i1[a] * N_KEYS + i2[b], first=(a + 1) * (b + 1) - 1)
        e = [jnp.exp(v - best[0]) for v in best]
        denom = e[0]
        for v in e[1:]:
            denom = denom + v
        for k in range(kk):
            i_ref[h * kk + k] = expert[k] >> 7
            j_ref[h * kk + k] = expert[k] & (N_KEYS - 1)
            g_ref[h * kk + k] = e[k] / denom
        return carry

    lax.fori_loop(0, PEER_HEADS, head, 0)


def _peer_topk_call(sub):
    hp, chunks, rows, _ = sub.shape
    hk = PEER_HEADS * PEER_TOPK
    out = pl.BlockSpec((hk, 8, LANES), lambda i: (0, i, 0))
    osd = lambda dt: jax.ShapeDtypeStruct((hk, chunks * 8, LANES), dt)
    return pl.pallas_call(
        _peer_topk_kernel,
        grid=(chunks,),
        in_specs=[pl.BlockSpec((hp, None, rows, LANES), lambda i: (0, i, 0, 0))],
        out_specs=[out, out, out],
        out_shape=[osd(jnp.int32), osd(jnp.int32), osd(jnp.float32)],
        compiler_params=pltpu.CompilerParams(dimension_semantics=("arbitrary",), vmem_limit_bytes=_VMEM_LIMIT),
        name="peer_topk",
    )(sub)


def _peer_coef_kernel(i_ref, j_ref, g_ref, m_ref):
    tt = i_ref.shape[0]
    rid = lax.broadcasted_iota(jnp.int32, (N_KEYS, LANES), 0)

    def body(t, carry):
        irow = i_ref[pl.ds(t, 1), :]
        jrow = j_ref[pl.ds(t, 1), :]
        grow = g_ref[pl.ds(t, 1), :]
        rt = jnp.where(rid == irow, grow, 0.0)
        ct = jnp.where(rid == jrow, 1.0, 0.0)
        mt = _dot_nt(rt, ct)
        for e in range(N_KEYS // _PEER_IB):
            m_ref[e, t] = mt[e * _PEER_IB:(e + 1) * _PEER_IB, :]
        return carry

    lax.fori_loop(0, tt, body, 0, unroll=32)


def _peer_coef_call(it, jt, gt, tt):
    t, hk = it.shape
    row = pl.BlockSpec((tt, hk), lambda i: (i, 0))
    nb = N_KEYS // _PEER_IB
    return pl.pallas_call(
        _peer_coef_kernel,
        grid=(t // tt,),
        in_specs=[row, row, row],
        out_specs=pl.BlockSpec((nb, tt, _PEER_IB, N_KEYS), lambda i: (0, i, 0, 0)),
        out_shape=jax.ShapeDtypeStruct((nb, t, _PEER_IB, N_KEYS), jnp.float32),
        compiler_params=pltpu.CompilerParams(dimension_semantics=("arbitrary",), vmem_limit_bytes=_VMEM_LIMIT),
        name="peer_coef",
    )(it, jt, gt)


def _peer_dense_kernel(hn_ref, h_ref, ut_ref, v_ref, m_ref, o_ref, *, ib):
    e = pl.program_id(1)

    @pl.when(e == 0)
    def _():
        o_ref[...] = h_ref[...]

    tm = hn_ref.shape[0]
    a = jax.nn.gelu(jnp.dot(hn_ref[...], ut_ref[...], preferred_element_type=jnp.float32))
    c = jnp.concatenate([a[:, (b * ib + ii) * N_KEYS:(b * ib + ii + 1) * N_KEYS] * m_ref[b, pl.ds(ii, tm, stride=ib), :]
                         for b in range(m_ref.shape[0]) for ii in range(ib)], axis=1)
    o_ref[...] += jnp.dot(c.astype(_MXU_DTYPE), v_ref[...], preferred_element_type=jnp.float32)


def _peer_dense_call(hn, h1, ut, v, m3, tm, ib, nb):
    t, d = hn.shape
    te = nb * ib * N_KEYS
    kern = functools.partial(_peer_dense_kernel, ib=ib)
    return pl.pallas_call(
        kern,
        grid=(t // tm, N_KEYS // (nb * ib)),
        in_specs=[pl.BlockSpec((tm, d), lambda ti, e: (ti, 0)),
                  pl.BlockSpec((tm, d), lambda ti, e: (ti, 0)),
                  pl.BlockSpec((d, te), lambda ti, e: (0, e)),
                  pl.BlockSpec((te, d), lambda ti, e: (e, 0)),
                  pl.BlockSpec((nb, tm * ib, N_KEYS), lambda ti, e: (e, ti, 0))],
        out_specs=pl.BlockSpec((tm, d), lambda ti, e: (ti, 0)),
        out_shape=jax.ShapeDtypeStruct((t, d), jnp.float32),
        compiler_params=pltpu.CompilerParams(dimension_semantics=("arbitrary", "arbitrary"),
                                             vmem_limit_bytes=_VMEM_LIMIT),
        name="peer_dense",
    )(hn, h1, ut, v, m3)


def _rope_inv(rot_dim, period):
    half = rot_dim // 2
    inv = jnp.power(jnp.float32(ROPE_THETA), -jnp.arange(half, dtype=jnp.float32) * 2.0 / rot_dim)
    lane = np.arange(LANES)
    in_rot = (lane % period) < rot_dim
    pat = jnp.where(jnp.asarray(in_rot), inv[jnp.asarray(lane % period % half)], 0.0)
    return pat.reshape(1, LANES).astype(jnp.float32)


def _pack_w_in(w):
    d = w.shape[0]
    z = lambda n: jnp.zeros((d, n), w.dtype)
    segs = [w[:, 0:1536 + KV_LATENT]]
    for h in range(IDX_HEADS):
        segs += [w[:, 1792 + h * IDX_DIM:1792 + (h + 1) * IDX_DIM], z(LANES - IDX_DIM)]
    segs += [w[:, 2048:2116], z(LANES - 68)]
    segs += [w[:, 2116:]]
    out = jnp.concatenate(segs, axis=1)
    assert out.shape[1] == _W_COLS
    return out.astype(_MXU_DTYPE)


def kernel(x, positions, norm1_g, w_in, v_norm_g, v_norm_b, spatial_w, spatial_b, kv_norm_g, w_uk, w_uv,
           q_norm_g, k_norm_g, w_a_out, w_b_out, w_o, norm2_g, peer_wq, peer_subkeys, peer_u, peer_v):
    bsz, s, d = x.shape
    t = bsz * s
    depth = w_in.shape[0]
    mx = _MXU_DTYPE
    n_sel = min(TOPK_MAX, s // 4)
    tm = min(256, t)
    tk = min(256, s // 4)
    inv = _rope_inv(ROT_DIM, LANES) + jnp.roll(_rope_inv(IDX_ROT, LANES), ROT_DIM, axis=1)
    pos = positions.reshape(t, 1).astype(jnp.float32)
    r2 = lambda a: a.reshape(1, -1)
    h = x.reshape(t, d)
    for l in range(depth):
        ma, gb, q, k, v, qi, ki, wi = _inproj_call(
            h, pos, r2(norm1_g[l]), _pack_w_in(w_in[l]), r2(v_norm_g[l]), r2(v_norm_b[l]), spatial_w[l],
            spatial_b[l].T, r2(kv_norm_g[l]), w_uk[l].astype(mx), w_uv[l].astype(mx), r2(q_norm_g[l]),
            r2(k_norm_g[l]), w_a_out[l].astype(mx), inv, tm)
        b3 = lambda a: a.reshape(bsz, s, a.shape[-1])
        kT = b3(k).transpose(0, 2, 1)
        kiT = b3(ki).transpose(0, 2, 1)
        kn = wi[:, _KN_LANE].reshape(bsz, 1, s)
        yb = _dsa_call(b3(q), b3(qi), b3(wi), kT, kiT, b3(v), kn, n_sel, tk).reshape(t, 512)
        sk = peer_subkeys[l].reshape(2 * PEER_HEADS, N_KEYS, PEER_HALF).astype(mx)
        h1, hn, sub = _merge_call(h, ma, gb, yb, w_b_out[l].astype(mx), w_o[l].astype(mx), r2(norm2_g[l]),
                                  peer_wq[l].astype(mx), sk, tm)
        hk = PEER_HEADS * PEER_TOPK
        ei, ej, eg = (a.reshape(hk, t).T for a in _peer_topk_call(sub))
        m4 = _peer_coef_call(ei, ej, eg, min(32, t))
        m3 = m4.reshape(N_KEYS // _PEER_IB, t * _PEER_IB, N_KEYS)
        h = _peer_dense_call(hn, h1, peer_u[l].T.astype(mx), peer_v[l].astype(mx), m3, min(512, t), _PEER_IB, 2)
    return h.reshape(bsz, s, d)
```

```python
import functools
import math

import jax
import jax.numpy as jnp
import numpy as np
from jax import lax
from jax.experimental import pallas as pl
from jax.experimental.pallas import tpu as pltpu

EPS = 1e-6
ROPE_THETA = 500000.0
CHUNK = 128
A_GROUPS = 4
A_GROUP_DIM = 128
N_HEADS = 4
HEAD_DIM = 128
KV_LATENT = 256
ROT_DIM = HEAD_DIM // 4
IDX_HEADS = 4
IDX_DIM = 64
IDX_ROT = IDX_DIM // 4
TOPK_MAX = 256
Q_BLOCK = 128
PEER_HEADS = 8
PEER_HALF = 128
N_KEYS = 128
PEER_TOPK = 16
LANES = 128
_PEER_IB = 8

_MXU_DTYPE = jnp.bfloat16
_INT_MIN = -2147483648
_KEY_LOWEST_FINITE = -2139095040
_LIST_DEPTH = 12
_MIN_DENOM = 2.0 ** -100
_KN_LANE = IDX_DIM + IDX_HEADS
_NEG_BIG = -1e30
_VMEM_LIMIT = 56 * 1024 * 1024
_V_EXT = 2 * HEAD_DIM


def _dot(a, b):
    return jnp.dot(a.astype(_MXU_DTYPE), b.astype(_MXU_DTYPE), preferred_element_type=jnp.float32)


def _dot_nt(a, b):
    return lax.dot_general(a.astype(_MXU_DTYPE), b.astype(_MXU_DTYPE), (((1,), (1,)), ((), ())),
                           preferred_element_type=jnp.float32)


def _rms(x, g):
    return x * lax.rsqrt(jnp.mean(x * x, axis=-1, keepdims=True) + EPS) * g


def _rope(x, cos_t, sin_lo, sin_hi, half):
    n = x.shape[-1]
    x_up = pltpu.roll(x, n - half, 1)
    x_dn = pltpu.roll(x, half, 1)
    return x * cos_t + x_up * sin_lo + x_dn * sin_hi


def _rope_tables(pos, inv):
    ang = pos * inv
    c = jnp.cos(ang)
    s = jnp.sin(ang)
    lane = lax.broadcasted_iota(jnp.int32, ang.shape, 1)

    def tables(c, s, rot):
        half = rot // 2
        return (jnp.where(lane < rot, c, 1.0), jnp.where(lane < half, -s, 0.0),
                jnp.where((lane >= half) & (lane < rot), s, 0.0))

    back = LANES - ROT_DIM
    return tables(c, s, ROT_DIM), tables(pltpu.roll(c, back, 1), pltpu.roll(s, back, 1), IDX_ROT)


_OFF_U, _OFF_V, _OFF_Q, _OFF_C, _OFF_QI, _OFF_KI, _OFF_G = 0, 512, 1024, 1536, 1792, 2304, 2432
_W_COLS = 2432 + 2048


def _inproj_kernel(x_ref, pos_ref, g1_ref, w_ref, vg_ref, vb_ref, ws_ref, bs_ref, kvg_ref, wuk_ref, wuv_ref,
                   qg_ref, kg_ref, wa_ref, inv_ref,
                   ma_ref, gb_ref, q_ref, k_ref, v_ref, qi_ref, ki_ref, wi_ref):
    tm = x_ref.shape[0]
    x = x_ref[...]
    xn = _rms(x, g1_ref[...]).astype(_MXU_DTYPE)

    def proj(off, width):
        return jnp.dot(xn, w_ref[:, off:off + width], preferred_element_type=jnp.float32)

    u = jax.nn.gelu(proj(_OFF_U, 512))
    v = jax.nn.gelu(proj(_OFF_V, 512))
    mu = jnp.mean(v, axis=-1, keepdims=True)
    vc = v - mu
    v = vc * lax.rsqrt(jnp.mean(vc * vc, axis=-1, keepdims=True) + EPS) * vg_ref[...] + vb_ref[...]
    v = v.astype(_MXU_DTYPE)
    row = lax.broadcasted_iota(jnp.int32, (CHUNK, CHUNK), 0)
    col = lax.broadcasted_iota(jnp.int32, (CHUNK, CHUNK), 1)
    z_chunks = []
    for c in range(tm // CHUNK):
        zg = []
        for g in range(A_GROUPS):
            wt = jnp.where(row >= col, ws_ref[g], 0.0).astype(_MXU_DTYPE)
            vcg = v[c * CHUNK:(c + 1) * CHUNK, g * A_GROUP_DIM:(g + 1) * A_GROUP_DIM]
            zg.append(jnp.dot(wt, vcg, preferred_element_type=jnp.float32) + bs_ref[:, g:g + 1])
        z_chunks.append(jnp.concatenate(zg, axis=1))
    z = jnp.concatenate(z_chunks, axis=0) if len(z_chunks) > 1 else z_chunks[0]
    ya = u * z
    gate_a = jax.nn.sigmoid(proj(_OFF_G, 1024))
    ma_ref[...] = gate_a * _dot(ya, wa_ref[...])
    gb_ref[...] = jax.nn.sigmoid(proj(_OFF_G + 1024, 1024))

    pos = pos_ref[...]
    (cq, sq_lo, sq_hi), (ci, si_lo, si_hi) = _rope_tables(pos, inv_ref[...])

    q = proj(_OFF_Q, 512)
    qg = qg_ref[...]
    qh = [_rms(q[:, h * HEAD_DIM:(h + 1) * HEAD_DIM], qg) for h in range(N_HEADS)]
    qh = [_rope(t, cq, sq_lo, sq_hi, ROT_DIM // 2) for t in qh]
    q_ref[...] = (jnp.concatenate(qh, axis=1) * (HEAD_DIM ** -0.5 * math.log2(math.e))).astype(q_ref.dtype)

    c_n = _rms(proj(_OFF_C, KV_LATENT), kvg_ref[...]).astype(_MXU_DTYPE)
    kk = _rms(jnp.dot(c_n, wuk_ref[...], preferred_element_type=jnp.float32), kg_ref[...])
    k_out = _rope(kk, cq, sq_lo, sq_hi, ROT_DIM // 2).astype(k_ref.dtype)
    k_ref[...] = k_out
    k32 = k_out.astype(jnp.float32)
    k_norm2 = jnp.sum(k32 * k32, axis=-1, keepdims=True)
    vv = jnp.dot(c_n, wuv_ref[...], preferred_element_type=jnp.float32)
    ones_col = jnp.where(lax.broadcasted_iota(jnp.int32, vv.shape, 1) == 0, 1.0, 0.0)
    v_ref[...] = jnp.concatenate([vv, ones_col], axis=1).astype(v_ref.dtype)

    qi = proj(_OFF_QI, 512)
    qis = [_rope(qi[:, h * LANES:(h + 1) * LANES], ci, si_lo, si_hi, IDX_ROT // 2) for h in range(IDX_HEADS)]
    qi_ref[...] = (jnp.concatenate(qis, axis=1) * (IDX_DIM ** -0.5)).astype(qi_ref.dtype)
    kw = proj(_OFF_KI, LANES)
    lane = lax.broadcasted_iota(jnp.int32, kw.shape, 1)
    ki = _rope(jnp.where(lane < IDX_DIM, kw, 0.0), ci, si_lo, si_hi, IDX_ROT // 2)
    ki_ref[...] = ki.astype(ki_ref.dtype)
    wi_ref[...] = jnp.where(lane == _KN_LANE, k_norm2, kw * (IDX_HEADS ** -0.5))


def _inproj_call(x2, pos, g1, w_pack, vg, vb, ws, bs_t, kvg, wuk, wuv, qg, kg, wa, inv, tm):
    t, d = x2.shape
    full = lambda a: pl.BlockSpec(a.shape, lambda i: (0,) * a.ndim)
    row = lambda w: pl.BlockSpec((tm, w), lambda i: (i, 0))
    f32, mx = jnp.float32, _MXU_DTYPE
    outs = [(1024, f32), (1024, f32), (512, mx), (128, mx), (_V_EXT, mx), (512, mx), (128, mx), (128, f32)]
    return pl.pallas_call(
        _inproj_kernel,
        grid=(t // tm,),
        in_specs=[row(d), row(1), full(g1), full(w_pack), full(vg), full(vb), full(ws), full(bs_t), full(kvg),
                  full(wuk), full(wuv), full(qg), full(kg), full(wa), full(inv)],
        out_specs=[row(w) for w, _ in outs],
        out_shape=[jax.ShapeDtypeStruct((t, w), dt) for w, dt in outs],
        compiler_params=pltpu.CompilerParams(dimension_semantics=("arbitrary",), vmem_limit_bytes=_VMEM_LIMIT),
        name="inproj",
    )(x2, pos, g1, w_pack, vg, vb, ws, bs_t, kvg, wuk, wuv, qg, kg, wa, inv)


def _dsa_kernel(q_ref, qi_ref, wi_ref, kT_ref, kiT_ref, v_ref, kn_ref, o_ref,
                sc_ref, list_ref, listT_ref, thr_ref, need_ref, wb_ref, qa_ref, shift_ref, p_ref, m_ref, acc_ref,
                *, tk, n_sel):
    blk = pl.program_id(1)
    nq = Q_BLOCK
    ck = 2 * tk
    n_pairs = (blk * nq + nq + ck - 1) // ck

    wi = wi_ref[...]
    for h in range(IDX_HEADS):
        wb_ref[h] = jnp.broadcast_to(wi[:, IDX_DIM + h:IDX_DIM + h + 1], (nq, tk))

    cq = 2 * ck
    n_quads = (n_pairs + 1) // 2
    tiles_per_pair = ck // LANES
    tiles_per_quad = cq // LANES
    depth = _LIST_DEPTH
    qpos = blk * nq + lax.broadcasted_iota(jnp.int32, (nq, tk), 0)
    lane_pos = lax.broadcasted_iota(jnp.int32, (nq, tk), 1)

    def score_quad(qd):
        for sub in range(cq // tk):
            off = pl.multiple_of(qd * cq + sub * tk, tk)
            kt = kiT_ref[:, pl.ds(off, tk)]
            sc = None
            for h in range(IDX_HEADS):
                lg = jnp.dot(qi_ref[:, h * LANES:(h + 1) * LANES], kt, preferred_element_type=jnp.float32)
                term = wb_ref[h] * jnp.maximum(lg, 0.0)
                sc = term if sc is None else sc + term
            sc = jnp.where(off + lane_pos <= qpos, sc, -jnp.inf)
            for c in range(tk // LANES):
                sc_ref[qd * tiles_per_quad + sub * (tk // LANES) + c] = sc[:, c * LANES:(c + 1) * LANES]

    def insert_quad(qd):
        for g in range(nq // 8):
            rows = pl.ds(g * 8, 8)
            lists = [list_ref[i, rows, :] for i in range(depth)]
            for c in range(tiles_per_quad):
                x = sc_ref[qd * tiles_per_quad + c, rows, :]
                for i in range(depth):
                    lists[i], x = jnp.maximum(lists[i], x), jnp.minimum(lists[i], x)
            for i in range(depth):
                list_ref[i, rows, :] = lists[i]

    def score_step(qd, carry):
        score_quad(qd)
        insert_quad(qd - 1)
        return carry

    list_ref[...] = jnp.full(list_ref.shape, -jnp.inf, jnp.float32)
    score_quad(0)
    lax.fori_loop(1, n_quads, score_step, 0)
    insert_quad(n_quads - 1)


    def to_key(x):
        bits = pltpu.bitcast(x, jnp.int32)
        return bits ^ ((bits >> 31) & 0x7FFFFFFF)

    def to_score(key):
        return pltpu.bitcast(key ^ ((key >> 31) & 0x7FFFFFFF), jnp.float32)

    def kth_largest(count_ge, shape):
        zero = jnp.zeros(shape, jnp.int32)
        base = jnp.where(count_ge(zero) >= n_sel, zero, _INT_MIN)

        def bit_step(b, base):
            cand = base | jnp.left_shift(jnp.int32(1), 30 - b)
            return jnp.where(count_ge(cand) >= n_sel, cand, base)

        thr = lax.fori_loop(0, 31, bit_step, base)
        return jnp.maximum(thr, _KEY_LOWEST_FINITE)

    for i in range(depth):
        listT_ref[i] = to_key(list_ref[i].T)

    def count_lists(cand_row):
        tot = None
        for i in range(depth):
            hit = (listT_ref[i] >= cand_row).astype(jnp.int32)
            tot = hit if tot is None else tot + hit
        return jnp.sum(tot, axis=0, keepdims=True)

    thr_row = kth_largest(count_lists, (1, nq))
    need_row = n_sel - count_lists(thr_row + 1)
    thr_ref[...] = to_score(jnp.broadcast_to(thr_row, (nq, nq))).T[:, 0:1]
    need_ref[...] = jnp.broadcast_to(need_row, (nq, nq)).astype(jnp.float32).T[:, 0:1]
    overflow = jnp.max(jnp.where(listT_ref[depth - 1] > thr_row, 1, 0))

    @pl.when(overflow > 0)
    def _():
        def count_all(cand):
            cb = jnp.broadcast_to(cand, (nq, LANES))

            def body(j, cnt):
                for c in range(tiles_per_pair):
                    cnt = cnt + (to_key(sc_ref[j * tiles_per_pair + c]) >= cb).astype(jnp.int32)
                return cnt

            cnt = lax.fori_loop(0, n_pairs, body, jnp.zeros((nq, LANES), jnp.int32))
            return jnp.sum(cnt, axis=1, keepdims=True)

        thr_col = kth_largest(count_all, (nq, 1))
        thr_ref[...] = to_score(thr_col)
        need_ref[...] = (n_sel - count_all(thr_col + 1)).astype(jnp.float32)

    thr = jnp.broadcast_to(thr_ref[...], (nq, tk))
    need = need_ref[...]

    tri = (lax.broadcasted_iota(jnp.int32, (tk, tk), 0) <= lax.broadcasted_iota(jnp.int32, (tk, tk), 1))
    tri = jnp.where(tri, 1.0, 0.0).astype(_MXU_DTYPE)

    def masked(first_tile, n_tiles, room, on, off):
        no_tie, all_tie, pres = [], [], []
        for k in range(n_tiles):
            t0 = (first_tile + k) * (tk // LANES)
            sc = jnp.concatenate([sc_ref[t0 + c] for c in range(tk // LANES)], axis=1)
            eq = sc == thr
            base = jnp.where(sc > thr, on, off)
            no_tie.append(base)
            all_tie.append(jnp.where(eq, on, base))
            pres.append(jnp.dot(jnp.where(eq, 1.0, 0.0).astype(_MXU_DTYPE), tri, preferred_element_type=jnp.float32))
        rooms = [room]
        for k in range(n_tiles):
            rooms.append(rooms[k] - pres[k][:, tk - 1:tk])
        tiles = [jnp.where(pres[k] <= rooms[k], all_tie[k], no_tie[k]) for k in range(n_tiles)]
        return tiles, rooms[n_tiles]

    q32 = q_ref[...].astype(jnp.float32)
    qn2 = None
    for h in range(N_HEADS):
        qh = q32[:, h * HEAD_DIM:(h + 1) * HEAD_DIM]
        n2 = jnp.sum(qh * qh, axis=1, keepdims=True)
        qn2 = n2 if qn2 is None else jnp.maximum(qn2, n2)
    kn2 = jnp.max(kn_ref[...], axis=1, keepdims=True)
    neg_bound = jnp.broadcast_to(-jnp.sqrt(qn2 * kn2), (nq, tk))
    eye = (lax.broadcasted_iota(jnp.int32, (nq, nq), 0) == lax.broadcasted_iota(jnp.int32, (nq, nq), 1))
    eye = jnp.where(eye, 1.0, 0.0).astype(_MXU_DTYPE)
    for h in range(N_HEADS):
        qa_ref[h] = jnp.concatenate([q_ref[:, h * HEAD_DIM:(h + 1) * HEAD_DIM], eye], axis=1)
    acc_ref[...] = jnp.zeros(acc_ref.shape, jnp.float32)

    last = n_quads - 1

    def stage_mask(qd, room):
        shifts, room = masked(qd * (cq // tk), cq // tk, room, neg_bound, _NEG_BIG)
        for sub, shift in enumerate(shifts):
            shift_ref[qd % 2, :, sub * tk:(sub + 1) * tk] = shift.astype(_MXU_DTYPE)
        return room

    def stage_weights(qd):
        rhs = jnp.concatenate([kT_ref[:, pl.ds(pl.multiple_of(qd * cq, cq), cq)], shift_ref[qd % 2]], axis=0)
        qa = qa_ref[...].reshape(N_HEADS * nq, HEAD_DIM + nq)
        p = jnp.exp2(jnp.dot(qa, rhs, preferred_element_type=jnp.float32)).astype(_MXU_DTYPE)
        p_ref[qd % 2] = p.reshape(N_HEADS, nq, cq)

    def stage_values(qd):
        vt = v_ref[pl.ds(pl.multiple_of(qd * cq, cq), cq), :]
        pv = jnp.dot(p_ref[qd % 2].reshape(N_HEADS * nq, cq), vt, preferred_element_type=jnp.float32)
        acc_ref[...] += pv.reshape(N_HEADS, nq, _V_EXT)

    def shifted_step(j, room):
        stage_values(j - 1)
        stage_weights(j)
        return stage_mask(jnp.minimum(j + 1, last), room)

    room = stage_mask(0, need)
    stage_weights(0)
    room = stage_mask(jnp.minimum(1, last), room)
    lax.fori_loop(1, n_quads, shifted_step, room)
    stage_values(last)
    denom = acc_ref[0][:, HEAD_DIM:HEAD_DIM + 1]
    for h in range(1, N_HEADS):
        denom = jnp.minimum(denom, acc_ref[h][:, HEAD_DIM:HEAD_DIM + 1])
    l_min = jnp.min(denom)

    @pl.when(l_min < _MIN_DENOM)
    def _():
        m_ref[...] = jnp.full(m_ref.shape, _NEG_BIG, jnp.float32)
        acc_ref[...] = jnp.zeros(acc_ref.shape, jnp.float32)

        def online_pair(j, room):
            off = pl.multiple_of(j * ck, ck)
            biases, room = masked(j * (ck // tk), ck // tk, room, 0.0, _NEG_BIG)
            bias = jnp.concatenate(biases, axis=1)
            kt = kT_ref[:, pl.ds(off, ck)]
            vt = v_ref[pl.ds(off, ck), :]
            for h in range(N_HEADS):
                s = jnp.dot(q_ref[:, h * HEAD_DIM:(h + 1) * HEAD_DIM], kt, preferred_element_type=jnp.float32) + bias
                m_old = m_ref[h]
                m_new = jnp.maximum(m_old, jnp.max(s, axis=1, keepdims=True))
                alpha = jnp.exp2(m_old - m_new)
                p = jnp.exp2(s - jnp.concatenate([m_new] * (ck // LANES), axis=1))
                acc_ref[h] = (jnp.concatenate([alpha] * (_V_EXT // LANES), axis=1) * acc_ref[h]
                              + jnp.dot(p.astype(_MXU_DTYPE), vt, preferred_element_type=jnp.float32))
                m_ref[h] = m_new
            return room

        lax.fori_loop(0, n_pairs, online_pair, need)

    ys = []
    for h in range(N_HEADS):
        a = acc_ref[h]
        ys.append(a[:, :HEAD_DIM] / a[:, HEAD_DIM:HEAD_DIM + 1])
    o_ref[...] = jnp.concatenate(ys, axis=1).astype(o_ref.dtype)


def _dsa_call(q, qi, wi, kT, kiT, v, kn, n_sel, tk):
    b, s, _ = q.shape
    nblk = s // Q_BLOCK
    assert s % (4 * tk) == 0 and tk % Q_BLOCK == 0
    qspec = lambda w: pl.BlockSpec((None, Q_BLOCK, w), lambda bi, i: (bi, i, 0))
    kern = functools.partial(_dsa_kernel, tk=tk, n_sel=n_sel)
    return pl.pallas_call(
        kern,
        grid=(b, nblk),
        in_specs=[qspec(512), qspec(512), qspec(128),
                  pl.BlockSpec((None, HEAD_DIM, s), lambda bi, i: (bi, 0, 0)),
                  pl.BlockSpec((None, LANES, s), lambda bi, i: (bi, 0, 0)),
                  pl.BlockSpec((None, s, _V_EXT), lambda bi, i: (bi, 0, 0)),
                  pl.BlockSpec((None, 1, s), lambda bi, i: (bi, 0, 0))],
        out_specs=qspec(512),
        out_shape=jax.ShapeDtypeStruct((b, s, 512), _MXU_DTYPE),
        scratch_shapes=[pltpu.VMEM((s // LANES, Q_BLOCK, LANES), jnp.float32),
                        pltpu.VMEM((_LIST_DEPTH, Q_BLOCK, LANES), jnp.float32),
                        pltpu.VMEM((_LIST_DEPTH, LANES, Q_BLOCK), jnp.int32),
                        pltpu.VMEM((Q_BLOCK, 1), jnp.float32),
                        pltpu.VMEM((Q_BLOCK, 1), jnp.float32),
                        pltpu.VMEM((IDX_HEADS, Q_BLOCK, tk), jnp.float32),
                        pltpu.VMEM((N_HEADS, Q_BLOCK, HEAD_DIM + Q_BLOCK), _MXU_DTYPE),
                        pltpu.VMEM((2, Q_BLOCK, 4 * tk), _MXU_DTYPE),
                        pltpu.VMEM((2, N_HEADS, Q_BLOCK, 4 * tk), _MXU_DTYPE),
                        pltpu.VMEM((N_HEADS, Q_BLOCK, LANES), jnp.float32),
                        pltpu.VMEM((N_HEADS, Q_BLOCK, _V_EXT), jnp.float32)],
        compiler_params=pltpu.CompilerParams(dimension_semantics=("arbitrary", "arbitrary"),
                                             vmem_limit_bytes=_VMEM_LIMIT),
        name="dsa",
    )(q, qi, wi, kT, kiT, v, kn)


_TOK_CHUNK = 8 * LANES


def _merge_kernel(x_ref, ma_ref, gb_ref, yb_ref, wb_ref, wo_ref, g2_ref, wq_ref, sk_ref, h_ref, hn_ref, sub_ref):
    tm = x_ref.shape[0]
    merged = ma_ref[...] + gb_ref[...] * jnp.dot(yb_ref[...], wb_ref[...], preferred_element_type=jnp.float32)
    h1 = x_ref[...] + _dot(merged, wo_ref[...])
    h_ref[...] = h1
    hn = _rms(h1, g2_ref[...]).astype(_MXU_DTYPE)
    hn_ref[...] = hn
    qq = jnp.dot(hn, wq_ref[...], preferred_element_type=jnp.float32).astype(_MXU_DTYPE)
    groups = tm // LANES
    first = (pl.program_id(0) % (_TOK_CHUNK // tm)) * groups
    for hp in range(2 * PEER_HEADS):
        tile = _dot_nt(sk_ref[hp], qq[:, hp * PEER_HALF:(hp + 1) * PEER_HALF])
        for g in range(groups):
            sub_ref[hp, pl.ds(first + g, N_KEYS, stride=8), :] = tile[:, g * LANES:(g + 1) * LANES]


def _merge_call(x2, ma, gb, yb, wb, wo, g2, wq, sk, tm):
    t, d = x2.shape
    assert t % _TOK_CHUNK == 0 and _TOK_CHUNK % tm == 0
    full = lambda a: pl.BlockSpec(a.shape, lambda i: (0,) * a.ndim)
    row = lambda w: pl.BlockSpec((tm, w), lambda i: (i, 0))
    steps = _TOK_CHUNK // tm
    sub_shape = (2 * PEER_HEADS, t // _TOK_CHUNK, N_KEYS * 8, LANES)
    return pl.pallas_call(
        _merge_kernel,
        grid=(t // tm,),
        in_specs=[row(d), row(d), row(d), row(512), full(wb), full(wo), full(g2), full(wq), full(sk)],
        out_specs=[row(d), row(d),
                   pl.BlockSpec((2 * PEER_HEADS, None, N_KEYS * 8, LANES), lambda i: (0, i // steps, 0, 0))],
        out_shape=[jax.ShapeDtypeStruct((t, d), jnp.float32), jax.ShapeDtypeStruct((t, d), _MXU_DTYPE),
                   jax.ShapeDtypeStruct(sub_shape, jnp.float32)],
        compiler_params=pltpu.CompilerParams(dimension_semantics=("arbitrary",), vmem_limit_bytes=_VMEM_LIMIT),
        name="merge",
    )(x2, ma, gb, yb, wb, wo, g2, wq, sk)


def _insert_sorted(vals, pays, x, xp, first=0):
    for lvl in range(first, len(vals)):
        c = x > vals[lvl]
        vals[lvl], x = jnp.where(c, x, vals[lvl]), jnp.where(c, vals[lvl], x)
        pays[lvl], xp = jnp.where(c, xp, pays[lvl]), jnp.where(c, pays[lvl], xp)


_PEER_PAIRS = [(a, b) for a in range(PEER_TOPK) for b in range(PEER_TOPK // (a + 1))]
_KEYS_PER_STEP = 4


def _peer_topk_kernel(sub_ref, i_ref, j_ref, g_ref):
    kk = PEER_TOPK
    neg = jnp.full((8, LANES), -jnp.inf, jnp.float32)
    zero = jnp.zeros((8, LANES), jnp.int32)

    def sorted_top(hp):
        def step(k4, carry):
            vals, idxs = list(carry[0]), list(carry[1])
            for u in range(_KEYS_PER_STEP):
                key = k4 * _KEYS_PER_STEP + u
                x = sub_ref[hp, pl.ds(pl.multiple_of(key * 8, 8), 8), :]
                _insert_sorted(vals, idxs, x, zero + key)
            return tuple(vals), tuple(idxs)

        return lax.fori_loop(0, N_KEYS // _KEYS_PER_STEP, step, ((neg,) * kk, (zero,) * kk))

    def head(h, carry):
        s1, i1 = sorted_top(2 * h)
        s2, i2 = sorted_top(2 * h + 1)
        best, expert = [neg] * kk, [zero] * kk
        for a, b in _PEER_PAIRS:
            _insert_sorted(best, expert, s1[a] + s2[b], i1[a] * N_KEYS + i2[b], first=(a + 1) * (b + 1) - 1)
        e = [jnp.exp(v - best[0]) for v in best]
        denom = e[0]
        for v in e[1:]:
            denom = denom + v
        for k in range(kk):
            i_ref[h * kk + k] = expert[k] >> 7
            j_ref[h * kk + k] = expert[k] & (N_KEYS - 1)
            g_ref[h * kk + k] = e[k] / denom
        return carry

    lax.fori_loop(0, PEER_HEADS, head, 0)


def _peer_topk_call(sub):
    hp, chunks, rows, _ = sub.shape
    hk = PEER_HEADS * PEER_TOPK
    out = pl.BlockSpec((hk, 8, LANES), lambda i: (0, i, 0))
    osd = lambda dt: jax.ShapeDtypeStruct((hk, chunks * 8, LANES), dt)
    return pl.pallas_call(
        _peer_topk_kernel,
        grid=(chunks,),
        in_specs=[pl.BlockSpec((hp, None, rows, LANES), lambda i: (0, i, 0, 0))],
        out_specs=[out, out, out],
        out_shape=[osd(jnp.int32), osd(jnp.int32), osd(jnp.float32)],
        compiler_params=pltpu.CompilerParams(dimension_semantics=("arbitrary",), vmem_limit_bytes=_VMEM_LIMIT),
        name="peer_topk",
    )(sub)


def _peer_coef_kernel(i_ref, j_ref, g_ref, m_ref):
    tt = i_ref.shape[0]
    rid = lax.broadcasted_iota(jnp.int32, (N_KEYS, LANES), 0)

    def body(t, carry):
        irow = i_ref[pl.ds(t, 1), :]
        jrow = j_ref[pl.ds(t, 1), :]
        grow = g_ref[pl.ds(t, 1), :]
        rt = jnp.where(rid == irow, grow, 0.0)
        ct = jnp.where(rid == jrow, 1.0, 0.0)
        mt = _dot_nt(rt, ct)
        for e in range(N_KEYS // _PEER_IB):
            m_ref[e, t] = mt[e * _PEER_IB:(e + 1) * _PEER_IB, :]
        return carry

    lax.fori_loop(0, tt, body, 0, unroll=32)


def _peer_coef_call(it, jt, gt, tt):
    t, hk = it.shape
    row = pl.BlockSpec((tt, hk), lambda i: (i, 0))
    nb = N_KEYS // _PEER_IB
    return pl.pallas_call(
        _peer_coef_kernel,
        grid=(t // tt,),
        in_specs=[row, row, row],
        out_specs=pl.BlockSpec((nb, tt, _PEER_IB, N_KEYS), lambda i: (0, i, 0, 0)),
        out_shape=jax.ShapeDtypeStruct((nb, t, _PEER_IB, N_KEYS), jnp.float32),
        compiler_params=pltpu.CompilerParams(dimension_semantics=("arbitrary",), vmem_limit_bytes=_VMEM_LIMIT),
        name="peer_coef",
    )(it, jt, gt)


def _peer_dense_kernel(hn_ref, h_ref, ut_ref, v_ref, m_ref, o_ref, *, ib):
    e = pl.program_id(1)

    @pl.when(e == 0)
    def _():
        o_ref[...] = h_ref[...]

    tm = hn_ref.shape[0]
    a = jax.nn.gelu(jnp.dot(hn_ref[...], ut_ref[...], preferred_element_type=jnp.float32))
    c = jnp.concatenate([a[:, (b * ib + ii) * N_KEYS:(b * ib + ii + 1) * N_KEYS] * m_ref[b, pl.ds(ii, tm, stride=ib), :]
                         for b in range(m_ref.shape[0]) for ii in range(ib)], axis=1)
    o_ref[...] += jnp.dot(c.astype(_MXU_DTYPE), v_ref[...], preferred_element_type=jnp.float32)


def _peer_dense_call(hn, h1, ut, v, m3, tm, ib, nb):
    t, d = hn.shape
    te = nb * ib * N_KEYS
    kern = functools.partial(_peer_dense_kernel, ib=ib)
    return pl.pallas_call(
        kern,
        grid=(t // tm, N_KEYS // (nb * ib)),
        in_specs=[pl.BlockSpec((tm, d), lambda ti, e: (ti, 0)),
                  pl.BlockSpec((tm, d), lambda ti, e: (ti, 0)),
                  pl.BlockSpec((d, te), lambda ti, e: (0, e)),
                  pl.BlockSpec((te, d), lambda ti, e: (e, 0)),
                  pl.BlockSpec((nb, tm * ib, N_KEYS), lambda ti, e: (e, ti, 0))],
        out_specs=pl.BlockSpec((tm, d), lambda ti, e: (ti, 0)),
        out_shape=jax.ShapeDtypeStruct((t, d), jnp.float32),
        compiler_params=pltpu.CompilerParams(dimension_semantics=("arbitrary", "arbitrary"),
                                             vmem_limit_bytes=_VMEM_LIMIT),
        name="peer_dense",
    )(hn, h1, ut, v, m3)


def _rope_inv(rot_dim, period):
    half = rot_dim // 2
    inv = jnp.power(jnp.float32(ROPE_THETA), -jnp.arange(half, dtype=jnp.float32) * 2.0 / rot_dim)
    lane = np.arange(LANES)
    in_rot = (lane % period) < rot_dim
    pat = jnp.where(jnp.asarray(in_rot), inv[jnp.asarray(lane % period % half)], 0.0)
    return pat.reshape(1, LANES).astype(jnp.float32)


def _pack_w_in(w):
    d = w.shape[0]
    z = lambda n: jnp.zeros((d, n), w.dtype)
    segs = [w[:, 0:1536 + KV_LATENT]]
    for h in range(IDX_HEADS):
        segs += [w[:, 1792 + h * IDX_DIM:1792 + (h + 1) * IDX_DIM], z(LANES - IDX_DIM)]
    segs += [w[:, 2048:2116], z(LANES - 68)]
    segs += [w[:, 2116:]]
    out = jnp.concatenate(segs, axis=1)
    assert out.shape[1] == _W_COLS
    return out.astype(_MXU_DTYPE)


def kernel(x, positions, norm1_g, w_in, v_norm_g, v_norm_b, spatial_w, spatial_b, kv_norm_g, w_uk, w_uv,
           q_norm_g, k_norm_g, w_a_out, w_b_out, w_o, norm2_g, peer_wq, peer_subkeys, peer_u, peer_v):
    bsz, s, d = x.shape
    t = bsz * s
    depth = w_in.shape[0]
    mx = _MXU_DTYPE
    n_sel = min(TOPK_MAX, s // 4)
    tm = min(256, t)
    tk = min(256, s // 4)
    inv = _rope_inv(ROT_DIM, LANES) + jnp.roll(_rope_inv(IDX_ROT, LANES), ROT_DIM, axis=1)
    pos = positions.reshape(t, 1).astype(jnp.float32)
    r2 = lambda a: a.reshape(1, -1)
    h = x.reshape(t, d)
    for l in range(depth):
        ma, gb, q, k, v, qi, ki, wi = _inproj_call(
            h, pos, r2(norm1_g[l]), _pack_w_in(w_in[l]), r2(v_norm_g[l]), r2(v_norm_b[l]), spatial_w[l],
            spatial_b[l].T, r2(kv_norm_g[l]), w_uk[l].astype(mx), w_uv[l].astype(mx), r2(q_norm_g[l]),
            r2(k_norm_g[l]), w_a_out[l].astype(mx), inv, tm)
        b3 = lambda a: a.reshape(bsz, s, a.shape[-1])
        kT = b3(k).transpose(0, 2, 1)
        kiT = b3(ki).transpose(0, 2, 1)
        kn = wi[:, _KN_LANE].reshape(bsz, 1, s)
        yb = _dsa_call(b3(q), b3(qi), b3(wi), kT, kiT, b3(v), kn, n_sel, tk).reshape(t, 512)
        sk = peer_subkeys[l].reshape(2 * PEER_HEADS, N_KEYS, PEER_HALF).astype(mx)
        h1, hn, sub = _merge_call(h, ma, gb, yb, w_b_out[l].astype(mx), w_o[l].astype(mx), r2(norm2_g[l]),
                                  peer_wq[l].astype(mx), sk, min(2 * tm, t))
        hk = PEER_HEADS * PEER_TOPK
        ei, ej, eg = (a.reshape(hk, t).T for a in _peer_topk_call(sub))
        m4 = _peer_coef_call(ei, ej, eg, min(32, t))
        m3 = m4.reshape(N_KEYS // _PEER_IB, t * _PEER_IB, N_KEYS)
        h = _peer_dense_call(hn, h1, peer_u[l].T.astype(mx), peer_v[l].astype(mx), m3, min(512, t), _PEER_IB, 2)
    return h.reshape(bsz, s, d)
```

```python
import functools
import math

import jax
import jax.numpy as jnp
import numpy as np
from jax import lax
from jax.experimental import pallas as pl
from jax.experimental.pallas import tpu as pltpu

EPS = 1e-6
ROPE_THETA = 500000.0
CHUNK = 128
A_GROUPS = 4
A_GROUP_DIM = 128
N_HEADS = 4
HEAD_DIM = 128
KV_LATENT = 256
ROT_DIM = HEAD_DIM // 4
IDX_HEADS = 4
IDX_DIM = 64
IDX_ROT = IDX_DIM // 4
TOPK_MAX = 256
Q_BLOCK = 128
PEER_HEADS = 8
PEER_HALF = 128
N_KEYS = 128
PEER_TOPK = 16
LANES = 128
_PEER_IB = 8

_MXU_DTYPE = jnp.bfloat16
_INT_MIN = -2147483648
_KEY_LOWEST_FINITE = -2139095040
_LIST_DEPTH = 12
_MIN_DENOM = 2.0 ** -100
_KN_LANE = IDX_DIM + IDX_HEADS
_NEG_BIG = -1e30
_VMEM_LIMIT = 56 * 1024 * 1024
_V_EXT = 2 * HEAD_DIM


def _dot(a, b):
    return jnp.dot(a.astype(_MXU_DTYPE), b.astype(_MXU_DTYPE), preferred_element_type=jnp.float32)


def _dot_nt(a, b):
    return lax.dot_general(a.astype(_MXU_DTYPE), b.astype(_MXU_DTYPE), (((1,), (1,)), ((), ())),
                           preferred_element_type=jnp.float32)


def _rms(x, g):
    return x * lax.rsqrt(jnp.mean(x * x, axis=-1, keepdims=True) + EPS) * g


def _rope(x, cos_t, sin_lo, sin_hi, half):
    n = x.shape[-1]
    x_up = pltpu.roll(x, n - half, 1)
    x_dn = pltpu.roll(x, half, 1)
    return x * cos_t + x_up * sin_lo + x_dn * sin_hi


def _rope_tables(pos, inv):
    ang = pos * inv
    c = jnp.cos(ang)
    s = jnp.sin(ang)
    lane = lax.broadcasted_iota(jnp.int32, ang.shape, 1)

    def tables(c, s, rot):
        half = rot // 2
        return (jnp.where(lane < rot, c, 1.0), jnp.where(lane < half, -s, 0.0),
                jnp.where((lane >= half) & (lane < rot), s, 0.0))

    back = LANES - ROT_DIM
    return tables(c, s, ROT_DIM), tables(pltpu.roll(c, back, 1), pltpu.roll(s, back, 1), IDX_ROT)


_OFF_U, _OFF_V, _OFF_Q, _OFF_C, _OFF_QI, _OFF_KI, _OFF_G = 0, 512, 1024, 1536, 1792, 2304, 2432
_W_COLS = 2432 + 2048


def _inproj_kernel(x_ref, pos_ref, g1_ref, w_ref, vg_ref, vb_ref, ws_ref, bs_ref, kvg_ref, wuk_ref, wuv_ref,
                   qg_ref, kg_ref, wa_ref, inv_ref,
                   ma_ref, gb_ref, q_ref, k_ref, v_ref, qi_ref, ki_ref, wi_ref):
    tm = x_ref.shape[0]
    x = x_ref[...]
    xn = _rms(x, g1_ref[...]).astype(_MXU_DTYPE)

    def proj(off, width):
        return jnp.dot(xn, w_ref[:, off:off + width], preferred_element_type=jnp.float32)

    u = jax.nn.gelu(proj(_OFF_U, 512))
    v = jax.nn.gelu(proj(_OFF_V, 512))
    mu = jnp.mean(v, axis=-1, keepdims=True)
    vc = v - mu
    v = vc * lax.rsqrt(jnp.mean(vc * vc, axis=-1, keepdims=True) + EPS) * vg_ref[...] + vb_ref[...]
    v = v.astype(_MXU_DTYPE)
    row = lax.broadcasted_iota(jnp.int32, (CHUNK, CHUNK), 0)
    col = lax.broadcasted_iota(jnp.int32, (CHUNK, CHUNK), 1)
    z_chunks = []
    for c in range(tm // CHUNK):
        zg = []
        for g in range(A_GROUPS):
            wt = jnp.where(row >= col, ws_ref[g], 0.0).astype(_MXU_DTYPE)
            vcg = v[c * CHUNK:(c + 1) * CHUNK, g * A_GROUP_DIM:(g + 1) * A_GROUP_DIM]
            zg.append(jnp.dot(wt, vcg, preferred_element_type=jnp.float32) + bs_ref[:, g:g + 1])
        z_chunks.append(jnp.concatenate(zg, axis=1))
    z = jnp.concatenate(z_chunks, axis=0) if len(z_chunks) > 1 else z_chunks[0]
    ya = u * z
    gate_a = jax.nn.sigmoid(proj(_OFF_G, 1024))
    ma_ref[...] = gate_a * _dot(ya, wa_ref[...])
    gb_ref[...] = jax.nn.sigmoid(proj(_OFF_G + 1024, 1024))

    pos = pos_ref[...]
    (cq, sq_lo, sq_hi), (ci, si_lo, si_hi) = _rope_tables(pos, inv_ref[...])

    q = proj(_OFF_Q, 512)
    qg = qg_ref[...]
    qh = [_rms(q[:, h * HEAD_DIM:(h + 1) * HEAD_DIM], qg) for h in range(N_HEADS)]
    qh = [_rope(t, cq, sq_lo, sq_hi, ROT_DIM // 2) for t in qh]
    q_ref[...] = (jnp.concatenate(qh, axis=1) * (HEAD_DIM ** -0.5 * math.log2(math.e))).astype(q_ref.dtype)

    c_n = _rms(proj(_OFF_C, KV_LATENT), kvg_ref[...]).astype(_MXU_DTYPE)
    kk = _rms(jnp.dot(c_n, wuk_ref[...], preferred_element_type=jnp.float32), kg_ref[...])
    k_out = _rope(kk, cq, sq_lo, sq_hi, ROT_DIM // 2).astype(k_ref.dtype)
    k_ref[...] = k_out
    k32 = k_out.astype(jnp.float32)
    k_norm2 = jnp.sum(k32 * k32, axis=-1, keepdims=True)
    vv = jnp.dot(c_n, wuv_ref[...], preferred_element_type=jnp.float32)
    ones_col = jnp.where(lax.broadcasted_iota(jnp.int32, vv.shape, 1) == 0, 1.0, 0.0)
    v_ref[...] = jnp.concatenate([vv, ones_col], axis=1).astype(v_ref.dtype)

    qi = proj(_OFF_QI, 512)
    qis = [_rope(qi[:, h * LANES:(h + 1) * LANES], ci, si_lo, si_hi, IDX_ROT // 2) for h in range(IDX_HEADS)]
    qi_ref[...] = (jnp.concatenate(qis, axis=1) * (IDX_DIM ** -0.5)).astype(qi_ref.dtype)
    kw = proj(_OFF_KI, LANES)
    lane = lax.broadcasted_iota(jnp.int32, kw.shape, 1)
    ki = _rope(jnp.where(lane < IDX_DIM, kw, 0.0), ci, si_lo, si_hi, IDX_ROT // 2)
    ki_ref[...] = ki.astype(ki_ref.dtype)
    wi_ref[...] = jnp.where(lane == _KN_LANE, k_norm2, kw * (IDX_HEADS ** -0.5))


def _inproj_call(x2, pos, g1, w_pack, vg, vb, ws, bs_t, kvg, wuk, wuv, qg, kg, wa, inv, tm):
    t, d = x2.shape
    full = lambda a: pl.BlockSpec(a.shape, lambda i: (0,) * a.ndim)
    row = lambda w: pl.BlockSpec((tm, w), lambda i: (i, 0))
    f32, mx = jnp.float32, _MXU_DTYPE
    outs = [(1024, f32), (1024, f32), (512, mx), (128, mx), (_V_EXT, mx), (512, mx), (128, mx), (128, f32)]
    return pl.pallas_call(
        _inproj_kernel,
        grid=(t // tm,),
        in_specs=[row(d), row(1), full(g1), full(w_pack), full(vg), full(vb), full(ws), full(bs_t), full(kvg),
                  full(wuk), full(wuv), full(qg), full(kg), full(wa), full(inv)],
        out_specs=[row(w) for w, _ in outs],
        out_shape=[jax.ShapeDtypeStruct((t, w), dt) for w, dt in outs],
        compiler_params=pltpu.CompilerParams(dimension_semantics=("arbitrary",), vmem_limit_bytes=_VMEM_LIMIT),
        name="inproj",
    )(x2, pos, g1, w_pack, vg, vb, ws, bs_t, kvg, wuk, wuv, qg, kg, wa, inv)


def _dsa_kernel(q_ref, qi_ref, wi_ref, kT_ref, kiT_ref, v_ref, kn_ref, o_ref,
                sc_ref, list_ref, listT_ref, thr_ref, need_ref, wb_ref, qa_ref, shift_ref, p_ref, m_ref, acc_ref,
                *, tk, n_sel):
    blk = pl.program_id(1)
    nq = Q_BLOCK
    ck = 2 * tk
    n_pairs = (blk * nq + nq + ck - 1) // ck

    wi = wi_ref[...]
    for h in range(IDX_HEADS):
        wb_ref[h] = jnp.broadcast_to(wi[:, IDX_DIM + h:IDX_DIM + h + 1], (nq, tk))

    cq = 2 * ck
    n_quads = (n_pairs + 1) // 2
    tiles_per_pair = ck // LANES
    tiles_per_quad = cq // LANES
    depth = _LIST_DEPTH
    qpos = blk * nq + lax.broadcasted_iota(jnp.int32, (nq, tk), 0)
    lane_pos = lax.broadcasted_iota(jnp.int32, (nq, tk), 1)

    def score_quad(qd):
        for sub in range(cq // tk):
            off = pl.multiple_of(qd * cq + sub * tk, tk)
            kt = kiT_ref[:, pl.ds(off, tk)]
            sc = None
            for h in range(IDX_HEADS):
                lg = jnp.dot(qi_ref[:, h * LANES:(h + 1) * LANES], kt, preferred_element_type=jnp.float32)
                term = wb_ref[h] * jnp.maximum(lg, 0.0)
                sc = term if sc is None else sc + term
            sc = jnp.where(off + lane_pos <= qpos, sc, -jnp.inf)
            for c in range(tk // LANES):
                sc_ref[qd * tiles_per_quad + sub * (tk // LANES) + c] = sc[:, c * LANES:(c + 1) * LANES]

    def insert_quad(qd):
        for g in range(nq // 8):
            rows = pl.ds(g * 8, 8)
            lists = [list_ref[i, rows, :] for i in range(depth)]
            for c in range(tiles_per_quad):
                x = sc_ref[qd * tiles_per_quad + c, rows, :]
                for i in range(depth):
                    lists[i], x = jnp.maximum(lists[i], x), jnp.minimum(lists[i], x)
            for i in range(depth):
                list_ref[i, rows, :] = lists[i]

    def score_step(qd, carry):
        score_quad(qd)
        insert_quad(qd - 1)
        return carry

    list_ref[...] = jnp.full(list_ref.shape, -jnp.inf, jnp.float32)
    score_quad(0)
    lax.fori_loop(1, n_quads, score_step, 0)
    insert_quad(n_quads - 1)


    def to_key(x):
        bits = pltpu.bitcast(x, jnp.int32)
        return bits ^ ((bits >> 31) & 0x7FFFFFFF)

    def to_score(key):
        return pltpu.bitcast(key ^ ((key >> 31) & 0x7FFFFFFF), jnp.float32)

    def kth_largest(count_ge, shape):
        zero = jnp.zeros(shape, jnp.int32)
        base = jnp.where(count_ge(zero) >= n_sel, zero, _INT_MIN)

        def bit_step(b, base):
            cand = base | jnp.left_shift(jnp.int32(1), 30 - b)
            return jnp.where(count_ge(cand) >= n_sel, cand, base)

        thr = lax.fori_loop(0, 31, bit_step, base)
        return jnp.maximum(thr, _KEY_LOWEST_FINITE)

    for i in range(depth):
        listT_ref[i] = to_key(list_ref[i].T)

    def count_lists(cand_row):
        tot = None
        for i in range(depth):
            hit = (listT_ref[i] >= cand_row).astype(jnp.int32)
            tot = hit if tot is None else tot + hit
        return jnp.sum(tot, axis=0, keepdims=True)

    thr_row = kth_largest(count_lists, (1, nq))
    need_row = n_sel - count_lists(thr_row + 1)
    thr_ref[...] = to_score(jnp.broadcast_to(thr_row, (nq, nq))).T[:, 0:1]
    need_ref[...] = jnp.broadcast_to(need_row, (nq, nq)).astype(jnp.float32).T[:, 0:1]
    overflow = jnp.max(jnp.where(listT_ref[depth - 1] > thr_row, 1, 0))

    @pl.when(overflow > 0)
    def _():
        def count_all(cand):
            cb = jnp.broadcast_to(cand, (nq, LANES))

            def body(j, cnt):
                for c in range(tiles_per_pair):
                    cnt = cnt + (to_key(sc_ref[j * tiles_per_pair + c]) >= cb).astype(jnp.int32)
                return cnt

            cnt = lax.fori_loop(0, n_pairs, body, jnp.zeros((nq, LANES), jnp.int32))
            return jnp.sum(cnt, axis=1, keepdims=True)

        thr_col = kth_largest(count_all, (nq, 1))
        thr_ref[...] = to_score(thr_col)
        need_ref[...] = (n_sel - count_all(thr_col + 1)).astype(jnp.float32)

    thr = jnp.broadcast_to(thr_ref[...], (nq, tk))
    need = need_ref[...]

    tri = (lax.broadcasted_iota(jnp.int32, (tk, tk), 0) <= lax.broadcasted_iota(jnp.int32, (tk, tk), 1))
    tri = jnp.where(tri, 1.0, 0.0).astype(_MXU_DTYPE)

    def selected(j, sub, room):
        t0 = (j * ck + sub * tk) // LANES
        sc = jnp.concatenate([sc_ref[t0 + c] for c in range(tk // LANES)], axis=1)
        eq = sc == thr
        pre = jnp.dot(jnp.where(eq, 1.0, 0.0).astype(_MXU_DTYPE), tri, preferred_element_type=jnp.float32)
        return (sc > thr) | (eq & (pre <= room)), room - pre[:, tk - 1:tk]

    q32 = q_ref[...].astype(jnp.float32)
    qn2 = None
    for h in range(N_HEADS):
        qh = q32[:, h * HEAD_DIM:(h + 1) * HEAD_DIM]
        n2 = jnp.sum(qh * qh, axis=1, keepdims=True)
        qn2 = n2 if qn2 is None else jnp.maximum(qn2, n2)
    kn2 = jnp.max(kn_ref[...], axis=1, keepdims=True)
    neg_bound = jnp.broadcast_to(-jnp.sqrt(qn2 * kn2), (nq, tk))
    eye = (lax.broadcasted_iota(jnp.int32, (nq, nq), 0) == lax.broadcasted_iota(jnp.int32, (nq, nq), 1))
    eye = jnp.where(eye, 1.0, 0.0).astype(_MXU_DTYPE)
    for h in range(N_HEADS):
        qa_ref[h] = jnp.concatenate([q_ref[:, h * HEAD_DIM:(h + 1) * HEAD_DIM], eye], axis=1)
    acc_ref[...] = jnp.zeros(acc_ref.shape, jnp.float32)

    last = n_quads - 1

    def stage_mask(qd, room):
        for sub in range(cq // tk):
            sel, room = selected(2 * qd, sub, room)
            shift_ref[qd % 2, :, sub * tk:(sub + 1) * tk] = jnp.where(sel, neg_bound, _NEG_BIG).astype(_MXU_DTYPE)
        return room

    def stage_weights(qd):
        rhs = jnp.concatenate([kT_ref[:, pl.ds(pl.multiple_of(qd * cq, cq), cq)], shift_ref[qd % 2]], axis=0)
        for h in range(N_HEADS):
            p_ref[qd % 2, h] = jnp.exp2(jnp.dot(qa_ref[h], rhs, preferred_element_type=jnp.float32)).astype(_MXU_DTYPE)

    def stage_values(qd):
        vt = v_ref[pl.ds(pl.multiple_of(qd * cq, cq), cq), :]
        for h in range(N_HEADS):
            acc_ref[h] += jnp.dot(p_ref[qd % 2, h], vt, preferred_element_type=jnp.float32)

    def shifted_step(j, room):
        stage_values(j - 1)
        stage_weights(j)
        return stage_mask(jnp.minimum(j + 1, last), room)

    room = stage_mask(0, need)
    stage_weights(0)
    room = stage_mask(jnp.minimum(1, last), room)
    lax.fori_loop(1, n_quads, shifted_step, room)
    stage_values(last)
    denom = acc_ref[0][:, HEAD_DIM:HEAD_DIM + 1]
    for h in range(1, N_HEADS):
        denom = jnp.minimum(denom, acc_ref[h][:, HEAD_DIM:HEAD_DIM + 1])
    l_min = jnp.min(denom)

    @pl.when(l_min < _MIN_DENOM)
    def _():
        m_ref[...] = jnp.full(m_ref.shape, _NEG_BIG, jnp.float32)
        acc_ref[...] = jnp.zeros(acc_ref.shape, jnp.float32)

        def online_pair(j, room):
            off = pl.multiple_of(j * ck, ck)
            biases = []
            for sub in range(2):
                sel, room = selected(j, sub, room)
                biases.append(jnp.where(sel, 0.0, _NEG_BIG))
            bias = jnp.concatenate(biases, axis=1)
            kt = kT_ref[:, pl.ds(off, ck)]
            vt = v_ref[pl.ds(off, ck), :]
            for h in range(N_HEADS):
                s = jnp.dot(q_ref[:, h * HEAD_DIM:(h + 1) * HEAD_DIM], kt, preferred_element_type=jnp.float32) + bias
                m_old = m_ref[h]
                m_new = jnp.maximum(m_old, jnp.max(s, axis=1, keepdims=True))
                alpha = jnp.exp2(m_old - m_new)
                p = jnp.exp2(s - jnp.concatenate([m_new] * (ck // LANES), axis=1))
                acc_ref[h] = (jnp.concatenate([alpha] * (_V_EXT // LANES), axis=1) * acc_ref[h]
                              + jnp.dot(p.astype(_MXU_DTYPE), vt, preferred_element_type=jnp.float32))
                m_ref[h] = m_new
            return room

        lax.fori_loop(0, n_pairs, online_pair, need)

    ys = []
    for h in range(N_HEADS):
        a = acc_ref[h]
        ys.append(a[:, :HEAD_DIM] / a[:, HEAD_DIM:HEAD_DIM + 1])
    o_ref[...] = jnp.concatenate(ys, axis=1).astype(o_ref.dtype)


def _dsa_call(q, qi, wi, kT, kiT, v, kn, n_sel, tk):
    b, s, _ = q.shape
    nblk = s // Q_BLOCK
    assert s % (4 * tk) == 0 and tk % Q_BLOCK == 0
    qspec = lambda w: pl.BlockSpec((None, Q_BLOCK, w), lambda bi, i: (bi, i, 0))
    kern = functools.partial(_dsa_kernel, tk=tk, n_sel=n_sel)
    return pl.pallas_call(
        kern,
        grid=(b, nblk),
        in_specs=[qspec(512), qspec(512), qspec(128),
                  pl.BlockSpec((None, HEAD_DIM, s), lambda bi, i: (bi, 0, 0)),
                  pl.BlockSpec((None, LANES, s), lambda bi, i: (bi, 0, 0)),
                  pl.BlockSpec((None, s, _V_EXT), lambda bi, i: (bi, 0, 0)),
                  pl.BlockSpec((None, 1, s), lambda bi, i: (bi, 0, 0))],
        out_specs=qspec(512),
        out_shape=jax.ShapeDtypeStruct((b, s, 512), _MXU_DTYPE),
        scratch_shapes=[pltpu.VMEM((s // LANES, Q_BLOCK, LANES), jnp.float32),
                        pltpu.VMEM((_LIST_DEPTH, Q_BLOCK, LANES), jnp.float32),
                        pltpu.VMEM((_LIST_DEPTH, LANES, Q_BLOCK), jnp.int32),
                        pltpu.VMEM((Q_BLOCK, 1), jnp.float32),
                        pltpu.VMEM((Q_BLOCK, 1), jnp.float32),
                        pltpu.VMEM((IDX_HEADS, Q_BLOCK, tk), jnp.float32),
                        pltpu.VMEM((N_HEADS, Q_BLOCK, HEAD_DIM + Q_BLOCK), _MXU_DTYPE),
                        pltpu.VMEM((2, Q_BLOCK, 4 * tk), _MXU_DTYPE),
                        pltpu.VMEM((2, N_HEADS, Q_BLOCK, 4 * tk), _MXU_DTYPE),
                        pltpu.VMEM((N_HEADS, Q_BLOCK, LANES), jnp.float32),
                        pltpu.VMEM((N_HEADS, Q_BLOCK, _V_EXT), jnp.float32)],
        compiler_params=pltpu.CompilerParams(dimension_semantics=("arbitrary", "arbitrary"),
                                             vmem_limit_bytes=_VMEM_LIMIT),
        name="dsa",
    )(q, qi, wi, kT, kiT, v, kn)


_TOK_CHUNK = 8 * LANES


def _merge_kernel(x_ref, ma_ref, gb_ref, yb_ref, wb_ref, wo_ref, g2_ref, wq_ref, sk_ref, h_ref, hn_ref, sub_ref):
    tm = x_ref.shape[0]
    merged = ma_ref[...] + gb_ref[...] * jnp.dot(yb_ref[...], wb_ref[...], preferred_element_type=jnp.float32)
    h1 = x_ref[...] + _dot(merged, wo_ref[...])
    h_ref[...] = h1
    hn = _rms(h1, g2_ref[...]).astype(_MXU_DTYPE)
    hn_ref[...] = hn
    qq = jnp.dot(hn, wq_ref[...], preferred_element_type=jnp.float32).astype(_MXU_DTYPE)
    groups = tm // LANES
    first = (pl.program_id(0) % (_TOK_CHUNK // tm)) * groups
    for hp in range(2 * PEER_HEADS):
        tile = _dot_nt(sk_ref[hp], qq[:, hp * PEER_HALF:(hp + 1) * PEER_HALF])
        for g in range(groups):
            sub_ref[hp, pl.ds(first + g, N_KEYS, stride=8), :] = tile[:, g * LANES:(g + 1) * LANES]


def _merge_call(x2, ma, gb, yb, wb, wo, g2, wq, sk, tm):
    t, d = x2.shape
    assert t % _TOK_CHUNK == 0 and _TOK_CHUNK % tm == 0
    full = lambda a: pl.BlockSpec(a.shape, lambda i: (0,) * a.ndim)
    row = lambda w: pl.BlockSpec((tm, w), lambda i: (i, 0))
    steps = _TOK_CHUNK // tm
    sub_shape = (2 * PEER_HEADS, t // _TOK_CHUNK, N_KEYS * 8, LANES)
    return pl.pallas_call(
        _merge_kernel,
        grid=(t // tm,),
        in_specs=[row(d), row(d), row(d), row(512), full(wb), full(wo), full(g2), full(wq), full(sk)],
        out_specs=[row(d), row(d),
                   pl.BlockSpec((2 * PEER_HEADS, None, N_KEYS * 8, LANES), lambda i: (0, i // steps, 0, 0))],
        out_shape=[jax.ShapeDtypeStruct((t, d), jnp.float32), jax.ShapeDtypeStruct((t, d), _MXU_DTYPE),
                   jax.ShapeDtypeStruct(sub_shape, jnp.float32)],
        compiler_params=pltpu.CompilerParams(dimension_semantics=("arbitrary",), vmem_limit_bytes=_VMEM_LIMIT),
        name="merge",
    )(x2, ma, gb, yb, wb, wo, g2, wq, sk)


def _insert_sorted(vals, pays, x, xp, first=0):
    for lvl in range(first, len(vals)):
        c = x > vals[lvl]
        vals[lvl], x = jnp.where(c, x, vals[lvl]), jnp.where(c, vals[lvl], x)
        pays[lvl], xp = jnp.where(c, xp, pays[lvl]), jnp.where(c, pays[lvl], xp)


_PEER_PAIRS = [(a, b) for a in range(PEER_TOPK) for b in range(PEER_TOPK // (a + 1))]
_KEYS_PER_STEP = 8


def _peer_topk_kernel(sub_ref, i_ref, j_ref, g_ref):
    kk = PEER_TOPK
    neg = jnp.full((8, LANES), -jnp.inf, jnp.float32)
    zero = jnp.zeros((8, LANES), jnp.int32)

    def sorted_top(hp):
        def step(k4, carry):
            vals, idxs = list(carry[0]), list(carry[1])
            for u in range(_KEYS_PER_STEP):
                key = k4 * _KEYS_PER_STEP + u
                x = sub_ref[hp, pl.ds(pl.multiple_of(key * 8, 8), 8), :]
                _insert_sorted(vals, idxs, x, zero + key)
            return tuple(vals), tuple(idxs)

        return lax.fori_loop(0, N_KEYS // _KEYS_PER_STEP, step, ((neg,) * kk, (zero,) * kk))

    def head(h, carry):
        s1, i1 = sorted_top(2 * h)
        s2, i2 = sorted_top(2 * h + 1)
        best, expert = [neg] * kk, [zero] * kk
        for a, b in _PEER_PAIRS:
            _insert_sorted(best, expert, s1[a] + s2[b], i1[a] * N_KEYS + i2[b], first=(a + 1) * (b + 1) - 1)
        e = [jnp.exp(v - best[0]) for v in best]
        denom = e[0]
        for v in e[1:]:
            denom = denom + v
        for k in range(kk):
            i_ref[h * kk + k] = expert[k] >> 7
            j_ref[h * kk + k] = expert[k] & (N_KEYS - 1)
            g_ref[h * kk + k] = e[k] / denom
        return carry

    lax.fori_loop(0, PEER_HEADS, head, 0)


def _peer_topk_call(sub):
    hp, chunks, rows, _ = sub.shape
    hk = PEER_HEADS * PEER_TOPK
    out = pl.BlockSpec((hk, 8, LANES), lambda i: (0, i, 0))
    osd = lambda dt: jax.ShapeDtypeStruct((hk, chunks * 8, LANES), dt)
    return pl.pallas_call(
        _peer_topk_kernel,
        grid=(chunks,),
        in_specs=[pl.BlockSpec((hp, None, rows, LANES), lambda i: (0, i, 0, 0))],
        out_specs=[out, out, out],
        out_shape=[osd(jnp.int32), osd(jnp.int32), osd(jnp.float32)],
        compiler_params=pltpu.CompilerParams(dimension_semantics=("arbitrary",), vmem_limit_bytes=_VMEM_LIMIT),
        name="peer_topk",
    )(sub)


def _peer_coef_kernel(i_ref, j_ref, g_ref, m_ref):
    tt = i_ref.shape[0]
    rid = lax.broadcasted_iota(jnp.int32, (N_KEYS, LANES), 0)

    def body(t, carry):
        irow = i_ref[pl.ds(t, 1), :]
        jrow = j_ref[pl.ds(t, 1), :]
        grow = g_ref[pl.ds(t, 1), :]
        rt = jnp.where(rid == irow, grow, 0.0)
        ct = jnp.where(rid == jrow, 1.0, 0.0)
        mt = _dot_nt(rt, ct)
        for e in range(N_KEYS // _PEER_IB):
            m_ref[e, t] = mt[e * _PEER_IB:(e + 1) * _PEER_IB, :]
        return carry

    lax.fori_loop(0, tt, body, 0, unroll=32)


def _peer_coef_call(it, jt, gt, tt):
    t, hk = it.shape
    row = pl.BlockSpec((tt, hk), lambda i: (i, 0))
    nb = N_KEYS // _PEER_IB
    return pl.pallas_call(
        _peer_coef_kernel,
        grid=(t // tt,),
        in_specs=[row, row, row],
        out_specs=pl.BlockSpec((nb, tt, _PEER_IB, N_KEYS), lambda i: (0, i, 0, 0)),
        out_shape=jax.ShapeDtypeStruct((nb, t, _PEER_IB, N_KEYS), jnp.float32),
        compiler_params=pltpu.CompilerParams(dimension_semantics=("arbitrary",), vmem_limit_bytes=_VMEM_LIMIT),
        name="peer_coef",
    )(it, jt, gt)


def _peer_dense_kernel(hn_ref, h_ref, ut_ref, v_ref, m_ref, o_ref, *, ib):
    e = pl.program_id(1)

    @pl.when(e == 0)
    def _():
        o_ref[...] = h_ref[...]

    tm = hn_ref.shape[0]
    a = jax.nn.gelu(jnp.dot(hn_ref[...], ut_ref[...], preferred_element_type=jnp.float32))
    c = jnp.concatenate([a[:, (b * ib + ii) * N_KEYS:(b * ib + ii + 1) * N_KEYS] * m_ref[b, pl.ds(ii, tm, stride=ib), :]
                         for b in range(m_ref.shape[0]) for ii in range(ib)], axis=1)
    o_ref[...] += jnp.dot(c.astype(_MXU_DTYPE), v_ref[...], preferred_element_type=jnp.float32)


def _peer_dense_call(hn, h1, ut, v, m3, tm, ib, nb):
    t, d = hn.shape
    te = nb * ib * N_KEYS
    kern = functools.partial(_peer_dense_kernel, ib=ib)
    return pl.pallas_call(
        kern,
        grid=(t // tm, N_KEYS // (nb * ib)),
        in_specs=[pl.BlockSpec((tm, d), lambda ti, e: (ti, 0)),
                  pl.BlockSpec((tm, d), lambda ti, e: (ti, 0)),
                  pl.BlockSpec((d, te), lambda ti, e: (0, e)),
                  pl.BlockSpec((te, d), lambda ti, e: (e, 0)),
                  pl.BlockSpec((nb, tm * ib, N_KEYS), lambda ti, e: (e, ti, 0))],
        out_specs=pl.BlockSpec((tm, d), lambda ti, e: (ti, 0)),
        out_shape=jax.ShapeDtypeStruct((t, d), jnp.float32),
        compiler_params=pltpu.CompilerParams(dimension_semantics=("arbitrary", "arbitrary"),
                                             vmem_limit_bytes=_VMEM_LIMIT),
        name="peer_dense",
    )(hn, h1, ut, v, m3)


def _rope_inv(rot_dim, period):
    half = rot_dim // 2
    inv = jnp.power(jnp.float32(ROPE_THETA), -jnp.arange(half, dtype=jnp.float32) * 2.0 / rot_dim)
    lane = np.arange(LANES)
    in_rot = (lane % period) < rot_dim
    pat = jnp.where(jnp.asarray(in_rot), inv[jnp.asarray(lane % period % half)], 0.0)
    return pat.reshape(1, LANES).astype(jnp.float32)


def _pack_w_in(w):
    d = w.shape[0]
    z = lambda n: jnp.zeros((d, n), w.dtype)
    segs = [w[:, 0:1536 + KV_LATENT]]
    for h in range(IDX_HEADS):
        segs += [w[:, 1792 + h * IDX_DIM:1792 + (h + 1) * IDX_DIM], z(LANES - IDX_DIM)]
    segs += [w[:, 2048:2116], z(LANES - 68)]
    segs += [w[:, 2116:]]
    out = jnp.concatenate(segs, axis=1)
    assert out.shape[1] == _W_COLS
    return out.astype(_MXU_DTYPE)


def kernel(x, positions, norm1_g, w_in, v_norm_g, v_norm_b, spatial_w, spatial_b, kv_norm_g, w_uk, w_uv,
           q_norm_g, k_norm_g, w_a_out, w_b_out, w_o, norm2_g, peer_wq, peer_subkeys, peer_u, peer_v):
    bsz, s, d = x.shape
    t = bsz * s
    depth = w_in.shape[0]
    mx = _MXU_DTYPE
    n_sel = min(TOPK_MAX, s // 4)
    tm = min(256, t)
    tk = min(256, s // 4)
    inv = _rope_inv(ROT_DIM, LANES) + jnp.roll(_rope_inv(IDX_ROT, LANES), ROT_DIM, axis=1)
    pos = positions.reshape(t, 1).astype(jnp.float32)
    r2 = lambda a: a.reshape(1, -1)
    h = x.reshape(t, d)
    for l in range(depth):
        ma, gb, q, k, v, qi, ki, wi = _inproj_call(
            h, pos, r2(norm1_g[l]), _pack_w_in(w_in[l]), r2(v_norm_g[l]), r2(v_norm_b[l]), spatial_w[l],
            spatial_b[l].T, r2(kv_norm_g[l]), w_uk[l].astype(mx), w_uv[l].astype(mx), r2(q_norm_g[l]),
            r2(k_norm_g[l]), w_a_out[l].astype(mx), inv, tm)
        b3 = lambda a: a.reshape(bsz, s, a.shape[-1])
        kT = b3(k).transpose(0, 2, 1)
        kiT = b3(ki).transpose(0, 2, 1)
        kn = wi[:, _KN_LANE].reshape(bsz, 1, s)
        yb = _dsa_call(b3(q), b3(qi), b3(wi), kT, kiT, b3(v), kn, n_sel, tk).reshape(t, 512)
        sk = peer_subkeys[l].reshape(2 * PEER_HEADS, N_KEYS, PEER_HALF).astype(mx)
        h1, hn, sub = _merge_call(h, ma, gb, yb, w_b_out[l].astype(mx), w_o[l].astype(mx), r2(norm2_g[l]),
                                  peer_wq[l].astype(mx), sk, min(2 * tm, t))
        hk = PEER_HEADS * PEER_TOPK
        ei, ej, eg = (a.reshape(hk, t).T for a in _peer_topk_call(sub))
        m4 = _peer_coef_call(ei, ej, eg, min(32, t))
        m3 = m4.reshape(N_KEYS // _PEER_IB, t * _PEER_IB, N_KEYS)
        h = _peer_dense_call(hn, h1, peer_u[l].T.astype(mx), peer_v[l].astype(mx), m3, min(512, t), _PEER_IB, 2)
    return h.reshape(bsz, s, d)
```

```python
import functools
import math

import jax
import jax.numpy as jnp
import numpy as np
from jax import lax
from jax.experimental import pallas as pl
from jax.experimental.pallas import tpu as pltpu

EPS = 1e-6
ROPE_THETA = 500000.0
CHUNK = 128
A_GROUPS = 4
A_GROUP_DIM = 128
N_HEADS = 4
HEAD_DIM = 128
KV_LATENT = 256
ROT_DIM = HEAD_DIM // 4
IDX_HEADS = 4
IDX_DIM = 64
IDX_ROT = IDX_DIM // 4
TOPK_MAX = 256
Q_BLOCK = 128
PEER_HEADS = 8
PEER_HALF = 128
N_KEYS = 128
PEER_TOPK = 16
LANES = 128
_PEER_IB = 8

_MXU_DTYPE = jnp.bfloat16
_INT_MIN = -2147483648
_KEY_LOWEST_FINITE = -2139095040
_LIST_DEPTH = 12
_MIN_DENOM = 2.0 ** -100
_KN_LANE = IDX_DIM + IDX_HEADS
_NEG_BIG = -1e30
_VMEM_LIMIT = 56 * 1024 * 1024
_V_EXT = 2 * HEAD_DIM


def _dot(a, b):
    return jnp.dot(a.astype(_MXU_DTYPE), b.astype(_MXU_DTYPE), preferred_element_type=jnp.float32)


def _dot_nt(a, b):
    return lax.dot_general(a.astype(_MXU_DTYPE), b.astype(_MXU_DTYPE), (((1,), (1,)), ((), ())),
                           preferred_element_type=jnp.float32)


def _rms(x, g):
    return x * lax.rsqrt(jnp.mean(x * x, axis=-1, keepdims=True) + EPS) * g


def _rope(x, cos_t, sin_lo, sin_hi, half):
    n = x.shape[-1]
    x_up = pltpu.roll(x, n - half, 1)
    x_dn = pltpu.roll(x, half, 1)
    return x * cos_t + x_up * sin_lo + x_dn * sin_hi


def _rope_tables(pos, inv):
    ang = pos * inv
    c = jnp.cos(ang)
    s = jnp.sin(ang)
    lane = lax.broadcasted_iota(jnp.int32, ang.shape, 1)

    def tables(c, s, rot):
        half = rot // 2
        return (jnp.where(lane < rot, c, 1.0), jnp.where(lane < half, -s, 0.0),
                jnp.where((lane >= half) & (lane < rot), s, 0.0))

    back = LANES - ROT_DIM
    return tables(c, s, ROT_DIM), tables(pltpu.roll(c, back, 1), pltpu.roll(s, back, 1), IDX_ROT)


_OFF_U, _OFF_V, _OFF_Q, _OFF_C, _OFF_QI, _OFF_KI, _OFF_G = 0, 512, 1024, 1536, 1792, 2304, 2432
_W_COLS = 2432 + 2048


def _inproj_kernel(x_ref, pos_ref, g1_ref, w_ref, vg_ref, vb_ref, ws_ref, bs_ref, kvg_ref, wuk_ref, wuv_ref,
                   qg_ref, kg_ref, wa_ref, inv_ref,
                   ma_ref, gb_ref, q_ref, k_ref, v_ref, qi_ref, ki_ref, wi_ref):
    tm = x_ref.shape[0]
    x = x_ref[...]
    xn = _rms(x, g1_ref[...]).astype(_MXU_DTYPE)

    def proj(off, width):
        return jnp.dot(xn, w_ref[:, off:off + width], preferred_element_type=jnp.float32)

    u = jax.nn.gelu(proj(_OFF_U, 512))
    v = jax.nn.gelu(proj(_OFF_V, 512))
    mu = jnp.mean(v, axis=-1, keepdims=True)
    vc = v - mu
    v = vc * lax.rsqrt(jnp.mean(vc * vc, axis=-1, keepdims=True) + EPS) * vg_ref[...] + vb_ref[...]
    v = v.astype(_MXU_DTYPE)
    row = lax.broadcasted_iota(jnp.int32, (CHUNK, CHUNK), 0)
    col = lax.broadcasted_iota(jnp.int32, (CHUNK, CHUNK), 1)
    z_chunks = []
    for c in range(tm // CHUNK):
        zg = []
        for g in range(A_GROUPS):
            wt = jnp.where(row >= col, ws_ref[g], 0.0).astype(_MXU_DTYPE)
            vcg = v[c * CHUNK:(c + 1) * CHUNK, g * A_GROUP_DIM:(g + 1) * A_GROUP_DIM]
            zg.append(jnp.dot(wt, vcg, preferred_element_type=jnp.float32) + bs_ref[:, g:g + 1])
        z_chunks.append(jnp.concatenate(zg, axis=1))
    z = jnp.concatenate(z_chunks, axis=0) if len(z_chunks) > 1 else z_chunks[0]
    ya = u * z
    gate_a = jax.nn.sigmoid(proj(_OFF_G, 1024))
    ma_ref[...] = gate_a * _dot(ya, wa_ref[...])
    gb_ref[...] = jax.nn.sigmoid(proj(_OFF_G + 1024, 1024))

    pos = pos_ref[...]
    (cq, sq_lo, sq_hi), (ci, si_lo, si_hi) = _rope_tables(pos, inv_ref[...])

    q = proj(_OFF_Q, 512)
    qg = qg_ref[...]
    qh = [_rms(q[:, h * HEAD_DIM:(h + 1) * HEAD_DIM], qg) for h in range(N_HEADS)]
    qh = [_rope(t, cq, sq_lo, sq_hi, ROT_DIM // 2) for t in qh]
    q_ref[...] = (jnp.concatenate(qh, axis=1) * (HEAD_DIM ** -0.5 * math.log2(math.e))).astype(q_ref.dtype)

    c_n = _rms(proj(_OFF_C, KV_LATENT), kvg_ref[...]).astype(_MXU_DTYPE)
    kk = _rms(jnp.dot(c_n, wuk_ref[...], preferred_element_type=jnp.float32), kg_ref[...])
    k_out = _rope(kk, cq, sq_lo, sq_hi, ROT_DIM // 2).astype(k_ref.dtype)
    k_ref[...] = k_out
    k32 = k_out.astype(jnp.float32)
    k_norm2 = jnp.sum(k32 * k32, axis=-1, keepdims=True)
    vv = jnp.dot(c_n, wuv_ref[...], preferred_element_type=jnp.float32)
    ones_col = jnp.where(lax.broadcasted_iota(jnp.int32, vv.shape, 1) == 0, 1.0, 0.0)
    v_ref[...] = jnp.concatenate([vv, ones_col], axis=1).astype(v_ref.dtype)

    qi = proj(_OFF_QI, 512)
    qis = [_rope(qi[:, h * LANES:(h + 1) * LANES], ci, si_lo, si_hi, IDX_ROT // 2) for h in range(IDX_HEADS)]
    qi_ref[...] = (jnp.concatenate(qis, axis=1) * (IDX_DIM ** -0.5)).astype(qi_ref.dtype)
    kw = proj(_OFF_KI, LANES)
    lane = lax.broadcasted_iota(jnp.int32, kw.shape, 1)
    ki = _rope(jnp.where(lane < IDX_DIM, kw, 0.0), ci, si_lo, si_hi, IDX_ROT // 2)
    ki_ref[...] = ki.astype(ki_ref.dtype)
    wi_ref[...] = jnp.where(lane == _KN_LANE, k_norm2, kw * (IDX_HEADS ** -0.5))


def _inproj_call(x2, pos, g1, w_pack, vg, vb, ws, bs_t, kvg, wuk, wuv, qg, kg, wa, inv, tm):
    t, d = x2.shape
    full = lambda a: pl.BlockSpec(a.shape, lambda i: (0,) * a.ndim)
    row = lambda w: pl.BlockSpec((tm, w), lambda i: (i, 0))
    f32, mx = jnp.float32, _MXU_DTYPE
    outs = [(1024, f32), (1024, f32), (512, mx), (128, mx), (_V_EXT, mx), (512, mx), (128, mx), (128, f32)]
    return pl.pallas_call(
        _inproj_kernel,
        grid=(t // tm,),
        in_specs=[row(d), row(1), full(g1), full(w_pack), full(vg), full(vb), full(ws), full(bs_t), full(kvg),
                  full(wuk), full(wuv), full(qg), full(kg), full(wa), full(inv)],
        out_specs=[row(w) for w, _ in outs],
        out_shape=[jax.ShapeDtypeStruct((t, w), dt) for w, dt in outs],
        compiler_params=pltpu.CompilerParams(dimension_semantics=("arbitrary",), vmem_limit_bytes=_VMEM_LIMIT),
        name="inproj",
    )(x2, pos, g1, w_pack, vg, vb, ws, bs_t, kvg, wuk, wuv, qg, kg, wa, inv)


def _dsa_kernel(q_ref, qi_ref, wi_ref, kT_ref, kiT_ref, v_ref, kn_ref, o_ref,
                sc_ref, list_ref, listT_ref, thr_ref, need_ref, wb_ref, qa_ref, shift_ref, p_ref, m_ref, acc_ref,
                *, tk, n_sel):
    blk = pl.program_id(1)
    nq = Q_BLOCK
    ck = 2 * tk
    n_pairs = (blk * nq + nq + ck - 1) // ck

    wi = wi_ref[...]
    for h in range(IDX_HEADS):
        wb_ref[h] = jnp.broadcast_to(wi[:, IDX_DIM + h:IDX_DIM + h + 1], (nq, tk))

    cq = 2 * ck
    n_quads = (n_pairs + 1) // 2
    tiles_per_pair = ck // LANES
    tiles_per_quad = cq // LANES
    depth = _LIST_DEPTH
    qpos = blk * nq + lax.broadcasted_iota(jnp.int32, (nq, tk), 0)
    lane_pos = lax.broadcasted_iota(jnp.int32, (nq, tk), 1)

    def score_quad(qd):
        for sub in range(cq // tk):
            off = pl.multiple_of(qd * cq + sub * tk, tk)
            kt = kiT_ref[:, pl.ds(off, tk)]
            sc = None
            for h in range(IDX_HEADS):
                lg = jnp.dot(qi_ref[:, h * LANES:(h + 1) * LANES], kt, preferred_element_type=jnp.float32)
                term = wb_ref[h] * jnp.maximum(lg, 0.0)
                sc = term if sc is None else sc + term
            sc = jnp.where(off + lane_pos <= qpos, sc, -jnp.inf)
            for c in range(tk // LANES):
                sc_ref[qd * tiles_per_quad + sub * (tk // LANES) + c] = sc[:, c * LANES:(c + 1) * LANES]

    def insert_quad(qd):
        for g in range(nq // 8):
            rows = pl.ds(g * 8, 8)
            lists = [list_ref[i, rows, :] for i in range(depth)]
            for c in range(tiles_per_quad):
                x = sc_ref[qd * tiles_per_quad + c, rows, :]
                for i in range(depth):
                    lists[i], x = jnp.maximum(lists[i], x), jnp.minimum(lists[i], x)
            for i in range(depth):
                list_ref[i, rows, :] = lists[i]

    def score_step(qd, carry):
        score_quad(qd)
        insert_quad(qd - 1)
        return carry

    list_ref[...] = jnp.full(list_ref.shape, -jnp.inf, jnp.float32)
    score_quad(0)
    lax.fori_loop(1, n_quads, score_step, 0)
    insert_quad(n_quads - 1)


    def to_key(x):
        bits = pltpu.bitcast(x, jnp.int32)
        return bits ^ ((bits >> 31) & 0x7FFFFFFF)

    def to_score(key):
        return pltpu.bitcast(key ^ ((key >> 31) & 0x7FFFFFFF), jnp.float32)

    def kth_largest(count_ge, shape):
        zero = jnp.zeros(shape, jnp.int32)
        base = jnp.where(count_ge(zero) >= n_sel, zero, _INT_MIN)

        def bit_step(b, base):
            cand = base | jnp.left_shift(jnp.int32(1), 30 - b)
            return jnp.where(count_ge(cand) >= n_sel, cand, base)

        thr = lax.fori_loop(0, 31, bit_step, base)
        return jnp.maximum(thr, _KEY_LOWEST_FINITE)

    for i in range(depth):
        listT_ref[i] = to_key(list_ref[i].T)

    def pick(bits, leaves):
        if not bits:
            return leaves[0]
        half = len(leaves) // 2
        return jnp.where(bits[0], pick(bits[1:], leaves[half:]), pick(bits[1:], leaves[:half]))

    def count_lists(cand_row):
        rows = LANES // 8
        tot = None
        for g in range(rows):
            sl = pl.ds(g * 8, 8)
            levels = [listT_ref[i, sl, :] for i in range(depth)]
            floor = jnp.full((8, nq), _INT_MIN, jnp.int32)
            bits, cnt = [], None
            n_steps = depth.bit_length()
            for k in range(n_steps):
                step = 1 << (n_steps - 1 - k)
                leaves = []
                for r_hi in range(1 << k):
                    idx = r_hi * 2 * step + step - 1
                    leaves.append(levels[idx] if idx < depth else floor)
                hit = pick(bits, leaves) >= cand_row
                bits.append(hit)
                add = jnp.where(hit, step, 0)
                cnt = add if cnt is None else cnt + add
            tot = cnt if tot is None else tot + cnt
        return jnp.sum(tot, axis=0, keepdims=True)

    thr_row = kth_largest(count_lists, (1, nq))
    need_row = n_sel - count_lists(thr_row + 1)
    thr_ref[...] = to_score(jnp.broadcast_to(thr_row, (nq, nq))).T[:, 0:1]
    need_ref[...] = jnp.broadcast_to(need_row, (nq, nq)).astype(jnp.float32).T[:, 0:1]
    overflow = jnp.max(jnp.where(listT_ref[depth - 1] > thr_row, 1, 0))

    @pl.when(overflow > 0)
    def _():
        def count_all(cand):
            cb = jnp.broadcast_to(cand, (nq, LANES))

            def body(j, cnt):
                for c in range(tiles_per_pair):
                    cnt = cnt + (to_key(sc_ref[j * tiles_per_pair + c]) >= cb).astype(jnp.int32)
                return cnt

            cnt = lax.fori_loop(0, n_pairs, body, jnp.zeros((nq, LANES), jnp.int32))
            return jnp.sum(cnt, axis=1, keepdims=True)

        thr_col = kth_largest(count_all, (nq, 1))
        thr_ref[...] = to_score(thr_col)
        need_ref[...] = (n_sel - count_all(thr_col + 1)).astype(jnp.float32)

    thr = jnp.broadcast_to(thr_ref[...], (nq, tk))
    need = need_ref[...]

    tri = (lax.broadcasted_iota(jnp.int32, (tk, tk), 0) <= lax.broadcasted_iota(jnp.int32, (tk, tk), 1))
    tri = jnp.where(tri, 1.0, 0.0).astype(_MXU_DTYPE)

    def selected(j, sub, room):
        t0 = (j * ck + sub * tk) // LANES
        sc = jnp.concatenate([sc_ref[t0 + c] for c in range(tk // LANES)], axis=1)
        eq = sc == thr
        pre = jnp.dot(jnp.where(eq, 1.0, 0.0).astype(_MXU_DTYPE), tri, preferred_element_type=jnp.float32)
        return (sc > thr) | (eq & (pre <= room)), room - pre[:, tk - 1:tk]

    q32 = q_ref[...].astype(jnp.float32)
    qn2 = None
    for h in range(N_HEADS):
        qh = q32[:, h * HEAD_DIM:(h + 1) * HEAD_DIM]
        n2 = jnp.sum(qh * qh, axis=1, keepdims=True)
        qn2 = n2 if qn2 is None else jnp.maximum(qn2, n2)
    kn2 = jnp.max(kn_ref[...], axis=1, keepdims=True)
    neg_bound = jnp.broadcast_to(-jnp.sqrt(qn2 * kn2), (nq, tk))
    eye = (lax.broadcasted_iota(jnp.int32, (nq, nq), 0) == lax.broadcasted_iota(jnp.int32, (nq, nq), 1))
    eye = jnp.where(eye, 1.0, 0.0).astype(_MXU_DTYPE)
    for h in range(N_HEADS):
        qa_ref[h] = jnp.concatenate([q_ref[:, h * HEAD_DIM:(h + 1) * HEAD_DIM], eye], axis=1)
    acc_ref[...] = jnp.zeros(acc_ref.shape, jnp.float32)

    last = n_quads - 1

    def stage_mask(qd, room):
        for sub in range(cq // tk):
            sel, room = selected(2 * qd, sub, room)
            shift_ref[qd % 2, :, sub * tk:(sub + 1) * tk] = jnp.where(sel, neg_bound, _NEG_BIG).astype(_MXU_DTYPE)
        return room

    def stage_weights(qd):
        rhs = jnp.concatenate([kT_ref[:, pl.ds(pl.multiple_of(qd * cq, cq), cq)], shift_ref[qd % 2]], axis=0)
        for h in range(N_HEADS):
            p_ref[qd % 2, h] = jnp.exp2(jnp.dot(qa_ref[h], rhs, preferred_element_type=jnp.float32)).astype(_MXU_DTYPE)

    def stage_values(qd):
        vt = v_ref[pl.ds(pl.multiple_of(qd * cq, cq), cq), :]
        for h in range(N_HEADS):
            acc_ref[h] += jnp.dot(p_ref[qd % 2, h], vt, preferred_element_type=jnp.float32)

    def shifted_step(j, room):
        stage_values(j - 1)
        stage_weights(j)
        return stage_mask(jnp.minimum(j + 1, last), room)

    room = stage_mask(0, need)
    stage_weights(0)
    room = stage_mask(jnp.minimum(1, last), room)
    lax.fori_loop(1, n_quads, shifted_step, room)
    stage_values(last)
    denom = acc_ref[0][:, HEAD_DIM:HEAD_DIM + 1]
    for h in range(1, N_HEADS):
        denom = jnp.minimum(denom, acc_ref[h][:, HEAD_DIM:HEAD_DIM + 1])
    l_min = jnp.min(denom)

    @pl.when(l_min < _MIN_DENOM)
    def _():
        m_ref[...] = jnp.full(m_ref.shape, _NEG_BIG, jnp.float32)
        acc_ref[...] = jnp.zeros(acc_ref.shape, jnp.float32)

        def online_pair(j, room):
            off = pl.multiple_of(j * ck, ck)
            biases = []
            for sub in range(2):
                sel, room = selected(j, sub, room)
                biases.append(jnp.where(sel, 0.0, _NEG_BIG))
            bias = jnp.concatenate(biases, axis=1)
            kt = kT_ref[:, pl.ds(off, ck)]
            vt = v_ref[pl.ds(off, ck), :]
            for h in range(N_HEADS):
                s = jnp.dot(q_ref[:, h * HEAD_DIM:(h + 1) * HEAD_DIM], kt, preferred_element_type=jnp.float32) + bias
                m_old = m_ref[h]
                m_new = jnp.maximum(m_old, jnp.max(s, axis=1, keepdims=True))
                alpha = jnp.exp2(m_old - m_new)
                p = jnp.exp2(s - jnp.concatenate([m_new] * (ck // LANES), axis=1))
                acc_ref[h] = (jnp.concatenate([alpha] * (_V_EXT // LANES), axis=1) * acc_ref[h]
                              + jnp.dot(p.astype(_MXU_DTYPE), vt, preferred_element_type=jnp.float32))
                m_ref[h] = m_new
            return room

        lax.fori_loop(0, n_pairs, online_pair, need)

    ys = []
    for h in range(N_HEADS):
        a = acc_ref[h]
        ys.append(a[:, :HEAD_DIM] / a[:, HEAD_DIM:HEAD_DIM + 1])
    o_ref[...] = jnp.concatenate(ys, axis=1).astype(o_ref.dtype)


def _dsa_call(q, qi, wi, kT, kiT, v, kn, n_sel, tk):
    b, s, _ = q.shape
    nblk = s // Q_BLOCK
    assert s % (4 * tk) == 0 and tk % Q_BLOCK == 0
    qspec = lambda w: pl.BlockSpec((None, Q_BLOCK, w), lambda bi, i: (bi, i, 0))
    kern = functools.partial(_dsa_kernel, tk=tk, n_sel=n_sel)
    return pl.pallas_call(
        kern,
        grid=(b, nblk),
        in_specs=[qspec(512), qspec(512), qspec(128),
                  pl.BlockSpec((None, HEAD_DIM, s), lambda bi, i: (bi, 0, 0)),
                  pl.BlockSpec((None, LANES, s), lambda bi, i: (bi, 0, 0)),
                  pl.BlockSpec((None, s, _V_EXT), lambda bi, i: (bi, 0, 0)),
                  pl.BlockSpec((None, 1, s), lambda bi, i: (bi, 0, 0))],
        out_specs=qspec(512),
        out_shape=jax.ShapeDtypeStruct((b, s, 512), _MXU_DTYPE),
        scratch_shapes=[pltpu.VMEM((s // LANES, Q_BLOCK, LANES), jnp.float32),
                        pltpu.VMEM((_LIST_DEPTH, Q_BLOCK, LANES), jnp.float32),
                        pltpu.VMEM((_LIST_DEPTH, LANES, Q_BLOCK), jnp.int32),
                        pltpu.VMEM((Q_BLOCK, 1), jnp.float32),
                        pltpu.VMEM((Q_BLOCK, 1), jnp.float32),
                        pltpu.VMEM((IDX_HEADS, Q_BLOCK, tk), jnp.float32),
                        pltpu.VMEM((N_HEADS, Q_BLOCK, HEAD_DIM + Q_BLOCK), _MXU_DTYPE),
                        pltpu.VMEM((2, Q_BLOCK, 4 * tk), _MXU_DTYPE),
                        pltpu.VMEM((2, N_HEADS, Q_BLOCK, 4 * tk), _MXU_DTYPE),
                        pltpu.VMEM((N_HEADS, Q_BLOCK, LANES), jnp.float32),
                        pltpu.VMEM((N_HEADS, Q_BLOCK, _V_EXT), jnp.float32)],
        compiler_params=pltpu.CompilerParams(dimension_semantics=("arbitrary", "arbitrary"),
                                             vmem_limit_bytes=_VMEM_LIMIT),
        name="dsa",
    )(q, qi, wi, kT, kiT, v, kn)


_TOK_CHUNK = 8 * LANES


def _merge_kernel(x_ref, ma_ref, gb_ref, yb_ref, wb_ref, wo_ref, g2_ref, wq_ref, sk_ref, h_ref, hn_ref, sub_ref):
    tm = x_ref.shape[0]
    merged = ma_ref[...] + gb_ref[...] * jnp.dot(yb_ref[...], wb_ref[...], preferred_element_type=jnp.float32)
    h1 = x_ref[...] + _dot(merged, wo_ref[...])
    h_ref[...] = h1
    hn = _rms(h1, g2_ref[...]).astype(_MXU_DTYPE)
    hn_ref[...] = hn
    qq = jnp.dot(hn, wq_ref[...], preferred_element_type=jnp.float32).astype(_MXU_DTYPE)
    groups = tm // LANES
    first = (pl.program_id(0) % (_TOK_CHUNK // tm)) * groups
    for hp in range(2 * PEER_HEADS):
        tile = _dot_nt(sk_ref[hp], qq[:, hp * PEER_HALF:(hp + 1) * PEER_HALF])
        for g in range(groups):
            sub_ref[hp, pl.ds(first + g, N_KEYS, stride=8), :] = tile[:, g * LANES:(g + 1) * LANES]


def _merge_call(x2, ma, gb, yb, wb, wo, g2, wq, sk, tm):
    t, d = x2.shape
    assert t % _TOK_CHUNK == 0 and _TOK_CHUNK % tm == 0
    full = lambda a: pl.BlockSpec(a.shape, lambda i: (0,) * a.ndim)
    row = lambda w: pl.BlockSpec((tm, w), lambda i: (i, 0))
    steps = _TOK_CHUNK // tm
    sub_shape = (2 * PEER_HEADS, t // _TOK_CHUNK, N_KEYS * 8, LANES)
    return pl.pallas_call(
        _merge_kernel,
        grid=(t // tm,),
        in_specs=[row(d), row(d), row(d), row(512), full(wb), full(wo), full(g2), full(wq), full(sk)],
        out_specs=[row(d), row(d),
                   pl.BlockSpec((2 * PEER_HEADS, None, N_KEYS * 8, LANES), lambda i: (0, i // steps, 0, 0))],
        out_shape=[jax.ShapeDtypeStruct((t, d), jnp.float32), jax.ShapeDtypeStruct((t, d), _MXU_DTYPE),
                   jax.ShapeDtypeStruct(sub_shape, jnp.float32)],
        compiler_params=pltpu.CompilerParams(dimension_semantics=("arbitrary",), vmem_limit_bytes=_VMEM_LIMIT),
        name="merge",
    )(x2, ma, gb, yb, wb, wo, g2, wq, sk)


def _insert_sorted(vals, pays, x, xp, first=0):
    for lvl in range(first, len(vals)):
        c = x > vals[lvl]
        vals[lvl], x = jnp.where(c, x, vals[lvl]), jnp.where(c, vals[lvl], x)
        pays[lvl], xp = jnp.where(c, xp, pays[lvl]), jnp.where(c, pays[lvl], xp)


_PEER_PAIRS = [(a, b) for a in range(PEER_TOPK) for b in range(PEER_TOPK // (a + 1))]
_KEYS_PER_STEP = 16


def _peer_topk_kernel(sub_ref, i_ref, j_ref, g_ref):
    kk = PEER_TOPK
    neg = jnp.full((8, LANES), -jnp.inf, jnp.float32)
    zero = jnp.zeros((8, LANES), jnp.int32)

    def sorted_top(hp):
        def step(k4, carry):
            vals, idxs = list(carry[0]), list(carry[1])
            for u in range(_KEYS_PER_STEP):
                key = k4 * _KEYS_PER_STEP + u
                x = sub_ref[hp, pl.ds(pl.multiple_of(key * 8, 8), 8), :]
                _insert_sorted(vals, idxs, x, zero + key)
            return tuple(vals), tuple(idxs)

        return lax.fori_loop(0, N_KEYS // _KEYS_PER_STEP, step, ((neg,) * kk, (zero,) * kk))

    def head(h, carry):
        s1, i1 = sorted_top(2 * h)
        s2, i2 = sorted_top(2 * h + 1)
        best, expert = [neg] * kk, [zero] * kk
        for a, b in _PEER_PAIRS:
            _insert_sorted(best, expert, s1[a] + s2[b], i1[a] * N_KEYS + i2[b], first=(a + 1) * (b + 1) - 1)
        e = [jnp.exp(v - best[0]) for v in best]
        denom = e[0]
        for v in e[1:]:
            denom = denom + v
        for k in range(kk):
            i_ref[h * kk + k] = expert[k] >> 7
            j_ref[h * kk + k] = expert[k] & (N_KEYS - 1)
            g_ref[h * kk + k] = e[k] / denom
        return carry

    lax.fori_loop(0, PEER_HEADS, head, 0)


def _peer_topk_call(sub):
    hp, chunks, rows, _ = sub.shape
    hk = PEER_HEADS * PEER_TOPK
    out = pl.BlockSpec((hk, 8, LANES), lambda i: (0, i, 0))
    osd = lambda dt: jax.ShapeDtypeStruct((hk, chunks * 8, LANES), dt)
    return pl.pallas_call(
        _peer_topk_kernel,
        grid=(chunks,),
        in_specs=[pl.BlockSpec((hp, None, rows, LANES), lambda i: (0, i, 0, 0))],
        out_specs=[out, out, out],
        out_shape=[osd(jnp.int32), osd(jnp.int32), osd(jnp.float32)],
        compiler_params=pltpu.CompilerParams(dimension_semantics=("arbitrary",), vmem_limit_bytes=_VMEM_LIMIT),
        name="peer_topk",
    )(sub)


def _peer_coef_kernel(i_ref, j_ref, g_ref, m_ref):
    tt = i_ref.shape[0]
    rid = lax.broadcasted_iota(jnp.int32, (N_KEYS, LANES), 0)

    def body(t, carry):
        irow = i_ref[pl.ds(t, 1), :]
        jrow = j_ref[pl.ds(t, 1), :]
        grow = g_ref[pl.ds(t, 1), :]
        rt = jnp.where(rid == irow, grow, 0.0)
        ct = jnp.where(rid == jrow, 1.0, 0.0)
        mt = _dot_nt(rt, ct)
        for e in range(N_KEYS // _PEER_IB):
            m_ref[e, t] = mt[e * _PEER_IB:(e + 1) * _PEER_IB, :]
        return carry

    lax.fori_loop(0, tt, body, 0, unroll=32)


def _peer_coef_call(it, jt, gt, tt):
    t, hk = it.shape
    row = pl.BlockSpec((tt, hk), lambda i: (i, 0))
    nb = N_KEYS // _PEER_IB
    return pl.pallas_call(
        _peer_coef_kernel,
        grid=(t // tt,),
        in_specs=[row, row, row],
        out_specs=pl.BlockSpec((nb, tt, _PEER_IB, N_KEYS), lambda i: (0, i, 0, 0)),
        out_shape=jax.ShapeDtypeStruct((nb, t, _PEER_IB, N_KEYS), jnp.float32),
        compiler_params=pltpu.CompilerParams(dimension_semantics=("arbitrary",), vmem_limit_bytes=_VMEM_LIMIT),
        name="peer_coef",
    )(it, jt, gt)


def _peer_dense_kernel(hn_ref, h_ref, ut_ref, v_ref, m_ref, o_ref, *, ib):
    e = pl.program_id(1)

    @pl.when(e == 0)
    def _():
        o_ref[...] = h_ref[...]

    tm = hn_ref.shape[0]
    a = jax.nn.gelu(jnp.dot(hn_ref[...], ut_ref[...], preferred_element_type=jnp.float32))
    c = jnp.concatenate([a[:, (b * ib + ii) * N_KEYS:(b * ib + ii + 1) * N_KEYS] * m_ref[b, pl.ds(ii, tm, stride=ib), :]
                         for b in range(m_ref.shape[0]) for ii in range(ib)], axis=1)
    o_ref[...] += jnp.dot(c.astype(_MXU_DTYPE), v_ref[...], preferred_element_type=jnp.float32)


def _peer_dense_call(hn, h1, ut, v, m3, tm, ib, nb):
    t, d = hn.shape
    te = nb * ib * N_KEYS
    kern = functools.partial(_peer_dense_kernel, ib=ib)
    return pl.pallas_call(
        kern,
        grid=(t // tm, N_KEYS // (nb * ib)),
        in_specs=[pl.BlockSpec((tm, d), lambda ti, e: (ti, 0)),
                  pl.BlockSpec((tm, d), lambda ti, e: (ti, 0)),
                  pl.BlockSpec((d, te), lambda ti, e: (0, e)),
                  pl.BlockSpec((te, d), lambda ti, e: (e, 0)),
                  pl.BlockSpec((nb, tm * ib, N_KEYS), lambda ti, e: (e, ti, 0))],
        out_specs=pl.BlockSpec((tm, d), lambda ti, e: (ti, 0)),
        out_shape=jax.ShapeDtypeStruct((t, d), jnp.float32),
        compiler_params=pltpu.CompilerParams(dimension_semantics=("arbitrary", "arbitrary"),
                                             vmem_limit_bytes=_VMEM_LIMIT),
        name="peer_dense",
    )(hn, h1, ut, v, m3)


def _rope_inv(rot_dim, period):
    half = rot_dim // 2
    inv = jnp.power(jnp.float32(ROPE_THETA), -jnp.arange(half, dtype=jnp.float32) * 2.0 / rot_dim)
    lane = np.arange(LANES)
    in_rot = (lane % period) < rot_dim
    pat = jnp.where(jnp.asarray(in_rot), inv[jnp.asarray(lane % period % half)], 0.0)
    return pat.reshape(1, LANES).astype(jnp.float32)


def _pack_w_in(w):
    d = w.shape[0]
    z = lambda n: jnp.zeros((d, n), w.dtype)
    segs = [w[:, 0:1536 + KV_LATENT]]
    for h in range(IDX_HEADS):
        segs += [w[:, 1792 + h * IDX_DIM:1792 + (h + 1) * IDX_DIM], z(LANES - IDX_DIM)]
    segs += [w[:, 2048:2116], z(LANES - 68)]
    segs += [w[:, 2116:]]
    out = jnp.concatenate(segs, axis=1)
    assert out.shape[1] == _W_COLS
    return out.astype(_MXU_DTYPE)


def kernel(x, positions, norm1_g, w_in, v_norm_g, v_norm_b, spatial_w, spatial_b, kv_norm_g, w_uk, w_uv,
           q_norm_g, k_norm_g, w_a_out, w_b_out, w_o, norm2_g, peer_wq, peer_subkeys, peer_u, peer_v):
    bsz, s, d = x.shape
    t = bsz * s
    depth = w_in.shape[0]
    mx = _MXU_DTYPE
    n_sel = min(TOPK_MAX, s // 4)
    tm = min(256, t)
    tk = min(256, s // 4)
    inv = _rope_inv(ROT_DIM, LANES) + jnp.roll(_rope_inv(IDX_ROT, LANES), ROT_DIM, axis=1)
    pos = positions.reshape(t, 1).astype(jnp.float32)
    r2 = lambda a: a.reshape(1, -1)
    h = x.reshape(t, d)
    for l in range(depth):
        ma, gb, q, k, v, qi, ki, wi = _inproj_call(
            h, pos, r2(norm1_g[l]), _pack_w_in(w_in[l]), r2(v_norm_g[l]), r2(v_norm_b[l]), spatial_w[l],
            spatial_b[l].T, r2(kv_norm_g[l]), w_uk[l].astype(mx), w_uv[l].astype(mx), r2(q_norm_g[l]),
            r2(k_norm_g[l]), w_a_out[l].astype(mx), inv, tm)
        b3 = lambda a: a.reshape(bsz, s, a.shape[-1])
        kT = b3(k).transpose(0, 2, 1)
        kiT = b3(ki).transpose(0, 2, 1)
        kn = wi[:, _KN_LANE].reshape(bsz, 1, s)
        yb = _dsa_call(b3(q), b3(qi), b3(wi), kT, kiT, b3(v), kn, n_sel, tk).reshape(t, 512)
        sk = peer_subkeys[l].reshape(2 * PEER_HEADS, N_KEYS, PEER_HALF).astype(mx)
        h1, hn, sub = _merge_call(h, ma, gb, yb, w_b_out[l].astype(mx), w_o[l].astype(mx), r2(norm2_g[l]),
                                  peer_wq[l].astype(mx), sk, min(2 * tm, t))
        hk = PEER_HEADS * PEER_TOPK
        ei, ej, eg = (a.reshape(hk, t).T for a in _peer_topk_call(sub))
        m4 = _peer_coef_call(ei, ej, eg, min(32, t))
        m3 = m4.reshape(N_KEYS // _PEER_IB, t * _PEER_IB, N_KEYS)
        h = _peer_dense_call(hn, h1, peer_u[l].T.astype(mx), peer_v[l].astype(mx), m3, min(512, t), _PEER_IB, 2)
    return h.reshape(bsz, s, d)
```

```python
import functools
import math

import jax
import jax.numpy as jnp
import numpy as np
from jax import lax
from jax.experimental import pallas as pl
from jax.experimental.pallas import tpu as pltpu

EPS = 1e-6
ROPE_THETA = 500000.0
CHUNK = 128
A_GROUPS = 4
A_GROUP_DIM = 128
N_HEADS = 4
HEAD_DIM = 128
KV_LATENT = 256
ROT_DIM = HEAD_DIM // 4
IDX_HEADS = 4
IDX_DIM = 64
IDX_ROT = IDX_DIM // 4
TOPK_MAX = 256
Q_BLOCK = 128
PEER_HEADS = 8
PEER_HALF = 128
N_KEYS = 128
PEER_TOPK = 16
LANES = 128
_PEER_IB = 8

_MXU_DTYPE = jnp.bfloat16
_INT_MIN = -2147483648
_KEY_LOWEST_FINITE = -2139095040
_LIST_DEPTH = 12
_MIN_DENOM = 2.0 ** -100
_KN_LANE = IDX_DIM + IDX_HEADS
_NEG_BIG = -1e30
_VMEM_LIMIT = 56 * 1024 * 1024
_V_EXT = 2 * HEAD_DIM


def _dot(a, b):
    return jnp.dot(a.astype(_MXU_DTYPE), b.astype(_MXU_DTYPE), preferred_element_type=jnp.float32)


def _dot_nt(a, b):
    return lax.dot_general(a.astype(_MXU_DTYPE), b.astype(_MXU_DTYPE), (((1,), (1,)), ((), ())),
                           preferred_element_type=jnp.float32)


def _rms(x, g):
    return x * lax.rsqrt(jnp.mean(x * x, axis=-1, keepdims=True) + EPS) * g


def _rope(x, cos_t, sin_lo, sin_hi, half):
    n = x.shape[-1]
    x_up = pltpu.roll(x, n - half, 1)
    x_dn = pltpu.roll(x, half, 1)
    return x * cos_t + x_up * sin_lo + x_dn * sin_hi


def _rope_tables(pos, inv):
    ang = pos * inv
    c = jnp.cos(ang)
    s = jnp.sin(ang)
    lane = lax.broadcasted_iota(jnp.int32, ang.shape, 1)

    def tables(c, s, rot):
        half = rot // 2
        return (jnp.where(lane < rot, c, 1.0), jnp.where(lane < half, -s, 0.0),
                jnp.where((lane >= half) & (lane < rot), s, 0.0))

    back = LANES - ROT_DIM
    return tables(c, s, ROT_DIM), tables(pltpu.roll(c, back, 1), pltpu.roll(s, back, 1), IDX_ROT)


_OFF_U, _OFF_V, _OFF_Q, _OFF_C, _OFF_QI, _OFF_KI, _OFF_G = 0, 512, 1024, 1536, 1792, 2304, 2432
_W_COLS = 2432 + 2048


def _inproj_kernel(x_ref, pos_ref, g1_ref, w_ref, vg_ref, vb_ref, ws_ref, bs_ref, kvg_ref, wuk_ref, wuv_ref,
                   qg_ref, kg_ref, wa_ref, inv_ref,
                   ma_ref, gb_ref, q_ref, k_ref, v_ref, qi_ref, ki_ref, wi_ref):
    tm = x_ref.shape[0]
    x = x_ref[...]
    xn = _rms(x, g1_ref[...]).astype(_MXU_DTYPE)

    def proj(off, width):
        return jnp.dot(xn, w_ref[:, off:off + width], preferred_element_type=jnp.float32)

    u = jax.nn.gelu(proj(_OFF_U, 512))
    v = jax.nn.gelu(proj(_OFF_V, 512))
    mu = jnp.mean(v, axis=-1, keepdims=True)
    vc = v - mu
    v = vc * lax.rsqrt(jnp.mean(vc * vc, axis=-1, keepdims=True) + EPS) * vg_ref[...] + vb_ref[...]
    v = v.astype(_MXU_DTYPE)
    row = lax.broadcasted_iota(jnp.int32, (CHUNK, CHUNK), 0)
    col = lax.broadcasted_iota(jnp.int32, (CHUNK, CHUNK), 1)
    z_chunks = []
    for c in range(tm // CHUNK):
        zg = []
        for g in range(A_GROUPS):
            wt = jnp.where(row >= col, ws_ref[g], 0.0).astype(_MXU_DTYPE)
            vcg = v[c * CHUNK:(c + 1) * CHUNK, g * A_GROUP_DIM:(g + 1) * A_GROUP_DIM]
            zg.append(jnp.dot(wt, vcg, preferred_element_type=jnp.float32) + bs_ref[:, g:g + 1])
        z_chunks.append(jnp.concatenate(zg, axis=1))
    z = jnp.concatenate(z_chunks, axis=0) if len(z_chunks) > 1 else z_chunks[0]
    ya = u * z
    gate_a = jax.nn.sigmoid(proj(_OFF_G, 1024))
    ma_ref[...] = gate_a * _dot(ya, wa_ref[...])
    gb_ref[...] = jax.nn.sigmoid(proj(_OFF_G + 1024, 1024))

    pos = pos_ref[...]
    (cq, sq_lo, sq_hi), (ci, si_lo, si_hi) = _rope_tables(pos, inv_ref[...])

    q = proj(_OFF_Q, 512)
    qg = qg_ref[...]
    qh = [_rms(q[:, h * HEAD_DIM:(h + 1) * HEAD_DIM], qg) for h in range(N_HEADS)]
    qh = [_rope(t, cq, sq_lo, sq_hi, ROT_DIM // 2) for t in qh]
    q_ref[...] = (jnp.concatenate(qh, axis=1) * (HEAD_DIM ** -0.5 * math.log2(math.e))).astype(q_ref.dtype)

    c_n = _rms(proj(_OFF_C, KV_LATENT), kvg_ref[...]).astype(_MXU_DTYPE)
    kk = _rms(jnp.dot(c_n, wuk_ref[...], preferred_element_type=jnp.float32), kg_ref[...])
    k_out = _rope(kk, cq, sq_lo, sq_hi, ROT_DIM // 2).astype(k_ref.dtype)
    k_ref[...] = k_out
    k32 = k_out.astype(jnp.float32)
    k_norm2 = jnp.sum(k32 * k32, axis=-1, keepdims=True)
    vv = jnp.dot(c_n, wuv_ref[...], preferred_element_type=jnp.float32)
    ones_col = jnp.where(lax.broadcasted_iota(jnp.int32, vv.shape, 1) == 0, 1.0, 0.0)
    v_ref[...] = jnp.concatenate([vv, ones_col], axis=1).astype(v_ref.dtype)

    qi = proj(_OFF_QI, 512)
    qis = [_rope(qi[:, h * LANES:(h + 1) * LANES], ci, si_lo, si_hi, IDX_ROT // 2) for h in range(IDX_HEADS)]
    qi_ref[...] = (jnp.concatenate(qis, axis=1) * (IDX_DIM ** -0.5)).astype(qi_ref.dtype)
    kw = proj(_OFF_KI, LANES)
    lane = lax.broadcasted_iota(jnp.int32, kw.shape, 1)
    ki = _rope(jnp.where(lane < IDX_DIM, kw, 0.0), ci, si_lo, si_hi, IDX_ROT // 2)
    ki_ref[...] = ki.astype(ki_ref.dtype)
    wi_ref[...] = jnp.where(lane == _KN_LANE, k_norm2, kw * (IDX_HEADS ** -0.5))


def _inproj_call(x2, pos, g1, w_pack, vg, vb, ws, bs_t, kvg, wuk, wuv, qg, kg, wa, inv, tm):
    t, d = x2.shape
    full = lambda a: pl.BlockSpec(a.shape, lambda i: (0,) * a.ndim)
    row = lambda w: pl.BlockSpec((tm, w), lambda i: (i, 0))
    f32, mx = jnp.float32, _MXU_DTYPE
    outs = [(1024, f32), (1024, f32), (512, mx), (128, mx), (_V_EXT, mx), (512, mx), (128, mx), (128, f32)]
    return pl.pallas_call(
        _inproj_kernel,
        grid=(t // tm,),
        in_specs=[row(d), row(1), full(g1), full(w_pack), full(vg), full(vb), full(ws), full(bs_t), full(kvg),
                  full(wuk), full(wuv), full(qg), full(kg), full(wa), full(inv)],
        out_specs=[row(w) for w, _ in outs],
        out_shape=[jax.ShapeDtypeStruct((t, w), dt) for w, dt in outs],
        compiler_params=pltpu.CompilerParams(dimension_semantics=("arbitrary",), vmem_limit_bytes=_VMEM_LIMIT),
        name="inproj",
    )(x2, pos, g1, w_pack, vg, vb, ws, bs_t, kvg, wuk, wuv, qg, kg, wa, inv)


def _dsa_kernel(q_ref, qi_ref, wi_ref, kT_ref, kiT_ref, v_ref, kn_ref, o_ref,
                sc_ref, list_ref, listT_ref, thr_ref, need_ref, wb_ref, qa_ref, shift_ref, p_ref, m_ref, acc_ref,
                *, tk, n_sel):
    blk = pl.program_id(1)
    nq = Q_BLOCK
    ck = 2 * tk
    n_pairs = (blk * nq + nq + ck - 1) // ck

    wi = wi_ref[...]
    for h in range(IDX_HEADS):
        wb_ref[h] = jnp.broadcast_to(wi[:, IDX_DIM + h:IDX_DIM + h + 1], (nq, tk))

    cq = 2 * ck
    n_quads = (n_pairs + 1) // 2
    tiles_per_pair = ck // LANES
    tiles_per_quad = cq // LANES
    depth = _LIST_DEPTH
    qpos = blk * nq + lax.broadcasted_iota(jnp.int32, (nq, tk), 0)
    lane_pos = lax.broadcasted_iota(jnp.int32, (nq, tk), 1)

    def score_quad(qd):
        for sub in range(cq // tk):
            off = pl.multiple_of(qd * cq + sub * tk, tk)
            kt = kiT_ref[:, pl.ds(off, tk)]
            sc = None
            for h in range(IDX_HEADS):
                lg = jnp.dot(qi_ref[:, h * LANES:(h + 1) * LANES], kt, preferred_element_type=jnp.float32)
                term = wb_ref[h] * jnp.maximum(lg, 0.0)
                sc = term if sc is None else sc + term
            sc = jnp.where(off + lane_pos <= qpos, sc, -jnp.inf)
            for c in range(tk // LANES):
                sc_ref[qd * tiles_per_quad + sub * (tk // LANES) + c] = sc[:, c * LANES:(c + 1) * LANES]

    def insert_quad(qd):
        for g in range(nq // 8):
            rows = pl.ds(g * 8, 8)
            lists = [list_ref[i, rows, :] for i in range(depth)]
            for c in range(tiles_per_quad):
                x = sc_ref[qd * tiles_per_quad + c, rows, :]
                for i in range(depth):
                    lists[i], x = jnp.maximum(lists[i], x), jnp.minimum(lists[i], x)
            for i in range(depth):
                list_ref[i, rows, :] = lists[i]

    def score_step(qd, carry):
        score_quad(qd)
        insert_quad(qd - 1)
        return carry

    list_ref[...] = jnp.full(list_ref.shape, -jnp.inf, jnp.float32)
    score_quad(0)
    lax.fori_loop(1, n_quads, score_step, 0)
    insert_quad(n_quads - 1)


    def to_key(x):
        bits = pltpu.bitcast(x, jnp.int32)
        return bits ^ ((bits >> 31) & 0x7FFFFFFF)

    def to_score(key):
        return pltpu.bitcast(key ^ ((key >> 31) & 0x7FFFFFFF), jnp.float32)

    def kth_largest(count_ge, shape):
        zero = jnp.zeros(shape, jnp.int32)
        base = jnp.where(count_ge(zero) >= n_sel, zero, _INT_MIN)

        def bit_step(b, base):
            cand = base | jnp.left_shift(jnp.int32(1), 30 - b)
            return jnp.where(count_ge(cand) >= n_sel, cand, base)

        thr = lax.fori_loop(0, 31, bit_step, base)
        return jnp.maximum(thr, _KEY_LOWEST_FINITE)

    for i in range(depth):
        listT_ref[i] = to_key(list_ref[i].T)

    def pick(bits, leaves):
        if not bits:
            return leaves[0]
        half = len(leaves) // 2
        return jnp.where(bits[0], pick(bits[1:], leaves[half:]), pick(bits[1:], leaves[:half]))

    def count_lists(cand_row):
        rows = LANES // 8
        tot = None
        for g in range(rows):
            sl = pl.ds(g * 8, 8)
            levels = [listT_ref[i, sl, :] for i in range(depth)]
            floor = jnp.full((8, nq), _INT_MIN, jnp.int32)
            bits, cnt = [], None
            n_steps = depth.bit_length()
            for k in range(n_steps):
                step = 1 << (n_steps - 1 - k)
                leaves = []
                for r_hi in range(1 << k):
                    idx = r_hi * 2 * step + step - 1
                    leaves.append(levels[idx] if idx < depth else floor)
                hit = pick(bits, leaves) >= cand_row
                bits.append(hit)
                add = jnp.where(hit, step, 0)
                cnt = add if cnt is None else cnt + add
            tot = cnt if tot is None else tot + cnt
        return jnp.sum(tot, axis=0, keepdims=True)

    thr_row = kth_largest(count_lists, (1, nq))
    need_row = n_sel - count_lists(thr_row + 1)
    thr_ref[...] = to_score(jnp.broadcast_to(thr_row, (nq, nq))).T[:, 0:1]
    need_ref[...] = jnp.broadcast_to(need_row, (nq, nq)).astype(jnp.float32).T[:, 0:1]
    overflow = jnp.max(jnp.where(listT_ref[depth - 1] > thr_row, 1, 0))

    @pl.when(overflow > 0)
    def _():
        def count_all(cand):
            cb = jnp.broadcast_to(cand, (nq, LANES))

            def body(j, cnt):
                for c in range(tiles_per_pair):
                    cnt = cnt + (to_key(sc_ref[j * tiles_per_pair + c]) >= cb).astype(jnp.int32)
                return cnt

            cnt = lax.fori_loop(0, n_pairs, body, jnp.zeros((nq, LANES), jnp.int32))
            return jnp.sum(cnt, axis=1, keepdims=True)

        thr_col = kth_largest(count_all, (nq, 1))
        thr_ref[...] = to_score(thr_col)
        need_ref[...] = (n_sel - count_all(thr_col + 1)).astype(jnp.float32)

    thr = jnp.broadcast_to(thr_ref[...], (nq, tk))
    need = need_ref[...]

    tri = (lax.broadcasted_iota(jnp.int32, (tk, tk), 0) <= lax.broadcasted_iota(jnp.int32, (tk, tk), 1))
    tri = jnp.where(tri, 1.0, 0.0).astype(_MXU_DTYPE)

    def selected(j, sub, room):
        t0 = (j * ck + sub * tk) // LANES
        sc = jnp.concatenate([sc_ref[t0 + c] for c in range(tk // LANES)], axis=1)
        eq = sc == thr
        pre = jnp.dot(jnp.where(eq, 1.0, 0.0).astype(_MXU_DTYPE), tri, preferred_element_type=jnp.float32)
        return (sc > thr) | (eq & (pre <= room)), room - pre[:, tk - 1:tk]

    q32 = q_ref[...].astype(jnp.float32)
    qn2 = None
    for h in range(N_HEADS):
        qh = q32[:, h * HEAD_DIM:(h + 1) * HEAD_DIM]
        n2 = jnp.sum(qh * qh, axis=1, keepdims=True)
        qn2 = n2 if qn2 is None else jnp.maximum(qn2, n2)
    kn2 = jnp.max(kn_ref[...], axis=1, keepdims=True)
    neg_bound = jnp.broadcast_to(-jnp.sqrt(qn2 * kn2), (nq, tk))
    eye = (lax.broadcasted_iota(jnp.int32, (nq, nq), 0) == lax.broadcasted_iota(jnp.int32, (nq, nq), 1))
    eye = jnp.where(eye, 1.0, 0.0).astype(_MXU_DTYPE)
    for h in range(N_HEADS):
        qa_ref[h] = jnp.concatenate([q_ref[:, h * HEAD_DIM:(h + 1) * HEAD_DIM], eye], axis=1)
    acc_ref[...] = jnp.zeros(acc_ref.shape, jnp.float32)

    last = n_quads - 1

    def stage_mask(qd, room):
        for sub in range(cq // tk):
            sel, room = selected(2 * qd, sub, room)
            shift_ref[qd % 2, :, sub * tk:(sub + 1) * tk] = jnp.where(sel, neg_bound, _NEG_BIG).astype(_MXU_DTYPE)
        return room

    def stage_weights(qd):
        rhs = jnp.concatenate([kT_ref[:, pl.ds(pl.multiple_of(qd * cq, cq), cq)], shift_ref[qd % 2]], axis=0)
        for h in range(N_HEADS):
            p_ref[qd % 2, h] = jnp.exp2(jnp.dot(qa_ref[h], rhs, preferred_element_type=jnp.float32)).astype(_MXU_DTYPE)

    def stage_values(qd):
        vt = v_ref[pl.ds(pl.multiple_of(qd * cq, cq), cq), :]
        for h in range(N_HEADS):
            acc_ref[h] += jnp.dot(p_ref[qd % 2, h], vt, preferred_element_type=jnp.float32)

    def shifted_step(j, room):
        stage_values(j - 1)
        stage_weights(j)
        return stage_mask(jnp.minimum(j + 1, last), room)

    room = stage_mask(0, need)
    stage_weights(0)
    room = stage_mask(jnp.minimum(1, last), room)
    lax.fori_loop(1, n_quads, shifted_step, room)
    stage_values(last)
    denom = acc_ref[0][:, HEAD_DIM:HEAD_DIM + 1]
    for h in range(1, N_HEADS):
        denom = jnp.minimum(denom, acc_ref[h][:, HEAD_DIM:HEAD_DIM + 1])
    l_min = jnp.min(denom)

    @pl.when(l_min < _MIN_DENOM)
    def _():
        m_ref[...] = jnp.full(m_ref.shape, _NEG_BIG, jnp.float32)
        acc_ref[...] = jnp.zeros(acc_ref.shape, jnp.float32)

        def online_pair(j, room):
            off = pl.multiple_of(j * ck, ck)
            biases = []
            for sub in range(2):
                sel, room = selected(j, sub, room)
                biases.append(jnp.where(sel, 0.0, _NEG_BIG))
            bias = jnp.concatenate(biases, axis=1)
            kt = kT_ref[:, pl.ds(off, ck)]
            vt = v_ref[pl.ds(off, ck), :]
            for h in range(N_HEADS):
                s = jnp.dot(q_ref[:, h * HEAD_DIM:(h + 1) * HEAD_DIM], kt, preferred_element_type=jnp.float32) + bias
                m_old = m_ref[h]
                m_new = jnp.maximum(m_old, jnp.max(s, axis=1, keepdims=True))
                alpha = jnp.exp2(m_old - m_new)
                p = jnp.exp2(s - jnp.concatenate([m_new] * (ck // LANES), axis=1))
                acc_ref[h] = (jnp.concatenate([alpha] * (_V_EXT // LANES), axis=1) * acc_ref[h]
                              + jnp.dot(p.astype(_MXU_DTYPE), vt, preferred_element_type=jnp.float32))
                m_ref[h] = m_new
            return room

        lax.fori_loop(0, n_pairs, online_pair, need)

    ys = []
    for h in range(N_HEADS):
        a = acc_ref[h]
        ys.append(a[:, :HEAD_DIM] / a[:, HEAD_DIM:HEAD_DIM + 1])
    o_ref[...] = jnp.concatenate(ys, axis=1).astype(o_ref.dtype)


def _dsa_call(q, qi, wi, kT, kiT, v, kn, n_sel, tk):
    b, s, _ = q.shape
    nblk = s // Q_BLOCK
    assert s % (4 * tk) == 0 and tk % Q_BLOCK == 0
    qspec = lambda w: pl.BlockSpec((None, Q_BLOCK, w), lambda bi, i: (bi, i, 0))
    kern = functools.partial(_dsa_kernel, tk=tk, n_sel=n_sel)
    return pl.pallas_call(
        kern,
        grid=(b, nblk),
        in_specs=[qspec(512), qspec(512), qspec(128),
                  pl.BlockSpec((None, HEAD_DIM, s), lambda bi, i: (bi, 0, 0)),
                  pl.BlockSpec((None, LANES, s), lambda bi, i: (bi, 0, 0)),
                  pl.BlockSpec((None, s, _V_EXT), lambda bi, i: (bi, 0, 0)),
                  pl.BlockSpec((None, 1, s), lambda bi, i: (bi, 0, 0))],
        out_specs=qspec(512),
        out_shape=jax.ShapeDtypeStruct((b, s, 512), _MXU_DTYPE),
        scratch_shapes=[pltpu.VMEM((s // LANES, Q_BLOCK, LANES), jnp.float32),
                        pltpu.VMEM((_LIST_DEPTH, Q_BLOCK, LANES), jnp.float32),
                        pltpu.VMEM((_LIST_DEPTH, LANES, Q_BLOCK), jnp.int32),
                        pltpu.VMEM((Q_BLOCK, 1), jnp.float32),
                        pltpu.VMEM((Q_BLOCK, 1), jnp.float32),
                        pltpu.VMEM((IDX_HEADS, Q_BLOCK, tk), jnp.float32),
                        pltpu.VMEM((N_HEADS, Q_BLOCK, HEAD_DIM + Q_BLOCK), _MXU_DTYPE),
                        pltpu.VMEM((2, Q_BLOCK, 4 * tk), _MXU_DTYPE),
                        pltpu.VMEM((2, N_HEADS, Q_BLOCK, 4 * tk), _MXU_DTYPE),
                        pltpu.VMEM((N_HEADS, Q_BLOCK, LANES), jnp.float32),
                        pltpu.VMEM((N_HEADS, Q_BLOCK, _V_EXT), jnp.float32)],
        compiler_params=pltpu.CompilerParams(dimension_semantics=("arbitrary", "arbitrary"),
                                             vmem_limit_bytes=_VMEM_LIMIT),
        name="dsa",
    )(q, qi, wi, kT, kiT, v, kn)


_TOK_CHUNK = 8 * LANES


def _merge_kernel(x_ref, ma_ref, gb_ref, yb_ref, wb_ref, wo_ref, g2_ref, wq_ref, sk_ref, h_ref, hn_ref, sub_ref):
    tm = x_ref.shape[0]
    merged = ma_ref[...] + gb_ref[...] * jnp.dot(yb_ref[...], wb_ref[...], preferred_element_type=jnp.float32)
    h1 = x_ref[...] + _dot(merged, wo_ref[...])
    h_ref[...] = h1
    hn = _rms(h1, g2_ref[...]).astype(_MXU_DTYPE)
    hn_ref[...] = hn
    qq = jnp.dot(hn, wq_ref[...], preferred_element_type=jnp.float32).astype(_MXU_DTYPE)
    groups = tm // LANES
    first = (pl.program_id(0) % (_TOK_CHUNK // tm)) * groups
    for hp in range(2 * PEER_HEADS):
        tile = _dot_nt(sk_ref[hp], qq[:, hp * PEER_HALF:(hp + 1) * PEER_HALF])
        for g in range(groups):
            sub_ref[hp, pl.ds(first + g, N_KEYS, stride=8), :] = tile[:, g * LANES:(g + 1) * LANES]


def _merge_call(x2, ma, gb, yb, wb, wo, g2, wq, sk, tm):
    t, d = x2.shape
    assert t % _TOK_CHUNK == 0 and _TOK_CHUNK % tm == 0
    full = lambda a: pl.BlockSpec(a.shape, lambda i: (0,) * a.ndim)
    row = lambda w: pl.BlockSpec((tm, w), lambda i: (i, 0))
    steps = _TOK_CHUNK // tm
    sub_shape = (2 * PEER_HEADS, t // _TOK_CHUNK, N_KEYS * 8, LANES)
    return pl.pallas_call(
        _merge_kernel,
        grid=(t // tm,),
        in_specs=[row(d), row(d), row(d), row(512), full(wb), full(wo), full(g2), full(wq), full(sk)],
        out_specs=[row(d), row(d),
                   pl.BlockSpec((2 * PEER_HEADS, None, N_KEYS * 8, LANES), lambda i: (0, i // steps, 0, 0))],
        out_shape=[jax.ShapeDtypeStruct((t, d), jnp.float32), jax.ShapeDtypeStruct((t, d), _MXU_DTYPE),
                   jax.ShapeDtypeStruct(sub_shape, jnp.float32)],
        compiler_params=pltpu.CompilerParams(dimension_semantics=("arbitrary",), vmem_limit_bytes=_VMEM_LIMIT),
        name="merge",
    )(x2, ma, gb, yb, wb, wo, g2, wq, sk)


def _insert_sorted(vals, pays, x, xp, first=0):
    for lvl in range(first, len(vals)):
        c = x > vals[lvl]
        vals[lvl], x = jnp.where(c, x, vals[lvl]), jnp.where(c, vals[lvl], x)
        pays[lvl], xp = jnp.where(c, xp, pays[lvl]), jnp.where(c, pays[lvl], xp)


_PEER_PAIRS = [(a, b) for a in range(PEER_TOPK) for b in range(PEER_TOPK // (a + 1))]
_KEYS_PER_STEP = 16


def _peer_topk_kernel(sub_ref, i_ref, j_ref, g_ref):
    kk = PEER_TOPK
    neg = jnp.full((8, LANES), -jnp.inf, jnp.float32)
    zero = jnp.zeros((8, LANES), jnp.int32)

    def sorted_top(hp):
        def step(k4, carry):
            vals, idxs = list(carry[0]), list(carry[1])
            for u in range(_KEYS_PER_STEP):
                key = k4 * _KEYS_PER_STEP + u
                x = sub_ref[hp, pl.ds(pl.multiple_of(key * 8, 8), 8), :]
                _insert_sorted(vals, idxs, x, zero + key)
            return tuple(vals), tuple(idxs)

        return lax.fori_loop(0, N_KEYS // _KEYS_PER_STEP, step, ((neg,) * kk, (zero,) * kk))

    def head(h, carry):
        s1, i1 = sorted_top(2 * h)
        s2, i2 = sorted_top(2 * h + 1)
        best, expert = [neg] * kk, [zero] * kk
        for a, b in _PEER_PAIRS:
            _insert_sorted(best, expert, s1[a] + s2[b], i1[a] * N_KEYS + i2[b], first=(a + 1) * (b + 1) - 1)
        e = [jnp.exp(v - best[0]) for v in best]
        denom = e[0]
        for v in e[1:]:
            denom = denom + v
        for k in range(kk):
            i_ref[h * kk + k] = expert[k] >> 7
            j_ref[h * kk + k] = expert[k] & (N_KEYS - 1)
            g_ref[h * kk + k] = e[k] / denom
        return carry

    lax.fori_loop(0, PEER_HEADS, head, 0)


def _peer_topk_call(sub):
    hp, chunks, rows, _ = sub.shape
    hk = PEER_HEADS * PEER_TOPK
    out = pl.BlockSpec((hk, 8, LANES), lambda i: (0, i, 0))
    osd = lambda dt: jax.ShapeDtypeStruct((hk, chunks * 8, LANES), dt)
    return pl.pallas_call(
        _peer_topk_kernel,
        grid=(chunks,),
        in_specs=[pl.BlockSpec((hp, None, rows, LANES), lambda i: (0, i, 0, 0))],
        out_specs=[out, out, out],
        out_shape=[osd(jnp.int32), osd(jnp.int32), osd(jnp.float32)],
        compiler_params=pltpu.CompilerParams(dimension_semantics=("arbitrary",), vmem_limit_bytes=_VMEM_LIMIT),
        name="peer_topk",
    )(sub)


def _peer_coef_kernel(i_ref, j_ref, g_ref, m_ref):
    tt = i_ref.shape[0]
    rid = lax.broadcasted_iota(jnp.int32, (N_KEYS, LANES), 0)

    def body(t, carry):
        irow = i_ref[pl.ds(t, 1), :]
        jrow = j_ref[pl.ds(t, 1), :]
        grow = g_ref[pl.ds(t, 1), :]
        rt = jnp.where(rid == irow, grow, 0.0)
        ct = jnp.where(rid == jrow, 1.0, 0.0)
        mt = _dot_nt(rt, ct)
        for e in range(N_KEYS // _PEER_IB):
            m_ref[e, t] = mt[e * _PEER_IB:(e + 1) * _PEER_IB, :]
        return carry

    lax.fori_loop(0, tt, body, 0, unroll=True)


def _peer_coef_call(it, jt, gt, tt):
    t, hk = it.shape
    row = pl.BlockSpec((tt, hk), lambda i: (i, 0))
    nb = N_KEYS // _PEER_IB
    return pl.pallas_call(
        _peer_coef_kernel,
        grid=(t // tt,),
        in_specs=[row, row, row],
        out_specs=pl.BlockSpec((nb, tt, _PEER_IB, N_KEYS), lambda i: (0, i, 0, 0)),
        out_shape=jax.ShapeDtypeStruct((nb, t, _PEER_IB, N_KEYS), jnp.float32),
        compiler_params=pltpu.CompilerParams(dimension_semantics=("arbitrary",), vmem_limit_bytes=_VMEM_LIMIT),
        name="peer_coef",
    )(it, jt, gt)


def _peer_dense_kernel(hn_ref, h_ref, ut_ref, v_ref, m_ref, o_ref, *, ib):
    e = pl.program_id(1)

    @pl.when(e == 0)
    def _():
        o_ref[...] = h_ref[...]

    tm = hn_ref.shape[0]
    a = jax.nn.gelu(jnp.dot(hn_ref[...], ut_ref[...], preferred_element_type=jnp.float32))
    c = jnp.concatenate([a[:, (b * ib + ii) * N_KEYS:(b * ib + ii + 1) * N_KEYS] * m_ref[b, pl.ds(ii, tm, stride=ib), :]
                         for b in range(m_ref.shape[0]) for ii in range(ib)], axis=1)
    o_ref[...] += jnp.dot(c.astype(_MXU_DTYPE), v_ref[...], preferred_element_type=jnp.float32)


def _peer_dense_call(hn, h1, ut, v, m3, tm, ib, nb):
    t, d = hn.shape
    te = nb * ib * N_KEYS
    kern = functools.partial(_peer_dense_kernel, ib=ib)
    return pl.pallas_call(
        kern,
        grid=(t // tm, N_KEYS // (nb * ib)),
        in_specs=[pl.BlockSpec((tm, d), lambda ti, e: (ti, 0)),
                  pl.BlockSpec((tm, d), lambda ti, e: (ti, 0)),
                  pl.BlockSpec((d, te), lambda ti, e: (0, e)),
                  pl.BlockSpec((te, d), lambda ti, e: (e, 0)),
                  pl.BlockSpec((nb, tm * ib, N_KEYS), lambda ti, e: (e, ti, 0))],
        out_specs=pl.BlockSpec((tm, d), lambda ti, e: (ti, 0)),
        out_shape=jax.ShapeDtypeStruct((t, d), jnp.float32),
        compiler_params=pltpu.CompilerParams(dimension_semantics=("arbitrary", "arbitrary"),
                                             vmem_limit_bytes=_VMEM_LIMIT),
        name="peer_dense",
    )(hn, h1, ut, v, m3)


def _rope_inv(rot_dim, period):
    half = rot_dim // 2
    inv = jnp.power(jnp.float32(ROPE_THETA), -jnp.arange(half, dtype=jnp.float32) * 2.0 / rot_dim)
    lane = np.arange(LANES)
    in_rot = (lane % period) < rot_dim
    pat = jnp.where(jnp.asarray(in_rot), inv[jnp.asarray(lane % period % half)], 0.0)
    return pat.reshape(1, LANES).astype(jnp.float32)


def _pack_w_in(w):
    d = w.shape[0]
    z = lambda n: jnp.zeros((d, n), w.dtype)
    segs = [w[:, 0:1536 + KV_LATENT]]
    for h in range(IDX_HEADS):
        segs += [w[:, 1792 + h * IDX_DIM:1792 + (h + 1) * IDX_DIM], z(LANES - IDX_DIM)]
    segs += [w[:, 2048:2116], z(LANES - 68)]
    segs += [w[:, 2116:]]
    out = jnp.concatenate(segs, axis=1)
    assert out.shape[1] == _W_COLS
    return out.astype(_MXU_DTYPE)


def kernel(x, positions, norm1_g, w_in, v_norm_g, v_norm_b, spatial_w, spatial_b, kv_norm_g, w_uk, w_uv,
           q_norm_g, k_norm_g, w_a_out, w_b_out, w_o, norm2_g, peer_wq, peer_subkeys, peer_u, peer_v):
    bsz, s, d = x.shape
    t = bsz * s
    depth = w_in.shape[0]
    mx = _MXU_DTYPE
    n_sel = min(TOPK_MAX, s // 4)
    tm = min(256, t)
    tk = min(256, s // 4)
    inv = _rope_inv(ROT_DIM, LANES) + jnp.roll(_rope_inv(IDX_ROT, LANES), ROT_DIM, axis=1)
    pos = positions.reshape(t, 1).astype(jnp.float32)
    r2 = lambda a: a.reshape(1, -1)
    h = x.reshape(t, d)
    for l in range(depth):
        ma, gb, q, k, v, qi, ki, wi = _inproj_call(
            h, pos, r2(norm1_g[l]), _pack_w_in(w_in[l]), r2(v_norm_g[l]), r2(v_norm_b[l]), spatial_w[l],
            spatial_b[l].T, r2(kv_norm_g[l]), w_uk[l].astype(mx), w_uv[l].astype(mx), r2(q_norm_g[l]),
            r2(k_norm_g[l]), w_a_out[l].astype(mx), inv, tm)
        b3 = lambda a: a.reshape(bsz, s, a.shape[-1])
        kT = b3(k).transpose(0, 2, 1)
        kiT = b3(ki).transpose(0, 2, 1)
        kn = wi[:, _KN_LANE].reshape(bsz, 1, s)
        yb = _dsa_call(b3(q), b3(qi), b3(wi), kT, kiT, b3(v), kn, n_sel, tk).reshape(t, 512)
        sk = peer_subkeys[l].reshape(2 * PEER_HEADS, N_KEYS, PEER_HALF).astype(mx)
        h1, hn, sub = _merge_call(h, ma, gb, yb, w_b_out[l].astype(mx), w_o[l].astype(mx), r2(norm2_g[l]),
                                  peer_wq[l].astype(mx), sk, min(2 * tm, t))
        hk = PEER_HEADS * PEER_TOPK
        ei, ej, eg = (a.reshape(hk, t).T for a in _peer_topk_call(sub))
        m4 = _peer_coef_call(ei, ej, eg, min(64, t))
        m3 = m4.reshape(N_KEYS // _PEER_IB, t * _PEER_IB, N_KEYS)
        h = _peer_dense_call(hn, h1, peer_u[l].T.astype(mx), peer_v[l].astype(mx), m3, min(512, t), _PEER_IB, 2)
    return h.reshape(bsz, s, d)
```

```python
import functools
import math

import jax
import jax.numpy as jnp
import numpy as np
from jax import lax
from jax.experimental import pallas as pl
from jax.experimental.pallas import tpu as pltpu

EPS = 1e-6
ROPE_THETA = 500000.0
CHUNK = 128
A_GROUPS = 4
A_GROUP_DIM = 128
N_HEADS = 4
HEAD_DIM = 128
KV_LATENT = 256
ROT_DIM = HEAD_DIM // 4
IDX_HEADS = 4
IDX_DIM = 64
IDX_ROT = IDX_DIM // 4
TOPK_MAX = 256
Q_BLOCK = 128
PEER_HEADS = 8
PEER_HALF = 128
N_KEYS = 128
PEER_TOPK = 16
LANES = 128
_PEER_IB = 8

_MXU_DTYPE = jnp.bfloat16
_INT_MIN = -2147483648
_KEY_LOWEST_FINITE = -2139095040
_LIST_DEPTH = 12
_MIN_DENOM = 2.0 ** -100
_KN_LANE = IDX_DIM + IDX_HEADS
_NEG_BIG = -1e30
_VMEM_LIMIT = 56 * 1024 * 1024
_V_EXT = 2 * HEAD_DIM


def _dot(a, b):
    return jnp.dot(a.astype(_MXU_DTYPE), b.astype(_MXU_DTYPE), preferred_element_type=jnp.float32)


def _dot_nt(a, b):
    return lax.dot_general(a.astype(_MXU_DTYPE), b.astype(_MXU_DTYPE), (((1,), (1,)), ((), ())),
                           preferred_element_type=jnp.float32)


def _rms(x, g):
    return x * lax.rsqrt(jnp.mean(x * x, axis=-1, keepdims=True) + EPS) * g


def _rope(x, cos_t, sin_lo, sin_hi, half):
    n = x.shape[-1]
    x_up = pltpu.roll(x, n - half, 1)
    x_dn = pltpu.roll(x, half, 1)
    return x * cos_t + x_up * sin_lo + x_dn * sin_hi


def _rope_tables(pos, inv):
    ang = pos * inv
    c = jnp.cos(ang)
    s = jnp.sin(ang)
    lane = lax.broadcasted_iota(jnp.int32, ang.shape, 1)

    def tables(c, s, rot):
        half = rot // 2
        return (jnp.where(lane < rot, c, 1.0), jnp.where(lane < half, -s, 0.0),
                jnp.where((lane >= half) & (lane < rot), s, 0.0))

    back = LANES - ROT_DIM
    return tables(c, s, ROT_DIM), tables(pltpu.roll(c, back, 1), pltpu.roll(s, back, 1), IDX_ROT)


_OFF_U, _OFF_V, _OFF_Q, _OFF_C, _OFF_QI, _OFF_KI, _OFF_G = 0, 512, 1024, 1536, 1792, 2304, 2432
_W_COLS = 2432 + 2048


def _inproj_kernel(x_ref, pos_ref, g1_ref, w_ref, vg_ref, vb_ref, ws_ref, bs_ref, kvg_ref, wuk_ref, wuv_ref,
                   qg_ref, kg_ref, wa_ref, inv_ref,
                   ma_ref, gb_ref, q_ref, k_ref, v_ref, qi_ref, ki_ref, wi_ref):
    tm = x_ref.shape[0]
    x = x_ref[...]
    xn = _rms(x, g1_ref[...]).astype(_MXU_DTYPE)

    def proj(off, width):
        return jnp.dot(xn, w_ref[:, off:off + width], preferred_element_type=jnp.float32)

    u = jax.nn.gelu(proj(_OFF_U, 512))
    v = jax.nn.gelu(proj(_OFF_V, 512))
    mu = jnp.mean(v, axis=-1, keepdims=True)
    vc = v - mu
    v = vc * lax.rsqrt(jnp.mean(vc * vc, axis=-1, keepdims=True) + EPS) * vg_ref[...] + vb_ref[...]
    v = v.astype(_MXU_DTYPE)
    row = lax.broadcasted_iota(jnp.int32, (CHUNK, CHUNK), 0)
    col = lax.broadcasted_iota(jnp.int32, (CHUNK, CHUNK), 1)
    z_chunks = []
    for c in range(tm // CHUNK):
        zg = []
        for g in range(A_GROUPS):
            wt = jnp.where(row >= col, ws_ref[g], 0.0).astype(_MXU_DTYPE)
            vcg = v[c * CHUNK:(c + 1) * CHUNK, g * A_GROUP_DIM:(g + 1) * A_GROUP_DIM]
            zg.append(jnp.dot(wt, vcg, preferred_element_type=jnp.float32) + bs_ref[:, g:g + 1])
        z_chunks.append(jnp.concatenate(zg, axis=1))
    z = jnp.concatenate(z_chunks, axis=0) if len(z_chunks) > 1 else z_chunks[0]
    ya = u * z
    gate_a = jax.nn.sigmoid(proj(_OFF_G, 1024))
    ma_ref[...] = gate_a * _dot(ya, wa_ref[...])
    gb_ref[...] = jax.nn.sigmoid(proj(_OFF_G + 1024, 1024))

    pos = pos_ref[...]
    (cq, sq_lo, sq_hi), (ci, si_lo, si_hi) = _rope_tables(pos, inv_ref[...])

    q = proj(_OFF_Q, 512)
    qg = qg_ref[...]
    qh = [_rms(q[:, h * HEAD_DIM:(h + 1) * HEAD_DIM], qg) for h in range(N_HEADS)]
    qh = [_rope(t, cq, sq_lo, sq_hi, ROT_DIM // 2) for t in qh]
    q_ref[...] = (jnp.concatenate(qh, axis=1) * (HEAD_DIM ** -0.5 * math.log2(math.e))).astype(q_ref.dtype)

    c_n = _rms(proj(_OFF_C, KV_LATENT), kvg_ref[...]).astype(_MXU_DTYPE)
    kk = _rms(jnp.dot(c_n, wuk_ref[...], preferred_element_type=jnp.float32), kg_ref[...])
    k_out = _rope(kk, cq, sq_lo, sq_hi, ROT_DIM // 2).astype(k_ref.dtype)
    k_ref[...] = k_out
    k32 = k_out.astype(jnp.float32)
    k_norm2 = jnp.sum(k32 * k32, axis=-1, keepdims=True)
    vv = jnp.dot(c_n, wuv_ref[...], preferred_element_type=jnp.float32)
    ones_col = jnp.where(lax.broadcasted_iota(jnp.int32, vv.shape, 1) == 0, 1.0, 0.0)
    v_ref[...] = jnp.concatenate([vv, ones_col], axis=1).astype(v_ref.dtype)

    qi = proj(_OFF_QI, 512)
    qis = [_rope(qi[:, h * LANES:(h + 1) * LANES], ci, si_lo, si_hi, IDX_ROT // 2) for h in range(IDX_HEADS)]
    qi_ref[...] = (jnp.concatenate(qis, axis=1) * (IDX_DIM ** -0.5)).astype(qi_ref.dtype)
    kw = proj(_OFF_KI, LANES)
    lane = lax.broadcasted_iota(jnp.int32, kw.shape, 1)
    ki = _rope(jnp.where(lane < IDX_DIM, kw, 0.0), ci, si_lo, si_hi, IDX_ROT // 2)
    ki_ref[...] = ki.astype(ki_ref.dtype)
    wi_ref[...] = jnp.where(lane == _KN_LANE, k_norm2, kw * (IDX_HEADS ** -0.5))


def _inproj_call(x2, pos, g1, w_pack, vg, vb, ws, bs_t, kvg, wuk, wuv, qg, kg, wa, inv, tm):
    t, d = x2.shape
    full = lambda a: pl.BlockSpec(a.shape, lambda i: (0,) * a.ndim)
    row = lambda w: pl.BlockSpec((tm, w), lambda i: (i, 0))
    f32, mx = jnp.float32, _MXU_DTYPE
    outs = [(1024, f32), (1024, f32), (512, mx), (128, mx), (_V_EXT, mx), (512, mx), (128, mx), (128, f32)]
    return pl.pallas_call(
        _inproj_kernel,
        grid=(t // tm,),
        in_specs=[row(d), row(1), full(g1), full(w_pack), full(vg), full(vb), full(ws), full(bs_t), full(kvg),
                  full(wuk), full(wuv), full(qg), full(kg), full(wa), full(inv)],
        out_specs=[row(w) for w, _ in outs],
        out_shape=[jax.ShapeDtypeStruct((t, w), dt) for w, dt in outs],
        compiler_params=pltpu.CompilerParams(dimension_semantics=("arbitrary",), vmem_limit_bytes=_VMEM_LIMIT),
        name="inproj",
    )(x2, pos, g1, w_pack, vg, vb, ws, bs_t, kvg, wuk, wuv, qg, kg, wa, inv)


def _dsa_kernel(q_ref, qi_ref, wi_ref, kT_ref, kiT_ref, v_ref, kn_ref, o_ref,
                sc_ref, list_ref, listT_ref, thr_ref, need_ref, wb_ref, qa_ref, shift_ref, p_ref, m_ref, acc_ref,
                *, tk, n_sel):
    blk = pl.program_id(1)
    nq = Q_BLOCK
    ck = 2 * tk
    n_pairs = (blk * nq + nq + ck - 1) // ck

    wi = wi_ref[...]
    for h in range(IDX_HEADS):
        wb_ref[h] = jnp.broadcast_to(wi[:, IDX_DIM + h:IDX_DIM + h + 1], (nq, tk))

    cq = 2 * ck
    n_quads = (n_pairs + 1) // 2
    tiles_per_pair = ck // LANES
    tiles_per_quad = cq // LANES
    depth = _LIST_DEPTH
    qpos = blk * nq + lax.broadcasted_iota(jnp.int32, (nq, tk), 0)
    lane_pos = lax.broadcasted_iota(jnp.int32, (nq, tk), 1)

    def score_quad(qd):
        for sub in range(cq // tk):
            off = pl.multiple_of(qd * cq + sub * tk, tk)
            kt = kiT_ref[:, pl.ds(off, tk)]
            sc = None
            for h in range(IDX_HEADS):
                lg = jnp.dot(qi_ref[:, h * LANES:(h + 1) * LANES], kt, preferred_element_type=jnp.float32)
                term = wb_ref[h] * jnp.maximum(lg, 0.0)
                sc = term if sc is None else sc + term
            sc = jnp.where(off + lane_pos <= qpos, sc, -jnp.inf)
            for c in range(tk // LANES):
                sc_ref[qd * tiles_per_quad + sub * (tk // LANES) + c] = sc[:, c * LANES:(c + 1) * LANES]

    def insert_quad(qd):
        for g in range(nq // 8):
            rows = pl.ds(g * 8, 8)
            lists = [list_ref[i, rows, :] for i in range(depth)]
            for c in range(tiles_per_quad):
                x = sc_ref[qd * tiles_per_quad + c, rows, :]
                for i in range(depth):
                    lists[i], x = jnp.maximum(lists[i], x), jnp.minimum(lists[i], x)
            for i in range(depth):
                list_ref[i, rows, :] = lists[i]

    def score_step(qd, carry):
        score_quad(qd)
        insert_quad(qd - 1)
        return carry

    list_ref[...] = jnp.full(list_ref.shape, -jnp.inf, jnp.float32)
    score_quad(0)
    lax.fori_loop(1, n_quads, score_step, 0)
    insert_quad(n_quads - 1)


    def to_key(x):
        bits = pltpu.bitcast(x, jnp.int32)
        return bits ^ ((bits >> 31) & 0x7FFFFFFF)

    def to_score(key):
        return pltpu.bitcast(key ^ ((key >> 31) & 0x7FFFFFFF), jnp.float32)

    def kth_largest(count_ge, shape):
        zero = jnp.zeros(shape, jnp.int32)
        base = jnp.where(count_ge(zero) >= n_sel, zero, _INT_MIN)

        def bit_step(b, base):
            cand = base | jnp.left_shift(jnp.int32(1), 30 - b)
            return jnp.where(count_ge(cand) >= n_sel, cand, base)

        thr = lax.fori_loop(0, 31, bit_step, base)
        return jnp.maximum(thr, _KEY_LOWEST_FINITE)

    for i in range(depth):
        listT_ref[i] = to_key(list_ref[i].T)

    def pick(bits, leaves):
        if not bits:
            return leaves[0]
        half = len(leaves) // 2
        return jnp.where(bits[0], pick(bits[1:], leaves[half:]), pick(bits[1:], leaves[:half]))

    def count_lists(cand_row):
        rows = LANES // 8
        tot = None
        for g in range(rows):
            sl = pl.ds(g * 8, 8)
            levels = [listT_ref[i, sl, :] for i in range(depth)]
            floor = jnp.full((8, nq), _INT_MIN, jnp.int32)
            bits, cnt = [], None
            n_steps = depth.bit_length()
            for k in range(n_steps):
                step = 1 << (n_steps - 1 - k)
                leaves = []
                for r_hi in range(1 << k):
                    idx = r_hi * 2 * step + step - 1
                    leaves.append(levels[idx] if idx < depth else floor)
                hit = pick(bits, leaves) >= cand_row
                bits.append(hit)
                add = jnp.where(hit, step, 0)
                cnt = add if cnt is None else cnt + add
            tot = cnt if tot is None else tot + cnt
        return jnp.sum(tot, axis=0, keepdims=True)

    thr_row = kth_largest(count_lists, (1, nq))
    need_row = n_sel - count_lists(thr_row + 1)
    thr_ref[...] = to_score(jnp.broadcast_to(thr_row, (nq, nq))).T[:, 0:1]
    need_ref[...] = jnp.broadcast_to(need_row, (nq, nq)).astype(jnp.float32).T[:, 0:1]
    overflow = jnp.max(jnp.where(listT_ref[depth - 1] > thr_row, 1, 0))

    @pl.when(overflow > 0)
    def _():
        def count_all(cand):
            cb = jnp.broadcast_to(cand, (nq, LANES))

            def body(j, cnt):
                for c in range(tiles_per_pair):
                    cnt = cnt + (to_key(sc_ref[j * tiles_per_pair + c]) >= cb).astype(jnp.int32)
                return cnt

            cnt = lax.fori_loop(0, n_pairs, body, jnp.zeros((nq, LANES), jnp.int32))
            return jnp.sum(cnt, axis=1, keepdims=True)

        thr_col = kth_largest(count_all, (nq, 1))
        thr_ref[...] = to_score(thr_col)
        need_ref[...] = (n_sel - count_all(thr_col + 1)).astype(jnp.float32)

    thr = jnp.broadcast_to(thr_ref[...], (nq, tk))
    need = need_ref[...]

    tri = (lax.broadcasted_iota(jnp.int32, (tk, tk), 0) <= lax.broadcasted_iota(jnp.int32, (tk, tk), 1))
    tri = jnp.where(tri, 1.0, 0.0).astype(_MXU_DTYPE)

    def selected(j, sub, room):
        t0 = (j * ck + sub * tk) // LANES
        sc = jnp.concatenate([sc_ref[t0 + c] for c in range(tk // LANES)], axis=1)
        eq = sc == thr
        pre = jnp.dot(jnp.where(eq, 1.0, 0.0).astype(_MXU_DTYPE), tri, preferred_element_type=jnp.float32)
        return (sc > thr) | (eq & (pre <= room)), room - pre[:, tk - 1:tk]

    q32 = q_ref[...].astype(jnp.float32)
    qn2 = None
    for h in range(N_HEADS):
        qh = q32[:, h * HEAD_DIM:(h + 1) * HEAD_DIM]
        n2 = jnp.sum(qh * qh, axis=1, keepdims=True)
        qn2 = n2 if qn2 is None else jnp.maximum(qn2, n2)
    kn2 = jnp.max(kn_ref[...], axis=1, keepdims=True)
    neg_bound = jnp.broadcast_to(-jnp.sqrt(qn2 * kn2), (nq, tk))
    eye = (lax.broadcasted_iota(jnp.int32, (nq, nq), 0) == lax.broadcasted_iota(jnp.int32, (nq, nq), 1))
    eye = jnp.where(eye, 1.0, 0.0).astype(_MXU_DTYPE)
    for h in range(N_HEADS):
        qa_ref[h] = jnp.concatenate([q_ref[:, h * HEAD_DIM:(h + 1) * HEAD_DIM], eye], axis=1)
    acc_ref[...] = jnp.zeros(acc_ref.shape, jnp.float32)

    last = n_quads - 1

    def stage_mask(qd, room):
        for sub in range(cq // tk):
            sel, room = selected(2 * qd, sub, room)
            shift_ref[qd % 2, :, sub * tk:(sub + 1) * tk] = jnp.where(sel, neg_bound, _NEG_BIG).astype(_MXU_DTYPE)
        return room

    def stage_weights(qd):
        rhs = jnp.concatenate([kT_ref[:, pl.ds(pl.multiple_of(qd * cq, cq), cq)], shift_ref[qd % 2]], axis=0)
        for h in range(N_HEADS):
            p_ref[qd % 2, h] = jnp.exp2(jnp.dot(qa_ref[h], rhs, preferred_element_type=jnp.float32)).astype(_MXU_DTYPE)

    def stage_values(qd):
        vt = v_ref[pl.ds(pl.multiple_of(qd * cq, cq), cq), :]
        for h in range(N_HEADS):
            acc_ref[h] += jnp.dot(p_ref[qd % 2, h], vt, preferred_element_type=jnp.float32)

    def shifted_step(j, room):
        stage_values(j - 1)
        stage_weights(j)
        return stage_mask(jnp.minimum(j + 1, last), room)

    room = stage_mask(0, need)
    stage_weights(0)
    room = stage_mask(jnp.minimum(1, last), room)
    lax.fori_loop(1, n_quads, shifted_step, room)
    stage_values(last)
    denom = acc_ref[0][:, HEAD_DIM:HEAD_DIM + 1]
    for h in range(1, N_HEADS):
        denom = jnp.minimum(denom, acc_ref[h][:, HEAD_DIM:HEAD_DIM + 1])
    l_min = jnp.min(denom)

    @pl.when(l_min < _MIN_DENOM)
    def _():
        m_ref[...] = jnp.full(m_ref.shape, _NEG_BIG, jnp.float32)
        acc_ref[...] = jnp.zeros(acc_ref.shape, jnp.float32)

        def online_pair(j, room):
            off = pl.multiple_of(j * ck, ck)
            biases = []
            for sub in range(2):
                sel, room = selected(j, sub, room)
                biases.append(jnp.where(sel, 0.0, _NEG_BIG))
            bias = jnp.concatenate(biases, axis=1)
            kt = kT_ref[:, pl.ds(off, ck)]
            vt = v_ref[pl.ds(off, ck), :]
            for h in range(N_HEADS):
                s = jnp.dot(q_ref[:, h * HEAD_DIM:(h + 1) * HEAD_DIM], kt, preferred_element_type=jnp.float32) + bias
                m_old = m_ref[h]
                m_new = jnp.maximum(m_old, jnp.max(s, axis=1, keepdims=True))
                alpha = jnp.exp2(m_old - m_new)
                p = jnp.exp2(s - jnp.concatenate([m_new] * (ck // LANES), axis=1))
                acc_ref[h] = (jnp.concatenate([alpha] * (_V_EXT // LANES), axis=1) * acc_ref[h]
                              + jnp.dot(p.astype(_MXU_DTYPE), vt, preferred_element_type=jnp.float32))
                m_ref[h] = m_new
            return room

        lax.fori_loop(0, n_pairs, online_pair, need)

    ys = []
    for h in range(N_HEADS):
        a = acc_ref[h]
        ys.append(a[:, :HEAD_DIM] / a[:, HEAD_DIM:HEAD_DIM + 1])
    o_ref[...] = jnp.concatenate(ys, axis=1).astype(o_ref.dtype)


def _dsa_call(q, qi, wi, kT, kiT, v, kn, n_sel, tk):
    b, s, _ = q.shape
    nblk = s // Q_BLOCK
    assert s % (4 * tk) == 0 and tk % Q_BLOCK == 0
    qspec = lambda w: pl.BlockSpec((None, Q_BLOCK, w), lambda bi, i: (bi, i, 0))
    kern = functools.partial(_dsa_kernel, tk=tk, n_sel=n_sel)
    return pl.pallas_call(
        kern,
        grid=(b, nblk),
        in_specs=[qspec(512), qspec(512), qspec(128),
                  pl.BlockSpec((None, HEAD_DIM, s), lambda bi, i: (bi, 0, 0)),
                  pl.BlockSpec((None, LANES, s), lambda bi, i: (bi, 0, 0)),
                  pl.BlockSpec((None, s, _V_EXT), lambda bi, i: (bi, 0, 0)),
                  pl.BlockSpec((None, 1, s), lambda bi, i: (bi, 0, 0))],
        out_specs=qspec(512),
        out_shape=jax.ShapeDtypeStruct((b, s, 512), _MXU_DTYPE),
        scratch_shapes=[pltpu.VMEM((s // LANES, Q_BLOCK, LANES), jnp.float32),
                        pltpu.VMEM((_LIST_DEPTH, Q_BLOCK, LANES), jnp.float32),
                        pltpu.VMEM((_LIST_DEPTH, LANES, Q_BLOCK), jnp.int32),
                        pltpu.VMEM((Q_BLOCK, 1), jnp.float32),
                        pltpu.VMEM((Q_BLOCK, 1), jnp.float32),
                        pltpu.VMEM((IDX_HEADS, Q_BLOCK, tk), jnp.float32),
                        pltpu.VMEM((N_HEADS, Q_BLOCK, HEAD_DIM + Q_BLOCK), _MXU_DTYPE),
                        pltpu.VMEM((2, Q_BLOCK, 4 * tk), _MXU_DTYPE),
                        pltpu.VMEM((2, N_HEADS, Q_BLOCK, 4 * tk), _MXU_DTYPE),
                        pltpu.VMEM((N_HEADS, Q_BLOCK, LANES), jnp.float32),
                        pltpu.VMEM((N_HEADS, Q_BLOCK, _V_EXT), jnp.float32)],
        compiler_params=pltpu.CompilerParams(dimension_semantics=("arbitrary", "arbitrary"),
                                             vmem_limit_bytes=_VMEM_LIMIT),
        name="dsa",
    )(q, qi, wi, kT, kiT, v, kn)


_TOK_CHUNK = 8 * LANES


def _merge_kernel(x_ref, ma_ref, gb_ref, yb_ref, wb_ref, wo_ref, g2_ref, wq_ref, sk_ref, h_ref, hn_ref, sub_ref):
    tm = x_ref.shape[0]
    merged = ma_ref[...] + gb_ref[...] * jnp.dot(yb_ref[...], wb_ref[...], preferred_element_type=jnp.float32)
    h1 = x_ref[...] + _dot(merged, wo_ref[...])
    h_ref[...] = h1
    hn = _rms(h1, g2_ref[...]).astype(_MXU_DTYPE)
    hn_ref[...] = hn
    qq = jnp.dot(hn, wq_ref[...], preferred_element_type=jnp.float32).astype(_MXU_DTYPE)
    groups = tm // LANES
    first = (pl.program_id(0) % (_TOK_CHUNK // tm)) * groups
    for hp in range(2 * PEER_HEADS):
        tile = _dot_nt(sk_ref[hp], qq[:, hp * PEER_HALF:(hp + 1) * PEER_HALF])
        for g in range(groups):
            sub_ref[hp, pl.ds(first + g, N_KEYS, stride=8), :] = tile[:, g * LANES:(g + 1) * LANES]


def _merge_call(x2, ma, gb, yb, wb, wo, g2, wq, sk, tm):
    t, d = x2.shape
    assert t % _TOK_CHUNK == 0 and _TOK_CHUNK % tm == 0
    full = lambda a: pl.BlockSpec(a.shape, lambda i: (0,) * a.ndim)
    row = lambda w: pl.BlockSpec((tm, w), lambda i: (i, 0))
    steps = _TOK_CHUNK // tm
    sub_shape = (2 * PEER_HEADS, t // _TOK_CHUNK, N_KEYS * 8, LANES)
    return pl.pallas_call(
        _merge_kernel,
        grid=(t // tm,),
        in_specs=[row(d), row(d), row(d), row(512), full(wb), full(wo), full(g2), full(wq), full(sk)],
        out_specs=[row(d), row(d),
                   pl.BlockSpec((2 * PEER_HEADS, None, N_KEYS * 8, LANES), lambda i: (0, i // steps, 0, 0))],
        out_shape=[jax.ShapeDtypeStruct((t, d), jnp.float32), jax.ShapeDtypeStruct((t, d), _MXU_DTYPE),
                   jax.ShapeDtypeStruct(sub_shape, jnp.float32)],
        compiler_params=pltpu.CompilerParams(dimension_semantics=("arbitrary",), vmem_limit_bytes=_VMEM_LIMIT),
        name="merge",
    )(x2, ma, gb, yb, wb, wo, g2, wq, sk)


def _insert_sorted(vals, pays, x, xp, first=0):
    for lvl in range(first, len(vals)):
        c = x > vals[lvl]
        vals[lvl], x = jnp.where(c, x, vals[lvl]), jnp.where(c, vals[lvl], x)
        pays[lvl], xp = jnp.where(c, xp, pays[lvl]), jnp.where(c, pays[lvl], xp)


_PEER_PAIRS = [(a, b) for a in range(PEER_TOPK) for b in range(PEER_TOPK // (a + 1))]
_KEYS_PER_STEP = 16


def _peer_topk_kernel(sub_ref, i_ref, j_ref, g_ref):
    kk = PEER_TOPK
    neg = jnp.full((8, LANES), -jnp.inf, jnp.float32)
    zero = jnp.zeros((8, LANES), jnp.int32)

    def sorted_top(hp):
        def step(k4, carry):
            vals, idxs = list(carry[0]), list(carry[1])
            for u in range(_KEYS_PER_STEP):
                key = k4 * _KEYS_PER_STEP + u
                x = sub_ref[hp, pl.ds(pl.multiple_of(key * 8, 8), 8), :]
                _insert_sorted(vals, idxs, x, zero + key)
            return tuple(vals), tuple(idxs)

        return lax.fori_loop(0, N_KEYS // _KEYS_PER_STEP, step, ((neg,) * kk, (zero,) * kk))

    def head(h, carry):
        s1, i1 = sorted_top(2 * h)
        s2, i2 = sorted_top(2 * h + 1)
        best, expert = [neg] * kk, [zero] * kk
        for a, b in _PEER_PAIRS:
            _insert_sorted(best, expert, s1[a] + s2[b], i1[a] * N_KEYS + i2[b], first=(a + 1) * (b + 1) - 1)
        e = [jnp.exp(v - best[0]) for v in best]
        denom = e[0]
        for v in e[1:]:
            denom = denom + v
        for k in range(kk):
            i_ref[h * kk + k] = expert[k] >> 7
            j_ref[h * kk + k] = expert[k] & (N_KEYS - 1)
            g_ref[h * kk + k] = e[k] / denom
        return carry

    lax.fori_loop(0, PEER_HEADS, head, 0)


def _peer_topk_call(sub):
    hp, chunks, rows, _ = sub.shape
    hk = PEER_HEADS * PEER_TOPK
    out = pl.BlockSpec((hk, 8, LANES), lambda i: (0, i, 0))
    osd = lambda dt: jax.ShapeDtypeStruct((hk, chunks * 8, LANES), dt)
    return pl.pallas_call(
        _peer_topk_kernel,
        grid=(chunks,),
        in_specs=[pl.BlockSpec((hp, None, rows, LANES), lambda i: (0, i, 0, 0))],
        out_specs=[out, out, out],
        out_shape=[osd(jnp.int32), osd(jnp.int32), osd(jnp.float32)],
        compiler_params=pltpu.CompilerParams(dimension_semantics=("arbitrary",), vmem_limit_bytes=_VMEM_LIMIT),
        name="peer_topk",
    )(sub)


def _peer_coef_kernel(i_ref, j_ref, g_ref, m_ref):
    tt = i_ref.shape[0]
    rid = lax.broadcasted_iota(jnp.int32, (N_KEYS, LANES), 0)

    def body(t, carry):
        irow = i_ref[pl.ds(t, 1), :]
        jrow = j_ref[pl.ds(t, 1), :]
        grow = g_ref[pl.ds(t, 1), :]
        rt = jnp.where(rid == irow, grow, 0.0)
        ct = jnp.where(rid == jrow, 1.0, 0.0)
        mt = _dot_nt(rt, ct)
        for e in range(N_KEYS // _PEER_IB):
            m_ref[e, t] = mt[e * _PEER_IB:(e + 1) * _PEER_IB, :]
        return carry

    lax.fori_loop(0, tt, body, 0, unroll=True)


def _peer_coef_call(it, jt, gt, tt):
    t, hk = it.shape
    row = pl.BlockSpec((tt, hk), lambda i: (i, 0))
    nb = N_KEYS // _PEER_IB
    return pl.pallas_call(
        _peer_coef_kernel,
        grid=(t // tt,),
        in_specs=[row, row, row],
        out_specs=pl.BlockSpec((nb, tt, _PEER_IB, N_KEYS), lambda i: (0, i, 0, 0)),
        out_shape=jax.ShapeDtypeStruct((nb, t, _PEER_IB, N_KEYS), jnp.float32),
        compiler_params=pltpu.CompilerParams(dimension_semantics=("arbitrary",), vmem_limit_bytes=_VMEM_LIMIT),
        name="peer_coef",
    )(it, jt, gt)


def _peer_dense_kernel(hn_ref, h_ref, ut_ref, v_ref, m_ref, o_ref, *, ib):
    e = pl.program_id(1)

    @pl.when(e == 0)
    def _():
        o_ref[...] = h_ref[...]

    tm = hn_ref.shape[0]
    a = jax.nn.gelu(jnp.dot(hn_ref[...], ut_ref[...], preferred_element_type=jnp.float32))
    c = jnp.concatenate([a[:, (b * ib + ii) * N_KEYS:(b * ib + ii + 1) * N_KEYS] * m_ref[b, pl.ds(ii, tm, stride=ib), :]
                         for b in range(m_ref.shape[0]) for ii in range(ib)], axis=1)
    o_ref[...] += jnp.dot(c.astype(_MXU_DTYPE), v_ref[...], preferred_element_type=jnp.float32)


def _peer_dense_call(hn, h1, ut, v, m3, tm, ib, nb):
    t, d = hn.shape
    te = nb * ib * N_KEYS
    kern = functools.partial(_peer_dense_kernel, ib=ib)
    return pl.pallas_call(
        kern,
        grid=(t // tm, N_KEYS // (nb * ib)),
        in_specs=[pl.BlockSpec((tm, d), lambda ti, e: (ti, 0)),
                  pl.BlockSpec((tm, d), lambda ti, e: (ti, 0)),
                  pl.BlockSpec((d, te), lambda ti, e: (0, e)),
                  pl.BlockSpec((te, d), lambda ti, e: (e, 0)),
                  pl.BlockSpec((nb, tm * ib, N_KEYS), lambda ti, e: (e, ti, 0))],
        out_specs=pl.BlockSpec((tm, d), lambda ti, e: (ti, 0)),
        out_shape=jax.ShapeDtypeStruct((t, d), jnp.float32),
        compiler_params=pltpu.CompilerParams(dimension_semantics=("arbitrary", "arbitrary"),
                                             vmem_limit_bytes=_VMEM_LIMIT),
        name="peer_dense",
    )(hn, h1, ut, v, m3)


def _rope_inv(rot_dim, period):
    half = rot_dim // 2
    inv = jnp.power(jnp.float32(ROPE_THETA), -jnp.arange(half, dtype=jnp.float32) * 2.0 / rot_dim)
    lane = np.arange(LANES)
    in_rot = (lane % period) < rot_dim
    pat = jnp.where(jnp.asarray(in_rot), inv[jnp.asarray(lane % period % half)], 0.0)
    return pat.reshape(1, LANES).astype(jnp.float32)


def _pack_w_in(w):
    d = w.shape[0]
    z = lambda n: jnp.zeros((d, n), w.dtype)
    segs = [w[:, 0:1536 + KV_LATENT]]
    for h in range(IDX_HEADS):
        segs += [w[:, 1792 + h * IDX_DIM:1792 + (h + 1) * IDX_DIM], z(LANES - IDX_DIM)]
    segs += [w[:, 2048:2116], z(LANES - 68)]
    segs += [w[:, 2116:]]
    out = jnp.concatenate(segs, axis=1)
    assert out.shape[1] == _W_COLS
    return out.astype(_MXU_DTYPE)


def kernel(x, positions, norm1_g, w_in, v_norm_g, v_norm_b, spatial_w, spatial_b, kv_norm_g, w_uk, w_uv,
           q_norm_g, k_norm_g, w_a_out, w_b_out, w_o, norm2_g, peer_wq, peer_subkeys, peer_u, peer_v):
    bsz, s, d = x.shape
    t = bsz * s
    depth = w_in.shape[0]
    mx = _MXU_DTYPE
    n_sel = min(TOPK_MAX, s // 4)
    tm = min(256, t)
    tk = min(256, s // 4)
    inv = _rope_inv(ROT_DIM, LANES) + jnp.roll(_rope_inv(IDX_ROT, LANES), ROT_DIM, axis=1)
    pos = positions.reshape(t, 1).astype(jnp.float32)
    r2 = lambda a: a.reshape(1, -1)
    h = x.reshape(t, d)
    for l in range(depth):
        ma, gb, q, k, v, qi, ki, wi = _inproj_call(
            h, pos, r2(norm1_g[l]), _pack_w_in(w_in[l]), r2(v_norm_g[l]), r2(v_norm_b[l]), spatial_w[l],
            spatial_b[l].T, r2(kv_norm_g[l]), w_uk[l].astype(mx), w_uv[l].astype(mx), r2(q_norm_g[l]),
            r2(k_norm_g[l]), w_a_out[l].astype(mx), inv, tm)
        b3 = lambda a: a.reshape(bsz, s, a.shape[-1])
        kT = b3(k).transpose(0, 2, 1)
        kiT = b3(ki).transpose(0, 2, 1)
        kn = wi[:, _KN_LANE].reshape(bsz, 1, s)
        yb = _dsa_call(b3(q), b3(qi), b3(wi), kT, kiT, b3(v), kn, n_sel, tk).reshape(t, 512)
        sk = peer_subkeys[l].reshape(2 * PEER_HEADS, N_KEYS, PEER_HALF).astype(mx)
        h1, hn, sub = _merge_call(h, ma, gb, yb, w_b_out[l].astype(mx), w_o[l].astype(mx), r2(norm2_g[l]),
                                  peer_wq[l].astype(mx), sk, min(2 * tm, t))
        hk = PEER_HEADS * PEER_TOPK
        ei, ej, eg = (a.reshape(hk, t).T for a in _peer_topk_call(sub))
        m4 = _peer_coef_call(ei, ej, eg, min(128, t))
        m3 = m4.reshape(N_KEYS // _PEER_IB, t * _PEER_IB, N_KEYS)
        h = _peer_dense_call(hn, h1, peer_u[l].T.astype(mx), peer_v[l].astype(mx), m3, min(512, t), _PEER_IB, 2)
    return h.reshape(bsz, s, d)
```

```python
import functools
import math

import jax
import jax.numpy as jnp
import numpy as np
from jax import lax
from jax.experimental import pallas as pl
from jax.experimental.pallas import tpu as pltpu

EPS = 1e-6
ROPE_THETA = 500000.0
CHUNK = 128
A_GROUPS = 4
A_GROUP_DIM = 128
N_HEADS = 4
HEAD_DIM = 128
KV_LATENT = 256
ROT_DIM = HEAD_DIM // 4
IDX_HEADS = 4
IDX_DIM = 64
IDX_ROT = IDX_DIM // 4
TOPK_MAX = 256
Q_BLOCK = 128
PEER_HEADS = 8
PEER_HALF = 128
N_KEYS = 128
PEER_TOPK = 16
LANES = 128
_PEER_IB = 8

_MXU_DTYPE = jnp.bfloat16
_INT_MIN = -2147483648
_KEY_LOWEST_FINITE = -2139095040
_LIST_DEPTH = 12
_MIN_DENOM = 2.0 ** -100
_KN_LANE = IDX_DIM + IDX_HEADS
_NEG_BIG = -1e30
_VMEM_LIMIT = 56 * 1024 * 1024
_V_EXT = 2 * HEAD_DIM


def _dot(a, b):
    return jnp.dot(a.astype(_MXU_DTYPE), b.astype(_MXU_DTYPE), preferred_element_type=jnp.float32)


def _dot_nt(a, b):
    return lax.dot_general(a.astype(_MXU_DTYPE), b.astype(_MXU_DTYPE), (((1,), (1,)), ((), ())),
                           preferred_element_type=jnp.float32)


def _rms(x, g):
    return x * lax.rsqrt(jnp.mean(x * x, axis=-1, keepdims=True) + EPS) * g


def _rope(x, cos_t, sin_lo, sin_hi, half):
    n = x.shape[-1]
    x_up = pltpu.roll(x, n - half, 1)
    x_dn = pltpu.roll(x, half, 1)
    return x * cos_t + x_up * sin_lo + x_dn * sin_hi


def _rope_tables(pos, inv):
    ang = pos * inv
    c = jnp.cos(ang)
    s = jnp.sin(ang)
    lane = lax.broadcasted_iota(jnp.int32, ang.shape, 1)

    def tables(c, s, rot):
        half = rot // 2
        return (jnp.where(lane < rot, c, 1.0), jnp.where(lane < half, -s, 0.0),
                jnp.where((lane >= half) & (lane < rot), s, 0.0))

    back = LANES - ROT_DIM
    return tables(c, s, ROT_DIM), tables(pltpu.roll(c, back, 1), pltpu.roll(s, back, 1), IDX_ROT)


_OFF_U, _OFF_V, _OFF_Q, _OFF_C, _OFF_QI, _OFF_KI, _OFF_G = 0, 512, 1024, 1536, 1792, 2304, 2432
_W_COLS = 2432 + 2048


def _inproj_kernel(x_ref, pos_ref, g1_ref, w_ref, vg_ref, vb_ref, ws_ref, bs_ref, kvg_ref, wuk_ref, wuv_ref,
                   qg_ref, kg_ref, wa_ref, inv_ref,
                   ma_ref, gb_ref, q_ref, k_ref, v_ref, qi_ref, ki_ref, wi_ref):
    tm = x_ref.shape[0]
    x = x_ref[...]
    xn = _rms(x, g1_ref[...]).astype(_MXU_DTYPE)

    def proj(off, width):
        return jnp.dot(xn, w_ref[:, off:off + width], preferred_element_type=jnp.float32)

    u = jax.nn.gelu(proj(_OFF_U, 512))
    v = jax.nn.gelu(proj(_OFF_V, 512))
    mu = jnp.mean(v, axis=-1, keepdims=True)
    vc = v - mu
    v = vc * lax.rsqrt(jnp.mean(vc * vc, axis=-1, keepdims=True) + EPS) * vg_ref[...] + vb_ref[...]
    v = v.astype(_MXU_DTYPE)
    row = lax.broadcasted_iota(jnp.int32, (CHUNK, CHUNK), 0)
    col = lax.broadcasted_iota(jnp.int32, (CHUNK, CHUNK), 1)
    z_chunks = []
    for c in range(tm // CHUNK):
        zg = []
        for g in range(A_GROUPS):
            wt = jnp.where(row >= col, ws_ref[g], 0.0).astype(_MXU_DTYPE)
            vcg = v[c * CHUNK:(c + 1) * CHUNK, g * A_GROUP_DIM:(g + 1) * A_GROUP_DIM]
            zg.append(jnp.dot(wt, vcg, preferred_element_type=jnp.float32) + bs_ref[:, g:g + 1])
        z_chunks.append(jnp.concatenate(zg, axis=1))
    z = jnp.concatenate(z_chunks, axis=0) if len(z_chunks) > 1 else z_chunks[0]
    ya = u * z
    gate_a = jax.nn.sigmoid(proj(_OFF_G, 1024))
    ma_ref[...] = gate_a * _dot(ya, wa_ref[...])
    gb_ref[...] = jax.nn.sigmoid(proj(_OFF_G + 1024, 1024))

    pos = pos_ref[...]
    (cq, sq_lo, sq_hi), (ci, si_lo, si_hi) = _rope_tables(pos, inv_ref[...])

    q = proj(_OFF_Q, 512)
    qg = qg_ref[...]
    qh = [_rms(q[:, h * HEAD_DIM:(h + 1) * HEAD_DIM], qg) for h in range(N_HEADS)]
    qh = [_rope(t, cq, sq_lo, sq_hi, ROT_DIM // 2) for t in qh]
    q_ref[...] = (jnp.concatenate(qh, axis=1) * (HEAD_DIM ** -0.5 * math.log2(math.e))).astype(q_ref.dtype)

    c_n = _rms(proj(_OFF_C, KV_LATENT), kvg_ref[...]).astype(_MXU_DTYPE)
    kk = _rms(jnp.dot(c_n, wuk_ref[...], preferred_element_type=jnp.float32), kg_ref[...])
    k_out = _rope(kk, cq, sq_lo, sq_hi, ROT_DIM // 2).astype(k_ref.dtype)
    k_ref[...] = k_out
    k32 = k_out.astype(jnp.float32)
    k_norm2 = jnp.sum(k32 * k32, axis=-1, keepdims=True)
    vv = jnp.dot(c_n, wuv_ref[...], preferred_element_type=jnp.float32)
    ones_col = jnp.where(lax.broadcasted_iota(jnp.int32, vv.shape, 1) == 0, 1.0, 0.0)
    v_ref[...] = jnp.concatenate([vv, ones_col], axis=1).astype(v_ref.dtype)

    qi = proj(_OFF_QI, 512)
    qis = [_rope(qi[:, h * LANES:(h + 1) * LANES], ci, si_lo, si_hi, IDX_ROT // 2) for h in range(IDX_HEADS)]
    qi_ref[...] = (jnp.concatenate(qis, axis=1) * (IDX_DIM ** -0.5)).astype(qi_ref.dtype)
    kw = proj(_OFF_KI, LANES)
    lane = lax.broadcasted_iota(jnp.int32, kw.shape, 1)
    ki = _rope(jnp.where(lane < IDX_DIM, kw, 0.0), ci, si_lo, si_hi, IDX_ROT // 2)
    ki_ref[...] = ki.astype(ki_ref.dtype)
    wi_ref[...] = jnp.where(lane == _KN_LANE, k_norm2, kw * (IDX_HEADS ** -0.5))


def _inproj_call(x2, pos, g1, w_pack, vg, vb, ws, bs_t, kvg, wuk, wuv, qg, kg, wa, inv, tm):
    t, d = x2.shape
    full = lambda a: pl.BlockSpec(a.shape, lambda i: (0,) * a.ndim)
    row = lambda w: pl.BlockSpec((tm, w), lambda i: (i, 0))
    f32, mx = jnp.float32, _MXU_DTYPE
    outs = [(1024, f32), (1024, f32), (512, mx), (128, mx), (_V_EXT, mx), (512, mx), (128, mx), (128, f32)]
    return pl.pallas_call(
        _inproj_kernel,
        grid=(t // tm,),
        in_specs=[row(d), row(1), full(g1), full(w_pack), full(vg), full(vb), full(ws), full(bs_t), full(kvg),
                  full(wuk), full(wuv), full(qg), full(kg), full(wa), full(inv)],
        out_specs=[row(w) for w, _ in outs],
        out_shape=[jax.ShapeDtypeStruct((t, w), dt) for w, dt in outs],
        compiler_params=pltpu.CompilerParams(dimension_semantics=("arbitrary",), vmem_limit_bytes=_VMEM_LIMIT),
        name="inproj",
    )(x2, pos, g1, w_pack, vg, vb, ws, bs_t, kvg, wuk, wuv, qg, kg, wa, inv)


def _dsa_kernel(q_ref, qi_ref, wi_ref, kT_ref, kiT_ref, v_ref, kn_ref, o_ref,
                sc_ref, list_ref, listT_ref, thr_ref, need_ref, wb_ref, qa_ref, shift_ref, p_ref, m_ref, acc_ref,
                *, tk, n_sel):
    blk = pl.program_id(1)
    nq = Q_BLOCK
    ck = 2 * tk
    n_pairs = (blk * nq + nq + ck - 1) // ck

    wi = wi_ref[...]
    for h in range(IDX_HEADS):
        wb_ref[h] = jnp.broadcast_to(wi[:, IDX_DIM + h:IDX_DIM + h + 1], (nq, tk))

    cq = 2 * ck
    n_quads = (n_pairs + 1) // 2
    tiles_per_pair = ck // LANES
    tiles_per_quad = cq // LANES
    depth = _LIST_DEPTH
    qpos = blk * nq + lax.broadcasted_iota(jnp.int32, (nq, tk), 0)
    lane_pos = lax.broadcasted_iota(jnp.int32, (nq, tk), 1)

    def score_quad(qd):
        for sub in range(cq // tk):
            off = pl.multiple_of(qd * cq + sub * tk, tk)
            kt = kiT_ref[:, pl.ds(off, tk)]
            sc = None
            for h in range(IDX_HEADS):
                lg = jnp.dot(qi_ref[:, h * LANES:(h + 1) * LANES], kt, preferred_element_type=jnp.float32)
                term = wb_ref[h] * jnp.maximum(lg, 0.0)
                sc = term if sc is None else sc + term
            sc = jnp.where(off + lane_pos <= qpos, sc, -jnp.inf)
            for c in range(tk // LANES):
                sc_ref[qd * tiles_per_quad + sub * (tk // LANES) + c] = sc[:, c * LANES:(c + 1) * LANES]

    def insert_quad(qd):
        for g in range(nq // 8):
            rows = pl.ds(g * 8, 8)
            lists = [list_ref[i, rows, :] for i in range(depth)]
            for c in range(tiles_per_quad):
                x = sc_ref[qd * tiles_per_quad + c, rows, :]
                for i in range(depth):
                    lists[i], x = jnp.maximum(lists[i], x), jnp.minimum(lists[i], x)
            for i in range(depth):
                list_ref[i, rows, :] = lists[i]

    def score_step(qd, carry):
        score_quad(qd)
        insert_quad(qd - 1)
        return carry

    list_ref[...] = jnp.full(list_ref.shape, -jnp.inf, jnp.float32)
    score_quad(0)
    lax.fori_loop(1, n_quads, score_step, 0)
    insert_quad(n_quads - 1)


    def to_key(x):
        bits = pltpu.bitcast(x, jnp.int32)
        return bits ^ ((bits >> 31) & 0x7FFFFFFF)

    def to_score(key):
        return pltpu.bitcast(key ^ ((key >> 31) & 0x7FFFFFFF), jnp.float32)

    def kth_largest(count_ge, shape):
        zero = jnp.zeros(shape, jnp.int32)
        base = jnp.where(count_ge(zero) >= n_sel, zero, _INT_MIN)

        def bit_step(b, base):
            cand = base | jnp.left_shift(jnp.int32(1), 30 - b)
            return jnp.where(count_ge(cand) >= n_sel, cand, base)

        thr = lax.fori_loop(0, 31, bit_step, base)
        return jnp.maximum(thr, _KEY_LOWEST_FINITE)

    for i in range(depth):
        listT_ref[i] = to_key(list_ref[i].T)

    def pick(bits, leaves):
        if not bits:
            return leaves[0]
        half = len(leaves) // 2
        return jnp.where(bits[0], pick(bits[1:], leaves[half:]), pick(bits[1:], leaves[:half]))

    def count_lists(cand_row):
        rows = LANES // 8
        tot = None
        for g in range(rows):
            sl = pl.ds(g * 8, 8)
            levels = [listT_ref[i, sl, :] for i in range(depth)]
            floor = jnp.full((8, nq), _INT_MIN, jnp.int32)
            bits, cnt = [], None
            n_steps = depth.bit_length()
            for k in range(n_steps):
                step = 1 << (n_steps - 1 - k)
                leaves = []
                for r_hi in range(1 << k):
                    idx = r_hi * 2 * step + step - 1
                    leaves.append(levels[idx] if idx < depth else floor)
                hit = pick(bits, leaves) >= cand_row
                bits.append(hit)
                add = jnp.where(hit, step, 0)
                cnt = add if cnt is None else cnt + add
            tot = cnt if tot is None else tot + cnt
        return jnp.sum(tot, axis=0, keepdims=True)

    thr_row = kth_largest(count_lists, (1, nq))
    need_row = n_sel - count_lists(thr_row + 1)
    thr_ref[...] = to_score(jnp.broadcast_to(thr_row, (nq, nq))).T[:, 0:1]
    need_ref[...] = jnp.broadcast_to(need_row, (nq, nq)).astype(jnp.float32).T[:, 0:1]
    overflow = jnp.max(jnp.where(listT_ref[depth - 1] > thr_row, 1, 0))

    @pl.when(overflow > 0)
    def _():
        def count_all(cand):
            cb = jnp.broadcast_to(cand, (nq, LANES))

            def body(j, cnt):
                for c in range(tiles_per_pair):
                    cnt = cnt + (to_key(sc_ref[j * tiles_per_pair + c]) >= cb).astype(jnp.int32)
                return cnt

            cnt = lax.fori_loop(0, n_pairs, body, jnp.zeros((nq, LANES), jnp.int32))
            return jnp.sum(cnt, axis=1, keepdims=True)

        thr_col = kth_largest(count_all, (nq, 1))
        thr_ref[...] = to_score(thr_col)
        need_ref[...] = (n_sel - count_all(thr_col + 1)).astype(jnp.float32)

    thr = jnp.broadcast_to(thr_ref[...], (nq, tk))
    need = need_ref[...]

    tri = (lax.broadcasted_iota(jnp.int32, (tk, tk), 0) <= lax.broadcasted_iota(jnp.int32, (tk, tk), 1))
    tri = jnp.where(tri, 1.0, 0.0).astype(_MXU_DTYPE)

    def selected(j, sub, room):
        t0 = (j * ck + sub * tk) // LANES
        sc = jnp.concatenate([sc_ref[t0 + c] for c in range(tk // LANES)], axis=1)
        eq = sc == thr
        pre = jnp.dot(jnp.where(eq, 1.0, 0.0).astype(_MXU_DTYPE), tri, preferred_element_type=jnp.float32)
        return (sc > thr) | (eq & (pre <= room)), room - pre[:, tk - 1:tk]

    q32 = q_ref[...].astype(jnp.float32)
    qn2 = None
    for h in range(N_HEADS):
        qh = q32[:, h * HEAD_DIM:(h + 1) * HEAD_DIM]
        n2 = jnp.sum(qh * qh, axis=1, keepdims=True)
        qn2 = n2 if qn2 is None else jnp.maximum(qn2, n2)
    kn2 = jnp.max(kn_ref[...], axis=1, keepdims=True)
    neg_bound = jnp.broadcast_to(-jnp.sqrt(qn2 * kn2), (nq, tk))
    eye = (lax.broadcasted_iota(jnp.int32, (nq, nq), 0) == lax.broadcasted_iota(jnp.int32, (nq, nq), 1))
    eye = jnp.where(eye, 1.0, 0.0).astype(_MXU_DTYPE)
    for h in range(N_HEADS):
        qa_ref[h] = jnp.concatenate([q_ref[:, h * HEAD_DIM:(h + 1) * HEAD_DIM], eye], axis=1)
    acc_ref[...] = jnp.zeros(acc_ref.shape, jnp.float32)

    last = n_quads - 1

    def stage_mask(qd, room):
        for sub in range(cq // tk):
            sel, room = selected(2 * qd, sub, room)
            shift_ref[qd % 2, :, sub * tk:(sub + 1) * tk] = jnp.where(sel, neg_bound, _NEG_BIG).astype(_MXU_DTYPE)
        return room

    def stage_weights(qd):
        rhs = jnp.concatenate([kT_ref[:, pl.ds(pl.multiple_of(qd * cq, cq), cq)], shift_ref[qd % 2]], axis=0)
        for h in range(N_HEADS):
            p_ref[qd % 2, h] = jnp.exp2(jnp.dot(qa_ref[h], rhs, preferred_element_type=jnp.float32)).astype(_MXU_DTYPE)

    def stage_values(qd):
        vt = v_ref[pl.ds(pl.multiple_of(qd * cq, cq), cq), :]
        for h in range(N_HEADS):
            acc_ref[h] += jnp.dot(p_ref[qd % 2, h], vt, preferred_element_type=jnp.float32)

    def shifted_step(j, room):
        stage_values(j - 1)
        stage_weights(j)
        return stage_mask(jnp.minimum(j + 1, last), room)

    room = stage_mask(0, need)
    stage_weights(0)
    room = stage_mask(jnp.minimum(1, last), room)
    lax.fori_loop(1, n_quads, shifted_step, room)
    stage_values(last)
    denom = acc_ref[0][:, HEAD_DIM:HEAD_DIM + 1]
    for h in range(1, N_HEADS):
        denom = jnp.minimum(denom, acc_ref[h][:, HEAD_DIM:HEAD_DIM + 1])
    l_min = jnp.min(denom)

    @pl.when(l_min < _MIN_DENOM)
    def _():
        m_ref[...] = jnp.full(m_ref.shape, _NEG_BIG, jnp.float32)
        acc_ref[...] = jnp.zeros(acc_ref.shape, jnp.float32)

        def online_pair(j, room):
            off = pl.multiple_of(j * ck, ck)
            biases = []
            for sub in range(2):
                sel, room = selected(j, sub, room)
                biases.append(jnp.where(sel, 0.0, _NEG_BIG))
            bias = jnp.concatenate(biases, axis=1)
            kt = kT_ref[:, pl.ds(off, ck)]
            vt = v_ref[pl.ds(off, ck), :]
            for h in range(N_HEADS):
                s = jnp.dot(q_ref[:, h * HEAD_DIM:(h + 1) * HEAD_DIM], kt, preferred_element_type=jnp.float32) + bias
                m_old = m_ref[h]
                m_new = jnp.maximum(m_old, jnp.max(s, axis=1, keepdims=True))
                alpha = jnp.exp2(m_old - m_new)
                p = jnp.exp2(s - jnp.concatenate([m_new] * (ck // LANES), axis=1))
                acc_ref[h] = (jnp.concatenate([alpha] * (_V_EXT // LANES), axis=1) * acc_ref[h]
                              + jnp.dot(p.astype(_MXU_DTYPE), vt, preferred_element_type=jnp.float32))
                m_ref[h] = m_new
            return room

        lax.fori_loop(0, n_pairs, online_pair, need)

    ys = []
    for h in range(N_HEADS):
        a = acc_ref[h]
        ys.append(a[:, :HEAD_DIM] / a[:, HEAD_DIM:HEAD_DIM + 1])
    o_ref[...] = jnp.concatenate(ys, axis=1).astype(o_ref.dtype)


def _dsa_call(q, qi, wi, kT, kiT, v, kn, n_sel, tk):
    b, s, _ = q.shape
    nblk = s // Q_BLOCK
    assert s % (4 * tk) == 0 and tk % Q_BLOCK == 0
    qspec = lambda w: pl.BlockSpec((None, Q_BLOCK, w), lambda bi, i: (bi, i, 0))
    kern = functools.partial(_dsa_kernel, tk=tk, n_sel=n_sel)
    return pl.pallas_call(
        kern,
        grid=(b, nblk),
        in_specs=[qspec(512), qspec(512), qspec(128),
                  pl.BlockSpec((None, HEAD_DIM, s), lambda bi, i: (bi, 0, 0)),
                  pl.BlockSpec((None, LANES, s), lambda bi, i: (bi, 0, 0)),
                  pl.BlockSpec((None, s, _V_EXT), lambda bi, i: (bi, 0, 0)),
                  pl.BlockSpec((None, 1, s), lambda bi, i: (bi, 0, 0))],
        out_specs=qspec(512),
        out_shape=jax.ShapeDtypeStruct((b, s, 512), _MXU_DTYPE),
        scratch_shapes=[pltpu.VMEM((s // LANES, Q_BLOCK, LANES), jnp.float32),
                        pltpu.VMEM((_LIST_DEPTH, Q_BLOCK, LANES), jnp.float32),
                        pltpu.VMEM((_LIST_DEPTH, LANES, Q_BLOCK), jnp.int32),
                        pltpu.VMEM((Q_BLOCK, 1), jnp.float32),
                        pltpu.VMEM((Q_BLOCK, 1), jnp.float32),
                        pltpu.VMEM((IDX_HEADS, Q_BLOCK, tk), jnp.float32),
                        pltpu.VMEM((N_HEADS, Q_BLOCK, HEAD_DIM + Q_BLOCK), _MXU_DTYPE),
                        pltpu.VMEM((2, Q_BLOCK, 4 * tk), _MXU_DTYPE),
                        pltpu.VMEM((2, N_HEADS, Q_BLOCK, 4 * tk), _MXU_DTYPE),
                        pltpu.VMEM((N_HEADS, Q_BLOCK, LANES), jnp.float32),
                        pltpu.VMEM((N_HEADS, Q_BLOCK, _V_EXT), jnp.float32)],
        compiler_params=pltpu.CompilerParams(dimension_semantics=("arbitrary", "arbitrary"),
                                             vmem_limit_bytes=_VMEM_LIMIT),
        name="dsa",
    )(q, qi, wi, kT, kiT, v, kn)


_TOK_CHUNK = 8 * LANES


def _merge_kernel(x_ref, ma_ref, gb_ref, yb_ref, wb_ref, wo_ref, g2_ref, wq_ref, sk_ref, h_ref, hn_ref, sub_ref):
    tm = x_ref.shape[0]
    merged = ma_ref[...] + gb_ref[...] * jnp.dot(yb_ref[...], wb_ref[...], preferred_element_type=jnp.float32)
    h1 = x_ref[...] + _dot(merged, wo_ref[...])
    h_ref[...] = h1
    hn = _rms(h1, g2_ref[...]).astype(_MXU_DTYPE)
    hn_ref[...] = hn
    qq = jnp.dot(hn, wq_ref[...], preferred_element_type=jnp.float32).astype(_MXU_DTYPE)
    groups = tm // LANES
    first = (pl.program_id(0) % (_TOK_CHUNK // tm)) * groups
    for hp in range(2 * PEER_HEADS):
        tile = _dot_nt(sk_ref[hp], qq[:, hp * PEER_HALF:(hp + 1) * PEER_HALF])
        for g in range(groups):
            sub_ref[hp, pl.ds(first + g, N_KEYS, stride=8), :] = tile[:, g * LANES:(g + 1) * LANES]


def _merge_call(x2, ma, gb, yb, wb, wo, g2, wq, sk, tm):
    t, d = x2.shape
    assert t % _TOK_CHUNK == 0 and _TOK_CHUNK % tm == 0
    full = lambda a: pl.BlockSpec(a.shape, lambda i: (0,) * a.ndim)
    row = lambda w: pl.BlockSpec((tm, w), lambda i: (i, 0))
    steps = _TOK_CHUNK // tm
    sub_shape = (2 * PEER_HEADS, t // _TOK_CHUNK, N_KEYS * 8, LANES)
    return pl.pallas_call(
        _merge_kernel,
        grid=(t // tm,),
        in_specs=[row(d), row(d), row(d), row(512), full(wb), full(wo), full(g2), full(wq), full(sk)],
        out_specs=[row(d), row(d),
                   pl.BlockSpec((2 * PEER_HEADS, None, N_KEYS * 8, LANES), lambda i: (0, i // steps, 0, 0))],
        out_shape=[jax.ShapeDtypeStruct((t, d), jnp.float32), jax.ShapeDtypeStruct((t, d), _MXU_DTYPE),
                   jax.ShapeDtypeStruct(sub_shape, jnp.float32)],
        compiler_params=pltpu.CompilerParams(dimension_semantics=("arbitrary",), vmem_limit_bytes=_VMEM_LIMIT),
        name="merge",
    )(x2, ma, gb, yb, wb, wo, g2, wq, sk)


def _insert_sorted(vals, pays, x, xp, first=0):
    for lvl in range(first, len(vals)):
        c = x > vals[lvl]
        vals[lvl], x = jnp.where(c, x, vals[lvl]), jnp.where(c, vals[lvl], x)
        pays[lvl], xp = jnp.where(c, xp, pays[lvl]), jnp.where(c, pays[lvl], xp)


_PEER_PAIRS = [(a, b) for a in range(PEER_TOPK) for b in range(PEER_TOPK // (a + 1))]
_KEYS_PER_STEP = 16


def _peer_topk_kernel(sub_ref, i_ref, j_ref, g_ref):
    kk = PEER_TOPK
    neg = jnp.full((8, LANES), -jnp.inf, jnp.float32)
    zero = jnp.zeros((8, LANES), jnp.int32)

    def sorted_top(hp):
        def step(k4, carry):
            vals, idxs = list(carry[0]), list(carry[1])
            for u in range(_KEYS_PER_STEP):
                key = k4 * _KEYS_PER_STEP + u
                x = sub_ref[hp, pl.ds(pl.multiple_of(key * 8, 8), 8), :]
                _insert_sorted(vals, idxs, x, zero + key)
            return tuple(vals), tuple(idxs)

        return lax.fori_loop(0, N_KEYS // _KEYS_PER_STEP, step, ((neg,) * kk, (zero,) * kk))

    def head(h, carry):
        s1, i1 = sorted_top(2 * h)
        s2, i2 = sorted_top(2 * h + 1)
        best, expert = [neg] * kk, [zero] * kk
        for a, b in _PEER_PAIRS:
            _insert_sorted(best, expert, s1[a] + s2[b], i1[a] * N_KEYS + i2[b], first=(a + 1) * (b + 1) - 1)
        e = [jnp.exp(v - best[0]) for v in best]
        denom = e[0]
        for v in e[1:]:
            denom = denom + v
        for k in range(kk):
            i_ref[h * kk + k] = expert[k] >> 7
            j_ref[h * kk + k] = expert[k] & (N_KEYS - 1)
            g_ref[h * kk + k] = e[k] / denom
        return carry

    lax.fori_loop(0, PEER_HEADS, head, 0)


def _peer_topk_call(sub):
    hp, chunks, rows, _ = sub.shape
    hk = PEER_HEADS * PEER_TOPK
    out = pl.BlockSpec((hk, 8, LANES), lambda i: (0, i, 0))
    osd = lambda dt: jax.ShapeDtypeStruct((hk, chunks * 8, LANES), dt)
    return pl.pallas_call(
        _peer_topk_kernel,
        grid=(chunks,),
        in_specs=[pl.BlockSpec((hp, None, rows, LANES), lambda i: (0, i, 0, 0))],
        out_specs=[out, out, out],
        out_shape=[osd(jnp.int32), osd(jnp.int32), osd(jnp.float32)],
        compiler_params=pltpu.CompilerParams(dimension_semantics=("arbitrary",), vmem_limit_bytes=_VMEM_LIMIT),
        name="peer_topk",
    )(sub)


def _peer_coef_kernel(i_ref, j_ref, g_ref, m_ref):
    tt = i_ref.shape[0]
    rid = lax.broadcasted_iota(jnp.int32, (N_KEYS, LANES), 0)

    def body(t, carry):
        irow = i_ref[pl.ds(t, 1), :]
        jrow = j_ref[pl.ds(t, 1), :]
        grow = g_ref[pl.ds(t, 1), :]
        rt = jnp.where(rid == irow, grow, 0.0)
        ct = jnp.where(rid == jrow, 1.0, 0.0)
        mt = _dot_nt(rt, ct)
        for e in range(N_KEYS // _PEER_IB):
            m_ref[e, t] = mt[e * _PEER_IB:(e + 1) * _PEER_IB, :]
        return carry

    lax.fori_loop(0, tt, body, 0, unroll=True)


def _peer_coef_call(it, jt, gt, tt):
    t, hk = it.shape
    row = pl.BlockSpec((tt, hk), lambda i: (i, 0))
    nb = N_KEYS // _PEER_IB
    return pl.pallas_call(
        _peer_coef_kernel,
        grid=(t // tt,),
        in_specs=[row, row, row],
        out_specs=pl.BlockSpec((nb, tt, _PEER_IB, N_KEYS), lambda i: (0, i, 0, 0)),
        out_shape=jax.ShapeDtypeStruct((nb, t, _PEER_IB, N_KEYS), jnp.float32),
        compiler_params=pltpu.CompilerParams(dimension_semantics=("arbitrary",), vmem_limit_bytes=_VMEM_LIMIT),
        name="peer_coef",
    )(it, jt, gt)


def _peer_dense_kernel(hn_ref, h_ref, ut_ref, v_ref, m_ref, o_ref, *, ib):
    e = pl.program_id(1)

    @pl.when(e == 0)
    def _():
        o_ref[...] = h_ref[...]

    tm = hn_ref.shape[0]
    a = jax.nn.gelu(jnp.dot(hn_ref[...], ut_ref[...], preferred_element_type=jnp.float32))
    c = jnp.concatenate([a[:, (b * ib + ii) * N_KEYS:(b * ib + ii + 1) * N_KEYS] * m_ref[b, pl.ds(ii, tm, stride=ib), :]
                         for b in range(m_ref.shape[0]) for ii in range(ib)], axis=1)
    o_ref[...] += jnp.dot(c.astype(_MXU_DTYPE), v_ref[...], preferred_element_type=jnp.float32)


def _peer_dense_call(hn, h1, ut, v, m3, tm, ib, nb):
    t, d = hn.shape
    te = nb * ib * N_KEYS
    kern = functools.partial(_peer_dense_kernel, ib=ib)
    return pl.pallas_call(
        kern,
        grid=(t // tm, N_KEYS // (nb * ib)),
        in_specs=[pl.BlockSpec((tm, d), lambda ti, e: (ti, 0)),
                  pl.BlockSpec((tm, d), lambda ti, e: (ti, 0)),
                  pl.BlockSpec((d, te), lambda ti, e: (0, e)),
                  pl.BlockSpec((te, d), lambda ti, e: (e, 0)),
                  pl.BlockSpec((nb, tm * ib, N_KEYS), lambda ti, e: (e, ti, 0))],
        out_specs=pl.BlockSpec((tm, d), lambda ti, e: (ti, 0)),
        out_shape=jax.ShapeDtypeStruct((t, d), jnp.float32),
        compiler_params=pltpu.CompilerParams(dimension_semantics=("arbitrary", "arbitrary"),
                                             vmem_limit_bytes=_VMEM_LIMIT),
        name="peer_dense",
    )(hn, h1, ut, v, m3)


def _rope_inv(rot_dim, period):
    half = rot_dim // 2
    inv = jnp.power(jnp.float32(ROPE_THETA), -jnp.arange(half, dtype=jnp.float32) * 2.0 / rot_dim)
    lane = np.arange(LANES)
    in_rot = (lane % period) < rot_dim
    pat = jnp.where(jnp.asarray(in_rot), inv[jnp.asarray(lane % period % half)], 0.0)
    return pat.reshape(1, LANES).astype(jnp.float32)


def _pack_w_in(w):
    d = w.shape[0]
    z = lambda n: jnp.zeros((d, n), w.dtype)
    segs = [w[:, 0:1536 + KV_LATENT]]
    for h in range(IDX_HEADS):
        segs += [w[:, 1792 + h * IDX_DIM:1792 + (h + 1) * IDX_DIM], z(LANES - IDX_DIM)]
    segs += [w[:, 2048:2116], z(LANES - 68)]
    segs += [w[:, 2116:]]
    out = jnp.concatenate(segs, axis=1)
    assert out.shape[1] == _W_COLS
    return out.astype(_MXU_DTYPE)


def kernel(x, positions, norm1_g, w_in, v_norm_g, v_norm_b, spatial_w, spatial_b, kv_norm_g, w_uk, w_uv,
           q_norm_g, k_norm_g, w_a_out, w_b_out, w_o, norm2_g, peer_wq, peer_subkeys, peer_u, peer_v):
    bsz, s, d = x.shape
    t = bsz * s
    depth = w_in.shape[0]
    mx = _MXU_DTYPE
    n_sel = min(TOPK_MAX, s // 4)
    tm = min(256, t)
    tk = min(256, s // 4)
    inv = _rope_inv(ROT_DIM, LANES) + jnp.roll(_rope_inv(IDX_ROT, LANES), ROT_DIM, axis=1)
    pos = positions.reshape(t, 1).astype(jnp.float32)
    r2 = lambda a: a.reshape(1, -1)
    h = x.reshape(t, d)
    for l in range(depth):
        ma, gb, q, k, v, qi, ki, wi = _inproj_call(
            h, pos, r2(norm1_g[l]), _pack_w_in(w_in[l]), r2(v_norm_g[l]), r2(v_norm_b[l]), spatial_w[l],
            spatial_b[l].T, r2(kv_norm_g[l]), w_uk[l].astype(mx), w_uv[l].astype(mx), r2(q_norm_g[l]),
            r2(k_norm_g[l]), w_a_out[l].astype(mx), inv, tm)
        b3 = lambda a: a.reshape(bsz, s, a.shape[-1])
        kT = b3(k).transpose(0, 2, 1)
        kiT = b3(ki).transpose(0, 2, 1)
        kn = wi[:, _KN_LANE].reshape(bsz, 1, s)
        yb = _dsa_call(b3(q), b3(qi), b3(wi), kT, kiT, b3(v), kn, n_sel, tk).reshape(t, 512)
        sk = peer_subkeys[l].reshape(2 * PEER_HEADS, N_KEYS, PEER_HALF).astype(mx)
        h1, hn, sub = _merge_call(h, ma, gb, yb, w_b_out[l].astype(mx), w_o[l].astype(mx), r2(norm2_g[l]),
                                  peer_wq[l].astype(mx), sk, min(2 * tm, t))
        hk = PEER_HEADS * PEER_TOPK
        ei, ej, eg = (a.reshape(hk, t).T for a in _peer_topk_call(sub))
        m4 = _peer_coef_call(ei, ej, eg, min(256, t))
        m3 = m4.reshape(N_KEYS // _PEER_IB, t * _PEER_IB, N_KEYS)
        h = _peer_dense_call(hn, h1, peer_u[l].T.astype(mx), peer_v[l].astype(mx), m3, min(512, t), _PEER_IB, 2)
    return h.reshape(bsz, s, d)
```

```python
import functools
import math

import jax
import jax.numpy as jnp
import numpy as np
from jax import lax
from jax.experimental import pallas as pl
from jax.experimental.pallas import tpu as pltpu

EPS = 1e-6
ROPE_THETA = 500000.0
CHUNK = 128
A_GROUPS = 4
A_GROUP_DIM = 128
N_HEADS = 4
HEAD_DIM = 128
KV_LATENT = 256
ROT_DIM = HEAD_DIM // 4
IDX_HEADS = 4
IDX_DIM = 64
IDX_ROT = IDX_DIM // 4
TOPK_MAX = 256
Q_BLOCK = 128
PEER_HEADS = 8
PEER_HALF = 128
N_KEYS = 128
PEER_TOPK = 16
LANES = 128
_PEER_IB = 8

_MXU_DTYPE = jnp.bfloat16
_INT_MIN = -2147483648
_KEY_LOWEST_FINITE = -2139095040
_LIST_DEPTH = 12
_MIN_DENOM = 2.0 ** -100
_KN_LANE = IDX_DIM + IDX_HEADS
_NEG_BIG = -1e30
_VMEM_LIMIT = 56 * 1024 * 1024
_V_EXT = 2 * HEAD_DIM


def _dot(a, b):
    return jnp.dot(a.astype(_MXU_DTYPE), b.astype(_MXU_DTYPE), preferred_element_type=jnp.float32)


def _dot_nt(a, b):
    return lax.dot_general(a.astype(_MXU_DTYPE), b.astype(_MXU_DTYPE), (((1,), (1,)), ((), ())),
                           preferred_element_type=jnp.float32)


def _rms(x, g):
    return x * lax.rsqrt(jnp.mean(x * x, axis=-1, keepdims=True) + EPS) * g


def _rope(x, cos_t, sin_lo, sin_hi, half):
    n = x.shape[-1]
    x_up = pltpu.roll(x, n - half, 1)
    x_dn = pltpu.roll(x, half, 1)
    return x * cos_t + x_up * sin_lo + x_dn * sin_hi


def _rope_tables(pos, inv):
    ang = pos * inv
    c = jnp.cos(ang)
    s = jnp.sin(ang)
    lane = lax.broadcasted_iota(jnp.int32, ang.shape, 1)

    def tables(c, s, rot):
        half = rot // 2
        return (jnp.where(lane < rot, c, 1.0), jnp.where(lane < half, -s, 0.0),
                jnp.where((lane >= half) & (lane < rot), s, 0.0))

    back = LANES - ROT_DIM
    return tables(c, s, ROT_DIM), tables(pltpu.roll(c, back, 1), pltpu.roll(s, back, 1), IDX_ROT)


_OFF_U, _OFF_V, _OFF_Q, _OFF_C, _OFF_QI, _OFF_KI, _OFF_G = 0, 512, 1024, 1536, 1792, 2304, 2432
_W_COLS = 2432 + 2048


def _inproj_kernel(x_ref, pos_ref, g1_ref, w_ref, vg_ref, vb_ref, ws_ref, bs_ref, kvg_ref, wuk_ref, wuv_ref,
                   qg_ref, kg_ref, wa_ref, inv_ref,
                   ma_ref, gb_ref, q_ref, k_ref, v_ref, qi_ref, ki_ref, wi_ref):
    tm = x_ref.shape[0]
    x = x_ref[...]
    xn = _rms(x, g1_ref[...]).astype(_MXU_DTYPE)

    def proj(off, width):
        return jnp.dot(xn, w_ref[:, off:off + width], preferred_element_type=jnp.float32)

    u = jax.nn.gelu(proj(_OFF_U, 512))
    v = jax.nn.gelu(proj(_OFF_V, 512))
    mu = jnp.mean(v, axis=-1, keepdims=True)
    vc = v - mu
    v = vc * lax.rsqrt(jnp.mean(vc * vc, axis=-1, keepdims=True) + EPS) * vg_ref[...] + vb_ref[...]
    v = v.astype(_MXU_DTYPE)
    row = lax.broadcasted_iota(jnp.int32, (CHUNK, CHUNK), 0)
    col = lax.broadcasted_iota(jnp.int32, (CHUNK, CHUNK), 1)
    z_chunks = []
    for c in range(tm // CHUNK):
        zg = []
        for g in range(A_GROUPS):
            wt = jnp.where(row >= col, ws_ref[g], 0.0).astype(_MXU_DTYPE)
            vcg = v[c * CHUNK:(c + 1) * CHUNK, g * A_GROUP_DIM:(g + 1) * A_GROUP_DIM]
            zg.append(jnp.dot(wt, vcg, preferred_element_type=jnp.float32) + bs_ref[:, g:g + 1])
        z_chunks.append(jnp.concatenate(zg, axis=1))
    z = jnp.concatenate(z_chunks, axis=0) if len(z_chunks) > 1 else z_chunks[0]
    ya = u * z
    gate_a = jax.nn.sigmoid(proj(_OFF_G, 1024))
    ma_ref[...] = gate_a * _dot(ya, wa_ref[...])
    gb_ref[...] = jax.nn.sigmoid(proj(_OFF_G + 1024, 1024))

    pos = pos_ref[...]
    (cq, sq_lo, sq_hi), (ci, si_lo, si_hi) = _rope_tables(pos, inv_ref[...])

    q = proj(_OFF_Q, 512)
    qg = qg_ref[...]
    qh = [_rms(q[:, h * HEAD_DIM:(h + 1) * HEAD_DIM], qg) for h in range(N_HEADS)]
    qh = [_rope(t, cq, sq_lo, sq_hi, ROT_DIM // 2) for t in qh]
    q_ref[...] = (jnp.concatenate(qh, axis=1) * (HEAD_DIM ** -0.5 * math.log2(math.e))).astype(q_ref.dtype)

    c_n = _rms(proj(_OFF_C, KV_LATENT), kvg_ref[...]).astype(_MXU_DTYPE)
    kk = _rms(jnp.dot(c_n, wuk_ref[...], preferred_element_type=jnp.float32), kg_ref[...])
    k_out = _rope(kk, cq, sq_lo, sq_hi, ROT_DIM // 2).astype(k_ref.dtype)
    k_ref[...] = k_out
    k32 = k_out.astype(jnp.float32)
    k_norm2 = jnp.sum(k32 * k32, axis=-1, keepdims=True)
    vv = jnp.dot(c_n, wuv_ref[...], preferred_element_type=jnp.float32)
    ones_col = jnp.where(lax.broadcasted_iota(jnp.int32, vv.shape, 1) == 0, 1.0, 0.0)
    v_ref[...] = jnp.concatenate([vv, ones_col], axis=1).astype(v_ref.dtype)

    qi = proj(_OFF_QI, 512)
    qis = [_rope(qi[:, h * LANES:(h + 1) * LANES], ci, si_lo, si_hi, IDX_ROT // 2) for h in range(IDX_HEADS)]
    qi_ref[...] = (jnp.concatenate(qis, axis=1) * (IDX_DIM ** -0.5)).astype(qi_ref.dtype)
    kw = proj(_OFF_KI, LANES)
    lane = lax.broadcasted_iota(jnp.int32, kw.shape, 1)
    ki = _rope(jnp.where(lane < IDX_DIM, kw, 0.0), ci, si_lo, si_hi, IDX_ROT // 2)
    ki_ref[...] = ki.astype(ki_ref.dtype)
    wi_ref[...] = jnp.where(lane == _KN_LANE, k_norm2, kw * (IDX_HEADS ** -0.5))


def _inproj_call(x2, pos, g1, w_pack, vg, vb, ws, bs_t, kvg, wuk, wuv, qg, kg, wa, inv, tm):
    t, d = x2.shape
    full = lambda a: pl.BlockSpec(a.shape, lambda i: (0,) * a.ndim)
    row = lambda w: pl.BlockSpec((tm, w), lambda i: (i, 0))
    f32, mx = jnp.float32, _MXU_DTYPE
    outs = [(1024, f32), (1024, f32), (512, mx), (128, mx), (_V_EXT, mx), (512, mx), (128, mx), (128, f32)]
    return pl.pallas_call(
        _inproj_kernel,
        grid=(t // tm,),
        in_specs=[row(d), row(1), full(g1), full(w_pack), full(vg), full(vb), full(ws), full(bs_t), full(kvg),
                  full(wuk), full(wuv), full(qg), full(kg), full(wa), full(inv)],
        out_specs=[row(w) for w, _ in outs],
        out_shape=[jax.ShapeDtypeStruct((t, w), dt) for w, dt in outs],
        compiler_params=pltpu.CompilerParams(dimension_semantics=("arbitrary",), vmem_limit_bytes=_VMEM_LIMIT),
        name="inproj",
    )(x2, pos, g1, w_pack, vg, vb, ws, bs_t, kvg, wuk, wuv, qg, kg, wa, inv)


def _dsa_kernel(q_ref, qi_ref, wi_ref, kT_ref, kiT_ref, v_ref, kn_ref, o_ref,
                sc_ref, list_ref, listT_ref, thr_ref, need_ref, wb_ref, qa_ref, shift_ref, p_ref, m_ref, acc_ref,
                *, tk, n_sel):
    blk = pl.program_id(1)
    nq = Q_BLOCK
    ck = 2 * tk
    n_pairs = (blk * nq + nq + ck - 1) // ck

    wi = wi_ref[...]
    for h in range(IDX_HEADS):
        wb_ref[h] = jnp.broadcast_to(wi[:, IDX_DIM + h:IDX_DIM + h + 1], (nq, tk))

    cq = 2 * ck
    n_quads = (n_pairs + 1) // 2
    tiles_per_pair = ck // LANES
    tiles_per_quad = cq // LANES
    depth = _LIST_DEPTH
    qpos = blk * nq + lax.broadcasted_iota(jnp.int32, (nq, tk), 0)
    lane_pos = lax.broadcasted_iota(jnp.int32, (nq, tk), 1)

    def score_quad(qd):
        for sub in range(cq // tk):
            off = pl.multiple_of(qd * cq + sub * tk, tk)
            kt = kiT_ref[:, pl.ds(off, tk)]
            sc = None
            for h in range(IDX_HEADS):
                lg = jnp.dot(qi_ref[:, h * LANES:(h + 1) * LANES], kt, preferred_element_type=jnp.float32)
                term = wb_ref[h] * jnp.maximum(lg, 0.0)
                sc = term if sc is None else sc + term
            sc = jnp.where(off + lane_pos <= qpos, sc, -jnp.inf)
            for c in range(tk // LANES):
                sc_ref[qd * tiles_per_quad + sub * (tk // LANES) + c] = sc[:, c * LANES:(c + 1) * LANES]

    def insert_quad(qd):
        for g in range(nq // 8):
            rows = pl.ds(g * 8, 8)
            lists = [list_ref[i, rows, :] for i in range(depth)]
            for c in range(tiles_per_quad):
                x = sc_ref[qd * tiles_per_quad + c, rows, :]
                for i in range(depth):
                    lists[i], x = jnp.maximum(lists[i], x), jnp.minimum(lists[i], x)
            for i in range(depth):
                list_ref[i, rows, :] = lists[i]

    def score_step(qd, carry):
        score_quad(qd)
        insert_quad(qd - 1)
        return carry

    list_ref[...] = jnp.full(list_ref.shape, -jnp.inf, jnp.float32)
    score_quad(0)
    lax.fori_loop(1, n_quads, score_step, 0)
    insert_quad(n_quads - 1)


    def to_key(x):
        bits = pltpu.bitcast(x, jnp.int32)
        return bits ^ ((bits >> 31) & 0x7FFFFFFF)

    def to_score(key):
        return pltpu.bitcast(key ^ ((key >> 31) & 0x7FFFFFFF), jnp.float32)

    def kth_largest(count_ge, shape):
        zero = jnp.zeros(shape, jnp.int32)
        base = jnp.where(count_ge(zero) >= n_sel, zero, _INT_MIN)

        def bit_step(b, base):
            cand = base | jnp.left_shift(jnp.int32(1), 30 - b)
            return jnp.where(count_ge(cand) >= n_sel, cand, base)

        thr = lax.fori_loop(0, 31, bit_step, base)
        return jnp.maximum(thr, _KEY_LOWEST_FINITE)

    for i in range(depth):
        listT_ref[i] = to_key(list_ref[i].T)

    def pick(bits, leaves):
        if not bits:
            return leaves[0]
        half = len(leaves) // 2
        return jnp.where(bits[0], pick(bits[1:], leaves[half:]), pick(bits[1:], leaves[:half]))

    def count_lists(cand_row):
        rows = LANES // 8
        tot = None
        for g in range(rows):
            sl = pl.ds(g * 8, 8)
            levels = [listT_ref[i, sl, :] for i in range(depth)]
            floor = jnp.full((8, nq), _INT_MIN, jnp.int32)
            bits, cnt = [], None
            n_steps = depth.bit_length()
            for k in range(n_steps):
                step = 1 << (n_steps - 1 - k)
                leaves = []
                for r_hi in range(1 << k):
                    idx = r_hi * 2 * step + step - 1
                    leaves.append(levels[idx] if idx < depth else floor)
                hit = pick(bits, leaves) >= cand_row
                bits.append(hit)
                add = jnp.where(hit, step, 0)
                cnt = add if cnt is None else cnt + add
            tot = cnt if tot is None else tot + cnt
        return jnp.sum(tot, axis=0, keepdims=True)

    thr_row = kth_largest(count_lists, (1, nq))
    need_row = n_sel - count_lists(thr_row + 1)
    thr_ref[...] = to_score(jnp.broadcast_to(thr_row, (nq, nq))).T[:, 0:1]
    need_ref[...] = jnp.broadcast_to(need_row, (nq, nq)).astype(jnp.float32).T[:, 0:1]
    overflow = jnp.max(jnp.where(listT_ref[depth - 1] > thr_row, 1, 0))

    @pl.when(overflow > 0)
    def _():
        def count_all(cand):
            cb = jnp.broadcast_to(cand, (nq, LANES))

            def body(j, cnt):
                for c in range(tiles_per_pair):
                    cnt = cnt + (to_key(sc_ref[j * tiles_per_pair + c]) >= cb).astype(jnp.int32)
                return cnt

            cnt = lax.fori_loop(0, n_pairs, body, jnp.zeros((nq, LANES), jnp.int32))
            return jnp.sum(cnt, axis=1, keepdims=True)

        thr_col = kth_largest(count_all, (nq, 1))
        thr_ref[...] = to_score(thr_col)
        need_ref[...] = (n_sel - count_all(thr_col + 1)).astype(jnp.float32)

    thr = jnp.broadcast_to(thr_ref[...], (nq, tk))
    need = need_ref[...]

    tri = (lax.broadcasted_iota(jnp.int32, (tk, tk), 0) <= lax.broadcasted_iota(jnp.int32, (tk, tk), 1))
    tri = jnp.where(tri, 1.0, 0.0).astype(_MXU_DTYPE)

    def selected(j, sub, room):
        t0 = (j * ck + sub * tk) // LANES
        sc = jnp.concatenate([sc_ref[t0 + c] for c in range(tk // LANES)], axis=1)
        eq = sc == thr
        pre = jnp.dot(jnp.where(eq, 1.0, 0.0).astype(_MXU_DTYPE), tri, preferred_element_type=jnp.float32)
        return (sc > thr) | (eq & (pre <= room)), room - pre[:, tk - 1:tk]

    q32 = q_ref[...].astype(jnp.float32)
    qn2 = None
    for h in range(N_HEADS):
        qh = q32[:, h * HEAD_DIM:(h + 1) * HEAD_DIM]
        n2 = jnp.sum(qh * qh, axis=1, keepdims=True)
        qn2 = n2 if qn2 is None else jnp.maximum(qn2, n2)
    kn2 = jnp.max(kn_ref[...], axis=1, keepdims=True)
    neg_bound = jnp.broadcast_to(-jnp.sqrt(qn2 * kn2), (nq, tk))
    eye = (lax.broadcasted_iota(jnp.int32, (nq, nq), 0) == lax.broadcasted_iota(jnp.int32, (nq, nq), 1))
    eye = jnp.where(eye, 1.0, 0.0).astype(_MXU_DTYPE)
    for h in range(N_HEADS):
        qa_ref[h] = jnp.concatenate([q_ref[:, h * HEAD_DIM:(h + 1) * HEAD_DIM], eye], axis=1)
    acc_ref[...] = jnp.zeros(acc_ref.shape, jnp.float32)

    last = n_quads - 1

    def stage_mask(qd, room):
        for sub in range(cq // tk):
            sel, room = selected(2 * qd, sub, room)
            shift_ref[qd % 2, :, sub * tk:(sub + 1) * tk] = jnp.where(sel, neg_bound, _NEG_BIG).astype(_MXU_DTYPE)
        return room

    def stage_weights(qd):
        rhs = jnp.concatenate([kT_ref[:, pl.ds(pl.multiple_of(qd * cq, cq), cq)], shift_ref[qd % 2]], axis=0)
        for h in range(N_HEADS):
            p_ref[qd % 2, h] = jnp.exp2(jnp.dot(qa_ref[h], rhs, preferred_element_type=jnp.float32)).astype(_MXU_DTYPE)

    def stage_values(qd):
        vt = v_ref[pl.ds(pl.multiple_of(qd * cq, cq), cq), :]
        for h in range(N_HEADS):
            acc_ref[h] += jnp.dot(p_ref[qd % 2, h], vt, preferred_element_type=jnp.float32)

    def shifted_step(j, room):
        stage_values(j - 1)
        stage_weights(j)
        return stage_mask(jnp.minimum(j + 1, last), room)

    room = stage_mask(0, need)
    stage_weights(0)
    room = stage_mask(jnp.minimum(1, last), room)
    lax.fori_loop(1, n_quads, shifted_step, room)
    stage_values(last)
    denom = acc_ref[0][:, HEAD_DIM:HEAD_DIM + 1]
    for h in range(1, N_HEADS):
        denom = jnp.minimum(denom, acc_ref[h][:, HEAD_DIM:HEAD_DIM + 1])
    l_min = jnp.min(denom)

    @pl.when(l_min < _MIN_DENOM)
    def _():
        m_ref[...] = jnp.full(m_ref.shape, _NEG_BIG, jnp.float32)
        acc_ref[...] = jnp.zeros(acc_ref.shape, jnp.float32)

        def online_pair(j, room):
            off = pl.multiple_of(j * ck, ck)
            biases = []
            for sub in range(2):
                sel, room = selected(j, sub, room)
                biases.append(jnp.where(sel, 0.0, _NEG_BIG))
            bias = jnp.concatenate(biases, axis=1)
            kt = kT_ref[:, pl.ds(off, ck)]
            vt = v_ref[pl.ds(off, ck), :]
            for h in range(N_HEADS):
                s = jnp.dot(q_ref[:, h * HEAD_DIM:(h + 1) * HEAD_DIM], kt, preferred_element_type=jnp.float32) + bias
                m_old = m_ref[h]
                m_new = jnp.maximum(m_old, jnp.max(s, axis=1, keepdims=True))
                alpha = jnp.exp2(m_old - m_new)
                p = jnp.exp2(s - jnp.concatenate([m_new] * (ck // LANES), axis=1))
                acc_ref[h] = (jnp.concatenate([alpha] * (_V_EXT // LANES), axis=1) * acc_ref[h]
                              + jnp.dot(p.astype(_MXU_DTYPE), vt, preferred_element_type=jnp.float32))
                m_ref[h] = m_new
            return room

        lax.fori_loop(0, n_pairs, online_pair, need)

    ys = []
    for h in range(N_HEADS):
        a = acc_ref[h]
        ys.append(a[:, :HEAD_DIM] / a[:, HEAD_DIM:HEAD_DIM + 1])
    o_ref[...] = jnp.concatenate(ys, axis=1).astype(o_ref.dtype)


def _dsa_call(q, qi, wi, kT, kiT, v, kn, n_sel, tk):
    b, s, _ = q.shape
    nblk = s // Q_BLOCK
    assert s % (4 * tk) == 0 and tk % Q_BLOCK == 0
    qspec = lambda w: pl.BlockSpec((None, Q_BLOCK, w), lambda bi, i: (bi, i, 0))
    kern = functools.partial(_dsa_kernel, tk=tk, n_sel=n_sel)
    return pl.pallas_call(
        kern,
        grid=(b, nblk),
        in_specs=[qspec(512), qspec(512), qspec(128),
                  pl.BlockSpec((None, HEAD_DIM, s), lambda bi, i: (bi, 0, 0)),
                  pl.BlockSpec((None, LANES, s), lambda bi, i: (bi, 0, 0)),
                  pl.BlockSpec((None, s, _V_EXT), lambda bi, i: (bi, 0, 0)),
                  pl.BlockSpec((None, 1, s), lambda bi, i: (bi, 0, 0))],
        out_specs=qspec(512),
        out_shape=jax.ShapeDtypeStruct((b, s, 512), _MXU_DTYPE),
        scratch_shapes=[pltpu.VMEM((s // LANES, Q_BLOCK, LANES), jnp.float32),
                        pltpu.VMEM((_LIST_DEPTH, Q_BLOCK, LANES), jnp.float32),
                        pltpu.VMEM((_LIST_DEPTH, LANES, Q_BLOCK), jnp.int32),
                        pltpu.VMEM((Q_BLOCK, 1), jnp.float32),
                        pltpu.VMEM((Q_BLOCK, 1), jnp.float32),
                        pltpu.VMEM((IDX_HEADS, Q_BLOCK, tk), jnp.float32),
                        pltpu.VMEM((N_HEADS, Q_BLOCK, HEAD_DIM + Q_BLOCK), _MXU_DTYPE),
                        pltpu.VMEM((2, Q_BLOCK, 4 * tk), _MXU_DTYPE),
                        pltpu.VMEM((2, N_HEADS, Q_BLOCK, 4 * tk), _MXU_DTYPE),
                        pltpu.VMEM((N_HEADS, Q_BLOCK, LANES), jnp.float32),
                        pltpu.VMEM((N_HEADS, Q_BLOCK, _V_EXT), jnp.float32)],
        compiler_params=pltpu.CompilerParams(dimension_semantics=("arbitrary", "arbitrary"),
                                             vmem_limit_bytes=_VMEM_LIMIT),
        name="dsa",
    )(q, qi, wi, kT, kiT, v, kn)


_TOK_CHUNK = 8 * LANES


def _merge_kernel(x_ref, ma_ref, gb_ref, yb_ref, wb_ref, wo_ref, g2_ref, wq_ref, sk_ref, h_ref, hn_ref, sub_ref):
    tm = x_ref.shape[0]
    merged = ma_ref[...] + gb_ref[...] * jnp.dot(yb_ref[...], wb_ref[...], preferred_element_type=jnp.float32)
    h1 = x_ref[...] + _dot(merged, wo_ref[...])
    h_ref[...] = h1
    hn = _rms(h1, g2_ref[...]).astype(_MXU_DTYPE)
    hn_ref[...] = hn
    qq = jnp.dot(hn, wq_ref[...], preferred_element_type=jnp.float32).astype(_MXU_DTYPE)
    groups = tm // LANES
    first = (pl.program_id(0) % (_TOK_CHUNK // tm)) * groups
    for hp in range(2 * PEER_HEADS):
        tile = _dot_nt(sk_ref[hp], qq[:, hp * PEER_HALF:(hp + 1) * PEER_HALF])
        for g in range(groups):
            sub_ref[hp, pl.ds(first + g, N_KEYS, stride=8), :] = tile[:, g * LANES:(g + 1) * LANES]


def _merge_call(x2, ma, gb, yb, wb, wo, g2, wq, sk, tm):
    t, d = x2.shape
    assert t % _TOK_CHUNK == 0 and _TOK_CHUNK % tm == 0
    full = lambda a: pl.BlockSpec(a.shape, lambda i: (0,) * a.ndim)
    row = lambda w: pl.BlockSpec((tm, w), lambda i: (i, 0))
    steps = _TOK_CHUNK // tm
    sub_shape = (2 * PEER_HEADS, t // _TOK_CHUNK, N_KEYS * 8, LANES)
    return pl.pallas_call(
        _merge_kernel,
        grid=(t // tm,),
        in_specs=[row(d), row(d), row(d), row(512), full(wb), full(wo), full(g2), full(wq), full(sk)],
        out_specs=[row(d), row(d),
                   pl.BlockSpec((2 * PEER_HEADS, None, N_KEYS * 8, LANES), lambda i: (0, i // steps, 0, 0))],
        out_shape=[jax.ShapeDtypeStruct((t, d), jnp.float32), jax.ShapeDtypeStruct((t, d), _MXU_DTYPE),
                   jax.ShapeDtypeStruct(sub_shape, jnp.float32)],
        compiler_params=pltpu.CompilerParams(dimension_semantics=("arbitrary",), vmem_limit_bytes=_VMEM_LIMIT),
        name="merge",
    )(x2, ma, gb, yb, wb, wo, g2, wq, sk)


def _insert_sorted(vals, pays, x, xp, first=0):
    for lvl in range(first, len(vals)):
        c = x > vals[lvl]
        vals[lvl], x = jnp.where(c, x, vals[lvl]), jnp.where(c, vals[lvl], x)
        pays[lvl], xp = jnp.where(c, xp, pays[lvl]), jnp.where(c, pays[lvl], xp)


_PEER_PAIRS = [(a, b) for a in range(PEER_TOPK) for b in range(PEER_TOPK // (a + 1))]
_KEYS_PER_STEP = 32


def _peer_topk_kernel(sub_ref, i_ref, j_ref, g_ref):
    kk = PEER_TOPK
    neg = jnp.full((8, LANES), -jnp.inf, jnp.float32)
    zero = jnp.zeros((8, LANES), jnp.int32)

    def sorted_top(hp):
        def step(k4, carry):
            vals, idxs = list(carry[0]), list(carry[1])
            for u in range(_KEYS_PER_STEP):
                key = k4 * _KEYS_PER_STEP + u
                x = sub_ref[hp, pl.ds(pl.multiple_of(key * 8, 8), 8), :]
                _insert_sorted(vals, idxs, x, zero + key)
            return tuple(vals), tuple(idxs)

        return lax.fori_loop(0, N_KEYS // _KEYS_PER_STEP, step, ((neg,) * kk, (zero,) * kk))

    def head(h, carry):
        s1, i1 = sorted_top(2 * h)
        s2, i2 = sorted_top(2 * h + 1)
        best, expert = [neg] * kk, [zero] * kk
        for a, b in _PEER_PAIRS:
            _insert_sorted(best, expert, s1[a] + s2[b], i1[a] * N_KEYS + i2[b], first=(a + 1) * (b + 1) - 1)
        e = [jnp.exp(v - best[0]) for v in best]
        denom = e[0]
        for v in e[1:]:
            denom = denom + v
        for k in range(kk):
            i_ref[h * kk + k] = expert[k] >> 7
            j_ref[h * kk + k] = expert[k] & (N_KEYS - 1)
            g_ref[h * kk + k] = e[k] / denom
        return carry

    lax.fori_loop(0, PEER_HEADS, head, 0)


def _peer_topk_call(sub):
    hp, chunks, rows, _ = sub.shape
    hk = PEER_HEADS * PEER_TOPK
    out = pl.BlockSpec((hk, 8, LANES), lambda i: (0, i, 0))
    osd = lambda dt: jax.ShapeDtypeStruct((hk, chunks * 8, LANES), dt)
    return pl.pallas_call(
        _peer_topk_kernel,
        grid=(chunks,),
        in_specs=[pl.BlockSpec((hp, None, rows, LANES), lambda i: (0, i, 0, 0))],
        out_specs=[out, out, out],
        out_shape=[osd(jnp.int32), osd(jnp.int32), osd(jnp.float32)],
        compiler_params=pltpu.CompilerParams(dimension_semantics=("arbitrary",), vmem_limit_bytes=_VMEM_LIMIT),
        name="peer_topk",
    )(sub)


def _peer_coef_kernel(i_ref, j_ref, g_ref, m_ref):
    tt = i_ref.shape[0]
    rid = lax.broadcasted_iota(jnp.int32, (N_KEYS, LANES), 0)

    def body(t, carry):
        irow = i_ref[pl.ds(t, 1), :]
        jrow = j_ref[pl.ds(t, 1), :]
        grow = g_ref[pl.ds(t, 1), :]
        rt = jnp.where(rid == irow, grow, 0.0)
        ct = jnp.where(rid == jrow, 1.0, 0.0)
        mt = _dot_nt(rt, ct)
        for e in range(N_KEYS // _PEER_IB):
            m_ref[e, t] = mt[e * _PEER_IB:(e + 1) * _PEER_IB, :]
        return carry

    lax.fori_loop(0, tt, body, 0, unroll=True)


def _peer_coef_call(it, jt, gt, tt):
    t, hk = it.shape
    row = pl.BlockSpec((tt, hk), lambda i: (i, 0))
    nb = N_KEYS // _PEER_IB
    return pl.pallas_call(
        _peer_coef_kernel,
        grid=(t // tt,),
        in_specs=[row, row, row],
        out_specs=pl.BlockSpec((nb, tt, _PEER_IB, N_KEYS), lambda i: (0, i, 0, 0)),
        out_shape=jax.ShapeDtypeStruct((nb, t, _PEER_IB, N_KEYS), jnp.float32),
        compiler_params=pltpu.CompilerParams(dimension_semantics=("arbitrary",), vmem_limit_bytes=_VMEM_LIMIT),
        name="peer_coef",
    )(it, jt, gt)


def _peer_dense_kernel(hn_ref, h_ref, ut_ref, v_ref, m_ref, o_ref, *, ib):
    e = pl.program_id(1)

    @pl.when(e == 0)
    def _():
        o_ref[...] = h_ref[...]

    tm = hn_ref.shape[0]
    a = jax.nn.gelu(jnp.dot(hn_ref[...], ut_ref[...], preferred_element_type=jnp.float32))
    c = jnp.concatenate([a[:, (b * ib + ii) * N_KEYS:(b * ib + ii + 1) * N_KEYS] * m_ref[b, pl.ds(ii, tm, stride=ib), :]
                         for b in range(m_ref.shape[0]) for ii in range(ib)], axis=1)
    o_ref[...] += jnp.dot(c.astype(_MXU_DTYPE), v_ref[...], preferred_element_type=jnp.float32)


def _peer_dense_call(hn, h1, ut, v, m3, tm, ib, nb):
    t, d = hn.shape
    te = nb * ib * N_KEYS
    kern = functools.partial(_peer_dense_kernel, ib=ib)
    return pl.pallas_call(
        kern,
        grid=(t // tm, N_KEYS // (nb * ib)),
        in_specs=[pl.BlockSpec((tm, d), lambda ti, e: (ti, 0)),
                  pl.BlockSpec((tm, d), lambda ti, e: (ti, 0)),
                  pl.BlockSpec((d, te), lambda ti, e: (0, e)),
                  pl.BlockSpec((te, d), lambda ti, e: (e, 0)),
                  pl.BlockSpec((nb, tm * ib, N_KEYS), lambda ti, e: (e, ti, 0))],
        out_specs=pl.BlockSpec((tm, d), lambda ti, e: (ti, 0)),
        out_shape=jax.ShapeDtypeStruct((t, d), jnp.float32),
        compiler_params=pltpu.CompilerParams(dimension_semantics=("arbitrary", "arbitrary"),
                                             vmem_limit_bytes=_VMEM_LIMIT),
        name="peer_dense",
    )(hn, h1, ut, v, m3)


def _rope_inv(rot_dim, period):
    half = rot_dim // 2
    inv = jnp.power(jnp.float32(ROPE_THETA), -jnp.arange(half, dtype=jnp.float32) * 2.0 / rot_dim)
    lane = np.arange(LANES)
    in_rot = (lane % period) < rot_dim
    pat = jnp.where(jnp.asarray(in_rot), inv[jnp.asarray(lane % period % half)], 0.0)
    return pat.reshape(1, LANES).astype(jnp.float32)


def _pack_w_in(w):
    d = w.shape[0]
    z = lambda n: jnp.zeros((d, n), w.dtype)
    segs = [w[:, 0:1536 + KV_LATENT]]
    for h in range(IDX_HEADS):
        segs += [w[:, 1792 + h * IDX_DIM:1792 + (h + 1) * IDX_DIM], z(LANES - IDX_DIM)]
    segs += [w[:, 2048:2116], z(LANES - 68)]
    segs += [w[:, 2116:]]
    out = jnp.concatenate(segs, axis=1)
    assert out.shape[1] == _W_COLS
    return out.astype(_MXU_DTYPE)


def kernel(x, positions, norm1_g, w_in, v_norm_g, v_norm_b, spatial_w, spatial_b, kv_norm_g, w_uk, w_uv,
           q_norm_g, k_norm_g, w_a_out, w_b_out, w_o, norm2_g, peer_wq, peer_subkeys, peer_u, peer_v):
    bsz, s, d = x.shape
    t = bsz * s
    depth = w_in.shape[0]
    mx = _MXU_DTYPE
    n_sel = min(TOPK_MAX, s // 4)
    tm = min(256, t)
    tk = min(256, s // 4)
    inv = _rope_inv(ROT_DIM, LANES) + jnp.roll(_rope_inv(IDX_ROT, LANES), ROT_DIM, axis=1)
    pos = positions.reshape(t, 1).astype(jnp.float32)
    r2 = lambda a: a.reshape(1, -1)
    h = x.reshape(t, d)
    for l in range(depth):
        ma, gb, q, k, v, qi, ki, wi = _inproj_call(
            h, pos, r2(norm1_g[l]), _pack_w_in(w_in[l]), r2(v_norm_g[l]), r2(v_norm_b[l]), spatial_w[l],
            spatial_b[l].T, r2(kv_norm_g[l]), w_uk[l].astype(mx), w_uv[l].astype(mx), r2(q_norm_g[l]),
            r2(k_norm_g[l]), w_a_out[l].astype(mx), inv, tm)
        b3 = lambda a: a.reshape(bsz, s, a.shape[-1])
        kT = b3(k).transpose(0, 2, 1)
        kiT = b3(ki).transpose(0, 2, 1)
        kn = wi[:, _KN_LANE].reshape(bsz, 1, s)
        yb = _dsa_call(b3(q), b3(qi), b3(wi), kT, kiT, b3(v), kn, n_sel, tk).reshape(t, 512)
        sk = peer_subkeys[l].reshape(2 * PEER_HEADS, N_KEYS, PEER_HALF).astype(mx)
        h1, hn, sub = _merge_call(h, ma, gb, yb, w_b_out[l].astype(mx), w_o[l].astype(mx), r2(norm2_g[l]),
                                  peer_wq[l].astype(mx), sk, min(2 * tm, t))
        hk = PEER_HEADS * PEER_TOPK
        ei, ej, eg = (a.reshape(hk, t).T for a in _peer_topk_call(sub))
        m4 = _peer_coef_call(ei, ej, eg, min(256, t))
        m3 = m4.reshape(N_KEYS // _PEER_IB, t * _PEER_IB, N_KEYS)
        h = _peer_dense_call(hn, h1, peer_u[l].T.astype(mx), peer_v[l].astype(mx), m3, min(1024, t), _PEER_IB, 1)
    return h.reshape(bsz, s, d)
```
